```python
import math
import jax, jax.numpy as jnp
from jax import lax
import numpy as np

D_MODEL = 1024
BATCH = 2
SEQ = 8192
DEPTH = 1
DEC_BATCH = 32
DEC_SEQ = 1
PAST_LEN = 8192
PAGE_SIZE = 128

D_FF = 2816
GLA_HEADS = 4
GLA_DK = D_MODEL // 2
GLA_DV = D_MODEL
GLA_HDK = GLA_DK // GLA_HEADS
GLA_HDV = GLA_DV // GLA_HEADS
GLA_RANK = 16
GLA_TAU = 16.0
GLA_CHUNK = 64
MOBA_HEADS = 8
MOBA_HD = 128
MOBA_W = MOBA_HEADS * MOBA_HD
MOBA_BLOCK = 256
MOBA_TOPK = 3
MOBA_QBLOCK = 32
ROPE_THETA = 10000.0
EPS = 1e-6
IN_SPLITS = (GLA_DK, GLA_DK, GLA_DV, GLA_DV, GLA_RANK, MOBA_W, MOBA_W, MOBA_W, D_MODEL, D_MODEL)
D_IN = sum(IN_SPLITS)

kernel_name = "macaron_gla_moba_hybrid_step"


def rms_norm(x, g):
    xf = x.astype(jnp.float32)
    y = xf * lax.rsqrt(jnp.mean(xf * xf, axis=-1, keepdims=True) + EPS)
    return (y * g.astype(jnp.float32)).astype(x.dtype)


def swiglu(h, wg, wu, wd):
    return (jax.nn.silu(h @ wg) * (h @ wu)) @ wd


def rope(x, pos):
    half = x.shape[-1] // 2
    inv = ROPE_THETA ** (-jnp.arange(half, dtype=jnp.float32) / half)
    ang = pos.astype(jnp.float32)[:, None] * inv[None, :]
    cos = jnp.cos(ang)[None, :, None, :]
    sin = jnp.sin(ang)[None, :, None, :]
    xf = x.astype(jnp.float32)
    x1, x2 = xf[..., :half], xf[..., half:]
    return jnp.concatenate([x1 * cos - x2 * sin, x2 * cos + x1 * sin], axis=-1).astype(x.dtype)


def split_cols(t):
    outs = []
    off = 0
    for s in IN_SPLITS:
        outs.append(t[..., off:off + s])
        off += s
    return outs


def gla_recurrence(q, k, v, log_a, s0):
    B, T, H, _ = q.shape
    c = math.gcd(T, GLA_CHUNK)
    n = T // c

    def chunks(t):
        return t.astype(jnp.float32).reshape(B, n, c, H, t.shape[-1]).transpose(1, 0, 3, 2, 4)

    causal = jnp.tril(jnp.ones((c, c), dtype=bool))

    def step(s, inp):
        qc, kc, vc, ac = inp
        b = jnp.cumsum(ac, axis=2)
        b_last = b[:, :, -1:, :]
        inter = jnp.einsum('bhid,bhde->bhie', qc * jnp.exp(b), s)
        diff = jnp.where(causal[None, None, :, :, None],
                         b[:, :, :, None, :] - b[:, :, None, :, :], -jnp.inf)
        attn = jnp.einsum('bhid,bhjd,bhijd->bhij', qc, kc, jnp.exp(diff))
        intra = jnp.einsum('bhij,bhje->bhie', attn, vc)
        s_new = jnp.exp(b_last[:, :, 0, :])[..., None] * s + \
            jnp.einsum('bhjd,bhje->bhde', kc * jnp.exp(b_last - b), vc)
        return s_new, inter + intra

    s_fin, o = lax.scan(step, s0.astype(jnp.float32), (chunks(q), chunks(k), chunks(v), chunks(log_a)))
    o = o.transpose(1, 0, 3, 2, 4).reshape(B, T, H, v.shape[-1])
    return o, s_fin


def moba_attend(q, k_all, v_all, pos):
    B, T, H, hd = q.shape
    nb = k_all.shape[1] // MOBA_BLOCK
    kb = k_all.reshape(B, nb, MOBA_BLOCK, H, hd)
    vb = v_all.reshape(B, nb, MOBA_BLOCK, H, hd)
    means = jnp.mean(kb.astype(jnp.float32), axis=2)
    n_top = min(MOBA_TOPK, nb)
    qb = math.gcd(T, MOBA_QBLOCK)
    nq = T // qb
    qs = q.reshape(B, nq, qb, H, hd).transpose(1, 0, 2, 3, 4)
    ps = pos.reshape(nq, qb)
    bi = jnp.arange(B)[:, None, None, None]
    hi = jnp.arange(H)[None, None, :, None]
    slot = jnp.arange(n_top + 1)
    offs = jnp.arange(MOBA_BLOCK)
    scale = MOBA_HD ** -0.5

    def attend_block(args):
        qc, pc = args
        own = pc // MOBA_BLOCK
        qf = qc.astype(jnp.float32)
        gate = jnp.einsum('bqhd,bnhd->bqhn', qf, means)
        fully_past = jnp.arange(nb)[None, :] < own[:, None]
        gate = jnp.where(fully_past[None, :, None, :], gate, -jnp.inf)
        _, top = lax.top_k(gate, n_top)
        idx = jnp.concatenate([top, jnp.broadcast_to(own[None, :, None, None], (B, qb, H, 1))], axis=-1)
        ks = kb[bi, idx, :, hi, :].astype(jnp.float32)
        vs = vb[bi, idx, :, hi, :].astype(jnp.float32)
        slot_ok = jnp.where(slot[None, :] < n_top, slot[None, :] < own[:, None], True)
        key_pos = idx[..., None] * MOBA_BLOCK + offs
        valid = slot_ok[None, :, None, :, None] & (key_pos <= pc[None, :, None, None, None])
        s = jnp.einsum('bqhd,bqhskd->bqhsk', qf, ks) * scale
        s = jnp.where(valid, s, -jnp.inf).reshape(B, qb, H, -1)
        p = jax.nn.softmax(s, axis=-1).reshape(B, qb, H, n_top + 1, MOBA_BLOCK)
        return jnp.einsum('bqhsk,bqhskd->bqhd', p, vs).astype(q.dtype)

    out = lax.map(attend_block, (qs, ps))
    return out.transpose(1, 0, 2, 3, 4).reshape(B, T, H, hd)


def decoder_layer(x, pos0, s0, past_k, past_v, p):
    B, T, _ = x.shape
    pos = pos0 + jnp.arange(T, dtype=jnp.int32)
    x = x + 0.5 * swiglu(rms_norm(x, p['ffn1_g']), p['ffn1_wg'], p['ffn1_wu'], p['ffn1_wd'])
    h = rms_norm(x, p['mix_g'])
    gq, gk, gv, gg, glr, mq, mk, mv, gate_a, gate_b = split_cols(h @ p['w_in'])
    log_a = jax.nn.log_sigmoid((glr @ p['w_a2'] + p['b_a']).astype(jnp.float32)) / GLA_TAU
    o_gla, s_new = gla_recurrence(
        gq.reshape(B, T, GLA_HEADS, GLA_HDK) * (GLA_HDK ** -0.5),
        gk.reshape(B, T, GLA_HEADS, GLA_HDK),
        gv.reshape(B, T, GLA_HEADS, GLA_HDV),
        log_a.reshape(B, T, GLA_HEADS, GLA_HDK), s0)
    o_gla = rms_norm(o_gla, p['gla_norm_g']).astype(x.dtype) * jax.nn.silu(gg.reshape(B, T, GLA_HEADS, GLA_HDV))
    o_gla = o_gla.reshape(B, T, GLA_DV)
    q = rope(rms_norm(mq.reshape(B, T, MOBA_HEADS, MOBA_HD), p['q_norm_g']), pos)
    k = rope(rms_norm(mk.reshape(B, T, MOBA_HEADS, MOBA_HD), p['k_norm_g']), pos)
    v = mv.reshape(B, T, MOBA_HEADS, MOBA_HD)
    L = pos0 + T
    Lp = -(-L // MOBA_BLOCK) * MOBA_BLOCK
    pad = jnp.zeros((B, Lp - L, MOBA_HEADS, MOBA_HD), k.dtype)
    if past_k is None:
        k_all = jnp.concatenate([k, pad], axis=1)
        v_all = jnp.concatenate([v, pad.astype(v.dtype)], axis=1)
    else:
        k_all = jnp.concatenate([past_k.astype(k.dtype), k, pad], axis=1)
        v_all = jnp.concatenate([past_v.astype(v.dtype), v, pad.astype(v.dtype)], axis=1)
    o_moba = moba_attend(q, k_all, v_all, pos).reshape(B, T, MOBA_W)
    mix = jax.nn.sigmoid(gate_a) * (o_gla @ p['w_pa']) + jax.nn.sigmoid(gate_b) * (o_moba @ p['w_pb'])
    x = x + mix @ p['w_o']
    x = x + 0.5 * swiglu(rms_norm(x, p['ffn2_g']), p['ffn2_wg'], p['ffn2_wu'], p['ffn2_wd'])
    return x, k, v, s_new


def setup_inputs(seed: int = 0) -> dict:
    key = jax.random.key(seed)
    ks = jax.random.split(key, 32)
    f32 = jnp.float32

    def w(k, shape, fan_in):
        return jax.random.normal(k, (DEPTH,) + shape, f32) * fan_in ** -0.5

    def gain(k, n):
        return 1.0 + 0.1 * jax.random.normal(k, (DEPTH, n), f32)

    n_pages = PAST_LEN // PAGE_SIZE
    n_used = DEC_BATCH * n_pages
    n_pool = n_used + max(1, n_used // 4)
    perm = jax.random.permutation(ks[0], n_pool)
    page_table = perm[:n_used].reshape(DEC_BATCH, n_pages).astype(jnp.int32)
    return {
        "x_prompt": jax.random.normal(ks[1], (BATCH, SEQ, D_MODEL), f32),
        "x_sample": jax.random.normal(ks[2], (DEC_BATCH, DEC_SEQ, D_MODEL), f32),
        "cache_k": jax.random.normal(ks[3], (DEPTH, n_pool, PAGE_SIZE, MOBA_HEADS, MOBA_HD), f32),
        "cache_v": jax.random.normal(ks[4], (DEPTH, n_pool, PAGE_SIZE, MOBA_HEADS, MOBA_HD), f32),
        "state_gla": jax.random.normal(ks[5], (DEPTH, DEC_BATCH, GLA_HEADS, GLA_HDK, GLA_HDV), f32),
        "page_table": page_table,
        "ffn1_g": gain(ks[6], D_MODEL),
        "ffn1_wg": w(ks[7], (D_MODEL, D_FF), D_MODEL),
        "ffn1_wu": w(ks[8], (D_MODEL, D_FF), D_MODEL),
        "ffn1_wd": w(ks[9], (D_FF, D_MODEL), D_FF),
        "mix_g": gain(ks[10], D_MODEL),
        "w_in": w(ks[11], (D_MODEL, D_IN), D_MODEL),
        "w_a2": w(ks[12], (GLA_RANK, GLA_DK), GLA_RANK),
        "b_a": 0.01 * jax.random.normal(ks[13], (DEPTH, GLA_DK), f32),
        "gla_norm_g": gain(ks[14], GLA_HDV),
        "q_norm_g": gain(ks[15], MOBA_HD),
        "k_norm_g": gain(ks[16], MOBA_HD),
        "w_pa": w(ks[17], (GLA_DV, D_MODEL), GLA_DV),
        "w_pb": w(ks[18], (MOBA_W, D_MODEL), MOBA_W),
        "w_o": w(ks[19], (D_MODEL, D_MODEL), D_MODEL),
        "ffn2_g": gain(ks[20], D_MODEL),
        "ffn2_wg": w(ks[21], (D_MODEL, D_FF), D_MODEL),
        "ffn2_wu": w(ks[22], (D_MODEL, D_FF), D_MODEL),
        "ffn2_wd": w(ks[23], (D_FF, D_MODEL), D_FF),
    }


def reference(x_prompt, x_sample, cache_k, cache_v, state_gla, page_table,
              ffn1_g, ffn1_wg, ffn1_wu, ffn1_wd, mix_g, w_in, w_a2, b_a, gla_norm_g,
              q_norm_g, k_norm_g, w_pa, w_pb, w_o, ffn2_g, ffn2_wg, ffn2_wu, ffn2_wd):
    y_p = x_prompt
    y_s = x_sample
    dec_b = x_sample.shape[0]
    kp, vp, sp, ksm, vsm, ssm = [], [], [], [], [], []
    for l in range(DEPTH):
        p = dict(ffn1_g=ffn1_g[l], ffn1_wg=ffn1_wg[l], ffn1_wu=ffn1_wu[l], ffn1_wd=ffn1_wd[l],
                 mix_g=mix_g[l], w_in=w_in[l], w_a2=w_a2[l], b_a=b_a[l], gla_norm_g=gla_norm_g[l],
                 q_norm_g=q_norm_g[l], k_norm_g=k_norm_g[l], w_pa=w_pa[l], w_pb=w_pb[l], w_o=w_o[l],
                 ffn2_g=ffn2_g[l], ffn2_wg=ffn2_wg[l], ffn2_wu=ffn2_wu[l], ffn2_wd=ffn2_wd[l])
        s0 = jnp.zeros((y_p.shape[0], GLA_HEADS, GLA_HDK, GLA_HDV), jnp.float32)
        y_p, k_new, v_new, s_new = decoder_layer(y_p, 0, s0, None, None, p)
        kp.append(k_new)
        vp.append(v_new)
        sp.append(s_new.astype(x_prompt.dtype))
        past_k = cache_k[l][page_table].reshape(dec_b, -1, MOBA_HEADS, MOBA_HD)
        past_v = cache_v[l][page_table].reshape(dec_b, -1, MOBA_HEADS, MOBA_HD)
        y_s, k_s, v_s, s_s = decoder_layer(y_s, past_k.shape[1], state_gla[l], past_k, past_v, p)
        ksm.append(k_s)
        vsm.append(v_s)
        ssm.append(s_s.astype(state_gla.dtype))
    return (y_p, y_s, jnp.stack(kp), jnp.stack(vp), jnp.stack(sp), jnp.stack(ksm), jnp.stack(vsm), jnp.stack(ssm))
```

```python
import functools

import numpy as np
import jax
import jax.numpy as jnp
from jax import lax
from jax.experimental import pallas as pl
from jax.experimental.pallas import tpu as pltpu

F32 = jnp.float32
BF16 = jnp.bfloat16

D_MODEL = 1024
D_FF = 2816
GLA_HEADS = 4
GLA_HDK = 128
GLA_HDV = 256
GLA_DK = GLA_HEADS * GLA_HDK
GLA_DV = GLA_HEADS * GLA_HDV
GLA_RANK = 16
GLA_TAU = 16.0
MOBA_HEADS = 8
MOBA_HD = 128
MOBA_W = MOBA_HEADS * MOBA_HD
MOBA_BLOCK = 256
MOBA_TOPK = 3
ROPE_THETA = 10000.0
EPS = 1e-6
PAGE_SIZE = 128
IN_SPLITS = (GLA_DK, GLA_DK, GLA_DV, GLA_DV, GLA_RANK, MOBA_W, MOBA_W, MOBA_W, D_MODEL, D_MODEL)

LANES = 128
SUBLANES = 8
VMEM_BYTES = 64 * 1024 * 1024

FF_CHUNK = 256
FFN_ROWS = 512
PROJ_ROWS = MOBA_BLOCK
GLA_CHUNK = 128
PAGES_PER_STEP = 8
MASK_BIG = 2.0 ** 100
NEG_INIT = -1.0e38


def _vmem_limit(nbytes):
    return int(min(VMEM_BYTES - (4 << 20), max(nbytes, 16 << 20)))


def _resident(shape):
    return pl.BlockSpec(shape, lambda *_: (0,) * len(shape), pipeline_mode=pl.Buffered(1))


def _dot(a, b):
    return jnp.dot(a, b, preferred_element_type=F32)


def _dot_nt(a, b):
    return lax.dot_general(a, b, (((1,), (1,)), ((), ())), preferred_element_type=F32)


def _rms_norm(x, g):
    return x * lax.rsqrt(jnp.mean(x * x, axis=-1, keepdims=True) + EPS) * g


def _log_sigmoid(z):
    return jnp.minimum(z, 0.0) - jnp.log1p(jnp.exp(-jnp.abs(z)))


def _split3(x):
    hi = x.astype(BF16)
    r = x - hi.astype(F32)
    mid = r.astype(BF16)
    lo = (r - mid.astype(F32)).astype(BF16)
    return hi, mid, lo


def _ffn_body(x_ref, g_ref, wg_ref, wu_ref, wd_ref, o_ref):
    x = x_ref[...]
    h = _rms_norm(x, g_ref[...]).astype(BF16)
    acc = jnp.zeros_like(x)
    for c in range(D_FF // FF_CHUNK):
        sl = slice(c * FF_CHUNK, (c + 1) * FF_CHUNK)
        a = _dot(h, wg_ref[:, sl])
        u = _dot(h, wu_ref[:, sl])
        act = (a * jax.nn.sigmoid(a) * u).astype(BF16)
        acc = acc + _dot(act, wd_ref[sl, :])
    o_ref[...] = x + 0.5 * acc


def _ffn(x, g, wg, wu, wd):
    m = x.shape[0]
    tm = min(FFN_ROWS, m)
    row = pl.BlockSpec((tm, D_MODEL), lambda i: (i, 0))
    need = 3 * D_MODEL * D_FF * 2 + 4 * tm * D_MODEL * 4 + 6 * tm * D_MODEL * 4
    return pl.pallas_call(
        _ffn_body,
        grid=(m // tm,),
        in_specs=[row, _resident((1, D_MODEL)), _resident((D_MODEL, D_FF)), _resident((D_MODEL, D_FF)),
                  _resident((D_FF, D_MODEL))],
        out_specs=row,
        out_shape=jax.ShapeDtypeStruct((m, D_MODEL), F32),
        compiler_params=pltpu.CompilerParams(dimension_semantics=("parallel",),
                                             vmem_limit_bytes=_vmem_limit(need)),
        name="swiglu_half_step",
    )(x, g, wg, wu, wd)


def _proj_body(x_ref, g_ref, wgq, wgk, wgv, wgg, wglr, wa2, ba, wgk_t, wglr_t, wa2_t, ba_t,
               wmq, wmk, wmv, qg, kg, rc, rs,
               gq_o, gk_o, gv_o, gg_o, la_o, gkt_o, lat_o, q_o, k_o, kb_o, v_o, vb_o, ksum_o):
    h = _rms_norm(x_ref[...], g_ref[...]).astype(BF16)
    gq_o[...] = _dot(h, wgq[...]) * (GLA_HDK ** -0.5)
    gk_o[...] = _dot(h, wgk[...])
    gv_o[...] = _dot(h, wgv[...]).astype(BF16)
    gg_o[...] = _dot(h, wgg[...]).astype(BF16)
    glr = _dot(h, wglr[...]).astype(BF16)
    la_o[...] = _log_sigmoid(_dot(glr, wa2[...]) + ba[...]) * (1.0 / GLA_TAU)
    gkt_o[...] = _dot_nt(wgk_t[...], h)
    glr_t = _dot_nt(wglr_t[...], h).astype(BF16)
    lat_o[...] = _log_sigmoid(_dot(wa2_t[...], glr_t) + ba_t[...]) * (1.0 / GLA_TAU)
    mq = _dot(h, wmq[...])
    mk = _dot(h, wmk[...])
    mv = _dot(h, wmv[...])
    cos = rc[...]
    sin = rs[...]
    for hd in range(MOBA_HEADS):
        sl = slice(hd * MOBA_HD, (hd + 1) * MOBA_HD)
        qh = _rms_norm(mq[:, sl], qg[:, sl])
        q_o[:, sl] = qh * cos + pltpu.roll(qh, MOBA_HD // 2, 1) * sin
        kh = _rms_norm(mk[:, sl], kg[:, sl])
        kh = kh * cos + pltpu.roll(kh, MOBA_HD // 2, 1) * sin
        k_o[:, sl] = kh
        kb_o[:, sl] = kh.astype(BF16)
    v_o[...] = mv
    vb_o[...] = mv.astype(BF16)
    ksum_o[0] = jnp.sum(k_o[...], axis=0, keepdims=True)


def _proj(x, mix_g, w, rope_cos, rope_sin):
    m = x.shape[0]
    tm = min(PROJ_ROWS, m)
    n_tiles = m // tm
    n_rope = rope_cos.shape[0] // tm

    def rows(width):
        return pl.BlockSpec((tm, width), lambda i: (i, 0))

    def cols(height):
        return pl.BlockSpec((height, tm), lambda i: (0, i))

    rope_spec = pl.BlockSpec((tm, MOBA_HD), lambda i: (i % n_rope, 0))
    weights = [w["gq"], w["gk"], w["gv"], w["gg"], w["glr"], w["a2"], w["ba"], w["gk_t"], w["glr_t"],
               w["a2_t"], w["ba_t"], w["mq"], w["mk"], w["mv"], w["qg"], w["kg"]]
    out_shape = [
        jax.ShapeDtypeStruct((m, GLA_DK), F32), jax.ShapeDtypeStruct((m, GLA_DK), F32),
        jax.ShapeDtypeStruct((m, GLA_DV), BF16), jax.ShapeDtypeStruct((m, GLA_DV), BF16),
        jax.ShapeDtypeStruct((m, GLA_DK), F32),
        jax.ShapeDtypeStruct((GLA_DK, m), F32), jax.ShapeDtypeStruct((GLA_DK, m), F32),
        jax.ShapeDtypeStruct((m, MOBA_W), F32), jax.ShapeDtypeStruct((m, MOBA_W), F32),
        jax.ShapeDtypeStruct((m, MOBA_W), BF16), jax.ShapeDtypeStruct((m, MOBA_W), F32),
        jax.ShapeDtypeStruct((m, MOBA_W), BF16), jax.ShapeDtypeStruct((n_tiles, 1, MOBA_W), F32),
    ]
    out_specs = [rows(GLA_DK), rows(GLA_DK), rows(GLA_DV), rows(GLA_DV), rows(GLA_DK),
                 cols(GLA_DK), cols(GLA_DK),
                 rows(MOBA_W), rows(MOBA_W), rows(MOBA_W), rows(MOBA_W), rows(MOBA_W),
                 pl.BlockSpec((1, 1, MOBA_W), lambda i: (i, 0, 0))]
    w_bytes = sum(int(a.size) * a.dtype.itemsize for a in weights)
    out_bytes = sum(tm * (s.shape[1] if s.shape[0] == m else s.shape[0]) * s.dtype.itemsize for s in out_shape[:-1])
    need = w_bytes + 2 * out_bytes + 2 * tm * D_MODEL * 4 + 8 * tm * MOBA_W * 4
    return pl.pallas_call(
        _proj_body,
        grid=(n_tiles,),
        in_specs=[rows(D_MODEL), _resident((1, D_MODEL))] + [_resident(a.shape) for a in weights]
                 + [rope_spec, rope_spec],
        out_specs=out_specs,
        out_shape=out_shape,
        compiler_params=pltpu.CompilerParams(dimension_semantics=("parallel",),
                                             vmem_limit_bytes=_vmem_limit(need)),
        name="mixer_input_projection",
    )(x, mix_g, *weights, rope_cos, rope_sin)


def _gla_tables():
    c = GLA_CHUNK
    t = np.arange(c)
    le = t[None, :] <= t[:, None]
    gt = t[None, :] > t[:, None]
    sums = [le]
    masks = []
    s = c // 2
    while s >= 1:
        same = (t // (2 * s))[:, None] == (t // (2 * s))[None, :]
        right = (t % (2 * s)) >= s
        sums.append(same & right[:, None] & right[None, :] & le)
        sums.append(same & ~right[:, None] & ~right[None, :] & gt)
        masks.append(same & right[:, None] & ~right[None, :])
        s //= 2
    masks.append(np.eye(c, dtype=bool))
    tail = np.concatenate([gt.T, np.ones((c, c), dtype=bool)], axis=1)
    return (np.concatenate(sums, 0).astype(np.float32), np.stack(masks).astype(np.float32),
            tail.astype(np.float32))


def _gla_body(q_ref, k_ref, la_ref, kt_ref, lat_ref, v_ref, gg_ref, ng_ref, sums_ref, masks_ref, tail_ref,
              o_ref, s_ref):
    c = GLA_CHUNK
    n_lvl = masks_ref.shape[0] - 1

    @pl.when(pl.program_id(1) == 0)
    def _():
        s_ref[...] = jnp.zeros_like(s_ref)

    sums = sums_ref[...]
    tail = tail_ref[...]
    for hd in range(GLA_HEADS):
        ks = slice(hd * GLA_HDK, (hd + 1) * GLA_HDK)
        vs = slice(hd * GLA_HDV, (hd + 1) * GLA_HDV)
        q = q_ref[:, ks]
        k = k_ref[:, ks]
        v = v_ref[:, vs]
        state = s_ref[0, hd]
        e = jnp.exp(sum(_dot(sums, p) for p in _split3(la_ref[:, ks])))
        et = jnp.exp(sum(_dot(p, tail) for p in _split3(lat_ref[ks, :])))
        out = _dot((q * e[:c]).astype(BF16), state.astype(BF16))
        attn = masks_ref[n_lvl] * _dot_nt(q.astype(BF16), k.astype(BF16))
        for lv in range(n_lvl):
            eq = e[(1 + 2 * lv) * c:(2 + 2 * lv) * c]
            ek = e[(2 + 2 * lv) * c:(3 + 2 * lv) * c]
            attn = attn + masks_ref[lv] * _dot_nt((q * eq).astype(BF16), (k * ek).astype(BF16))
        out = out + _dot(attn.astype(BF16), v)
        k_dec = (kt_ref[ks, :] * et[:, :c]).astype(BF16)
        decay = et[:, c:]
        s_ref[0, hd] = jnp.concatenate([state[:, :c] * decay, state[:, c:] * decay], axis=1) + _dot(k_dec, v)
        gate = gg_ref[:, vs].astype(F32)
        o_ref[:, vs] = (_rms_norm(out, ng_ref[...]) * (gate * jax.nn.sigmoid(gate))).astype(BF16)


def _gla_prompt(gq, gk, la, gk_t, la_t, gv, gg, norm_g, batch, seq):
    c = GLA_CHUNK
    n_chunks = seq // c
    sums, masks, tail = (jnp.asarray(a, BF16 if i != 1 else F32) for i, a in enumerate(_gla_tables()))

    def rows(width):
        return pl.BlockSpec((c, width), lambda b, t: (b * n_chunks + t, 0))

    cols = pl.BlockSpec((GLA_DK, c), lambda b, t: (0, b * n_chunks + t))
    need = 2 * (5 * c * GLA_DK * 4 + 3 * c * GLA_DV * 2) + 4 * GLA_DK * GLA_HDV * 4 + (8 << 20)
    return pl.pallas_call(
        _gla_body,
        grid=(batch, n_chunks),
        in_specs=[rows(GLA_DK), rows(GLA_DK), rows(GLA_DK), cols, cols, rows(GLA_DV), rows(GLA_DV),
                  _resident((1, GLA_HDV)), _resident(sums.shape), _resident(masks.shape), _resident(tail.shape)],
        out_specs=[rows(GLA_DV),
                   pl.BlockSpec((1, GLA_HEADS, GLA_HDK, GLA_HDV), lambda b, t: (b, 0, 0, 0))],
        out_shape=[jax.ShapeDtypeStruct((batch * seq, GLA_DV), BF16),
                   jax.ShapeDtypeStruct((batch, GLA_HEADS, GLA_HDK, GLA_HDV), F32)],
        compiler_params=pltpu.CompilerParams(dimension_semantics=("parallel", "arbitrary"),
                                             vmem_limit_bytes=_vmem_limit(need)),
        name="gla_prompt_chunks",
    )(gq, gk, la, gk_t, la_t, gv, gg, norm_g, sums, masks, tail)


def _moba_select_body(q_ref, ksum_ref, qe_ref):
    own = pl.program_id(1).astype(F32)
    tq = q_ref.shape[0]
    nb = ksum_ref.shape[1]
    col = lax.broadcasted_iota(jnp.int32, (tq, LANES), 1).astype(F32)
    pad = jnp.zeros((LANES - nb, MOBA_HD), F32)
    for hd in range(MOBA_HEADS):
        sl = slice(hd * MOBA_HD, (hd + 1) * MOBA_HD)
        q = q_ref[:, sl]
        means = jnp.concatenate([ksum_ref[0, :, sl] * (1.0 / MOBA_BLOCK), pad], axis=0)
        q_hi = q.astype(BF16)
        q_lo = (q - q_hi.astype(F32)).astype(BF16)
        m_hi = means.astype(BF16)
        m_lo = (means - m_hi.astype(F32)).astype(BF16)
        gate = _dot_nt(q_hi, m_hi) + _dot_nt(q_hi, m_lo) + _dot_nt(q_lo, m_hi)
        valid = col < own
        gate = jnp.where(valid, gate, -jnp.inf)
        picked = col == own
        for _ in range(MOBA_TOPK):
            best = jnp.max(gate, axis=1, keepdims=True)
            first = jnp.min(jnp.where(gate == best, col, float(LANES)), axis=1, keepdims=True)
            pick = (col == first) & valid
            picked = picked | pick
            gate = jnp.where(pick, -jnp.inf, gate)
        qe_ref[0, hd, :, :MOBA_HD] = (q * (MOBA_HD ** -0.5)).astype(BF16)
        qe_ref[0, hd, :, MOBA_HD:] = jnp.where(picked, 0.0, -1.0).astype(BF16)


def _moba_select(q, ksum, batch, seq):
    tq = MOBA_BLOCK
    nb = seq // tq
    return pl.pallas_call(
        _moba_select_body,
        grid=(batch, nb),
        in_specs=[pl.BlockSpec((tq, MOBA_W), lambda b, i: (b * nb + i, 0)),
                  pl.BlockSpec((1, nb, MOBA_W), lambda b, i: (b, 0, 0))],
        out_specs=pl.BlockSpec((1, MOBA_HEADS, tq, 2 * MOBA_HD), lambda b, i: (b, 0, i, 0)),
        out_shape=jax.ShapeDtypeStruct((batch, MOBA_HEADS, seq, 2 * MOBA_HD), BF16),
        compiler_params=pltpu.CompilerParams(dimension_semantics=("parallel", "parallel"),
                                             vmem_limit_bytes=_vmem_limit(16 << 20)),
        name="moba_block_select",
    )(q, ksum)


def _moba_attend_body(qe_ref, k_ref, v_ref, o_ref, acc_ref, m_ref, l_ref):
    j = pl.program_id(2)
    nb = pl.num_programs(2)
    tb = MOBA_BLOCK

    @pl.when(j == 0)
    def _():
        acc_ref[...] = jnp.zeros_like(acc_ref)
        l_ref[...] = jnp.zeros_like(l_ref)
        m_ref[...] = jnp.full_like(m_ref, NEG_INIT)

    v = v_ref[...]
    lane = lax.broadcasted_iota(jnp.int32, (tb, MOBA_HD), 1)
    k_ext = jnp.concatenate([k_ref[...], jnp.where(lane == j, MASK_BIG, 0.0).astype(BF16)], axis=1)

    def update(n, causal):
        rows = pl.ds(pl.multiple_of(n * tb, tb), tb)
        s = _dot_nt(qe_ref[0, 0, rows, :], k_ext)
        if causal:
            r = lax.broadcasted_iota(jnp.int32, (tb, tb), 0)
            c = lax.broadcasted_iota(jnp.int32, (tb, tb), 1)
            s = jnp.where(c <= r, s, -MASK_BIG)
        m_old = m_ref[rows, :]
        m_new = jnp.maximum(m_old, jnp.max(s, axis=1, keepdims=True))
        alpha = jnp.exp(m_old - m_new)
        p = jnp.exp(s - jnp.concatenate([m_new, m_new], axis=1))
        l_new = alpha * l_ref[rows, :] + jnp.sum(p, axis=1, keepdims=True)
        acc_new = alpha * acc_ref[rows, :] + _dot(p.astype(BF16), v)
        m_ref[rows, :] = m_new
        l_ref[rows, :] = l_new
        acc_ref[rows, :] = acc_new
        return acc_new, l_new

    acc, l_sum = update(j, True)
    o_ref[...] = (acc / l_sum).astype(BF16)

    def body(n, carry):
        update(n, False)
        return carry

    lax.fori_loop(j + 1, nb, body, 0)


def _moba_attend(qe, kb, vb, batch, seq):
    tb = MOBA_BLOCK
    nb = seq // tb
    kv = pl.BlockSpec((tb, MOBA_HD), lambda b, h, j: (b * nb + j, h))
    need = 2 * seq * 2 * MOBA_HD * 2 + 3 * seq * MOBA_HD * 4 + (8 << 20)
    return pl.pallas_call(
        _moba_attend_body,
        grid=(batch, MOBA_HEADS, nb),
        in_specs=[pl.BlockSpec((1, 1, seq, 2 * MOBA_HD), lambda b, h, j: (b, h, 0, 0)), kv, kv],
        out_specs=kv,
        out_shape=jax.ShapeDtypeStruct((batch * seq, MOBA_W), BF16),
        scratch_shapes=[pltpu.VMEM((seq, MOBA_HD), F32), pltpu.VMEM((seq, MOBA_HD), F32),
                        pltpu.VMEM((seq, MOBA_HD), F32)],
        compiler_params=pltpu.CompilerParams(dimension_semantics=("parallel", "parallel", "arbitrary"),
                                             vmem_limit_bytes=_vmem_limit(need)),
        name="moba_prompt_attention",
    )(qe, kb, vb)


def _merge_body(x_ref, g_ref, oa_ref, ob_ref, wga, wgb, wpa, wpb, wo, o_ref):
    x = x_ref[...]
    h = _rms_norm(x, g_ref[...]).astype(BF16)
    mix = jax.nn.sigmoid(_dot(h, wga[...])) * _dot(oa_ref[...], wpa[...])
    mix = mix + jax.nn.sigmoid(_dot(h, wgb[...])) * _dot(ob_ref[...], wpb[...])
    o_ref[...] = x + _dot(mix.astype(BF16), wo[...])


def _merge(x, mix_g, o_gla, o_moba, w):
    m = x.shape[0]
    tm = min(FFN_ROWS, m)
    row32 = pl.BlockSpec((tm, D_MODEL), lambda i: (i, 0))
    sq = _resident((D_MODEL, D_MODEL))
    need = 5 * D_MODEL * D_MODEL * 2 + 4 * tm * D_MODEL * 4 + 4 * tm * D_MODEL * 2 + 6 * tm * D_MODEL * 4
    return pl.pallas_call(
        _merge_body,
        grid=(m // tm,),
        in_specs=[row32, _resident((1, D_MODEL)), row32, row32, sq, sq, sq, sq, sq],
        out_specs=row32,
        out_shape=jax.ShapeDtypeStruct((m, D_MODEL), F32),
        compiler_params=pltpu.CompilerParams(dimension_semantics=("parallel",),
                                             vmem_limit_bytes=_vmem_limit(need)),
        name="gated_merge_projection",
    )(x, mix_g, o_gla, o_moba, w["ga"], w["gb"], w["pa"], w["pb"], w["o"])


def _gla_sample_body(q_ref, k_ref, la_ref, v_ref, gg_ref, s_ref, ng_ref, o_ref, so_ref):
    for hd in range(GLA_HEADS):
        v = v_ref[0, hd].astype(F32)
        s_new = jnp.exp(la_ref[0, hd]) * s_ref[0, hd] + k_ref[0, hd] * v
        so_ref[0, hd] = s_new
        out = jnp.sum(q_ref[0, hd] * s_new, axis=0, keepdims=True)
        gate = gg_ref[0, hd].astype(F32)
        o_ref[0, hd] = (_rms_norm(out, ng_ref[...]) * (gate * jax.nn.sigmoid(gate))).astype(BF16)


def _gla_sample(gq, gk, la, gv, gg, state, norm_g):
    n = gq.shape[0]
    col = pl.BlockSpec((1, GLA_HEADS, GLA_HDK, 1), lambda i: (i, 0, 0, 0))
    row = pl.BlockSpec((1, GLA_HEADS, 1, GLA_HDV), lambda i: (i, 0, 0, 0))
    st = pl.BlockSpec((1, GLA_HEADS, GLA_HDK, GLA_HDV), lambda i: (i, 0, 0, 0))
    as_col = lambda a: a.reshape(n, GLA_HEADS, GLA_HDK, 1)
    as_row = lambda a: a.reshape(n, GLA_HEADS, 1, GLA_HDV)
    o, s_new = pl.pallas_call(
        _gla_sample_body,
        grid=(n,),
        in_specs=[col, col, col, row, row, st, _resident((1, GLA_HDV))],
        out_specs=[row, st],
        out_shape=[jax.ShapeDtypeStruct((n, GLA_HEADS, 1, GLA_HDV), BF16),
                   jax.ShapeDtypeStruct((n, GLA_HEADS, GLA_HDK, GLA_HDV), F32)],
        compiler_params=pltpu.CompilerParams(dimension_semantics=("parallel",),
                                             vmem_limit_bytes=_vmem_limit(16 << 20)),
        name="gla_sample_step",
    )(as_col(gq), as_col(gk), as_col(la), as_row(gv), as_row(gg), state, norm_g)
    return o.reshape(n, GLA_DV), s_new


def _page_sum_body(pt_ref, *refs):
    pages, o_ref = refs[:PAGES_PER_STEP], refs[PAGES_PER_STEP]
    per_block = MOBA_BLOCK // PAGE_SIZE
    for blk in range(PAGES_PER_STEP // per_block):
        tot = jnp.sum(pages[blk * per_block][0], axis=0, keepdims=True)
        for p in range(1, per_block):
            tot = tot + jnp.sum(pages[blk * per_block + p][0], axis=0, keepdims=True)
        o_ref[0, 0, blk:blk + 1, :] = tot


def _page_block_means(cache, page_table_flat, n_seq, n_pages):
    groups = n_pages // PAGES_PER_STEP
    blocks_per_step = PAGES_PER_STEP * PAGE_SIZE // MOBA_BLOCK

    def page_spec(i):
        return pl.BlockSpec((1, PAGE_SIZE, MOBA_W),
                            lambda s, g, pt: (pt[s * n_pages + g * PAGES_PER_STEP + i], 0, 0))

    sums = pl.pallas_call(
        _page_sum_body,
        grid_spec=pltpu.PrefetchScalarGridSpec(
            num_scalar_prefetch=1,
            grid=(n_seq, groups),
            in_specs=[page_spec(i) for i in range(PAGES_PER_STEP)],
            out_specs=pl.BlockSpec((1, 1, blocks_per_step, MOBA_W), lambda s, g, pt: (s, g, 0, 0)),
        ),
        out_shape=jax.ShapeDtypeStruct((n_seq, groups, blocks_per_step, MOBA_W), F32),
        compiler_params=pltpu.CompilerParams(dimension_semantics=("parallel", "parallel"),
                                             vmem_limit_bytes=_vmem_limit(3 * PAGES_PER_STEP * PAGE_SIZE * MOBA_W * 4)),
        name="paged_key_block_sums",
    )(page_table_flat, *([cache] * PAGES_PER_STEP))
    return sums.reshape(n_seq, groups * blocks_per_step, MOBA_W)


def _sample_select_body(q_ref, ksum_ref, sel_ref):
    nb = ksum_ref.shape[1]
    q = q_ref[0]
    lane = lax.broadcasted_iota(jnp.int32, (nb, LANES), 1)
    row = lax.broadcasted_iota(jnp.int32, (nb, LANES), 0).astype(F32)
    gate = jnp.full((nb, LANES), -jnp.inf, F32)
    for hd in range(MOBA_HEADS):
        sl = slice(hd * MOBA_HD, (hd + 1) * MOBA_HD)
        g = jnp.sum(ksum_ref[0, :, sl] * (1.0 / MOBA_BLOCK) * q[:, sl], axis=1, keepdims=True)
        gate = jnp.where(lane == hd, g, gate)
    out_row = lax.broadcasted_iota(jnp.int32, (SUBLANES, LANES), 0)
    out = jnp.zeros((SUBLANES, LANES), jnp.int32)
    for r in range(MOBA_TOPK):
        best = jnp.max(gate, axis=0, keepdims=True)
        first = jnp.min(jnp.where(gate == best, row, float(nb)), axis=0, keepdims=True)
        out = jnp.where(out_row == r, first.astype(jnp.int32), out)
        gate = jnp.where(row == first, -jnp.inf, gate)
    sel_ref[0] = out


def _sample_select(q, ksum):
    n, nb = ksum.shape[0], ksum.shape[1]
    sel = pl.pallas_call(
        _sample_select_body,
        grid=(n,),
        in_specs=[pl.BlockSpec((1, 1, MOBA_W), lambda i: (i, 0, 0)),
                  pl.BlockSpec((1, nb, MOBA_W), lambda i: (i, 0, 0))],
        out_specs=pl.BlockSpec((1, SUBLANES, LANES), lambda i: (i, 0, 0)),
        out_shape=jax.ShapeDtypeStruct((n, SUBLANES, LANES), jnp.int32),
        compiler_params=pltpu.CompilerParams(dimension_semantics=("parallel",)),
        name="moba_sample_select",
    )(q.reshape(n, 1, MOBA_W), ksum)
    return jnp.transpose(sel[:, :MOBA_TOPK, :MOBA_HEADS], (0, 2, 1)).reshape(-1)


def _sample_attend_body(pt_ref, sel_ref, q_ref, kn_ref, vn_ref, *refs):
    n_sel = MOBA_TOPK * (MOBA_BLOCK // PAGE_SIZE)
    k_refs, v_refs, o_ref = refs[:n_sel], refs[n_sel:2 * n_sel], refs[2 * n_sel]
    scale = MOBA_HD ** -0.5
    q = q_ref[0]
    s_new = jnp.sum(kn_ref[0] * q, axis=1, keepdims=True) * scale
    scores = [jnp.sum(kr[0] * q, axis=1, keepdims=True) * scale for kr in k_refs]
    top = s_new
    for s in scores:
        top = jnp.maximum(top, jnp.max(s, axis=0, keepdims=True))
    p_new = jnp.exp(s_new - top)
    denom = p_new
    acc = p_new * vn_ref[0]
    for s, vr in zip(scores, v_refs):
        p = jnp.exp(s - top)
        denom = denom + jnp.sum(p, axis=0, keepdims=True)
        acc = acc + jnp.sum(p * vr[0], axis=0, keepdims=True)
    o_ref[0] = (acc / denom).astype(BF16)


def _sample_attend(q, k_new, v_new, cache_k, cache_v, page_table_flat, sel_flat, n_pages):
    n = q.shape[0]
    per_block = MOBA_BLOCK // PAGE_SIZE
    tok = pl.BlockSpec((1, 1, MOBA_HD), lambda s, h, pt, sel: (s, 0, h))

    def page_spec(slot, half):
        def index(s, h, pt, sel):
            blk = sel[(s * MOBA_HEADS + h) * MOBA_TOPK + slot]
            return (pt[s * n_pages + blk * per_block + half], 0, h)
        return pl.BlockSpec((1, PAGE_SIZE, MOBA_HD), index)

    pages = [page_spec(slot, half) for slot in range(MOBA_TOPK) for half in range(per_block)]
    as_tok = lambda a: a.reshape(n, 1, MOBA_W)
    out = pl.pallas_call(
        _sample_attend_body,
        grid_spec=pltpu.PrefetchScalarGridSpec(
            num_scalar_prefetch=2,
            grid=(n, MOBA_HEADS),
            in_specs=[tok, tok, tok] + pages + pages,
            out_specs=tok,
        ),
        out_shape=jax.ShapeDtypeStruct((n, 1, MOBA_W), BF16),
        compiler_params=pltpu.CompilerParams(dimension_semantics=("parallel", "parallel")),
        name="moba_sample_attention",
    )(page_table_flat, sel_flat, as_tok(q), as_tok(k_new), as_tok(v_new),
      *([cache_k] * len(pages)), *([cache_v] * len(pages)))
    return out.reshape(n, MOBA_W)


def _rope_tables(pos):
    half = MOBA_HD // 2
    inv = ROPE_THETA ** (-jnp.arange(half, dtype=F32) / half)
    ang = pos.astype(F32)[:, None] * inv[None, :]
    cos, sin = jnp.cos(ang), jnp.sin(ang)
    return jnp.concatenate([cos, cos], axis=1), jnp.concatenate([-sin, sin], axis=1)


def _layer_weights(w_in, w_a2, b_a, q_norm_g, k_norm_g, w_pa, w_pb, w_o):
    offs = np.concatenate([[0], np.cumsum(IN_SPLITS)])
    gq, gk, gv, gg, glr, mq, mk, mv, ga, gb = (w_in[:, offs[i]:offs[i + 1]] for i in range(len(IN_SPLITS)))
    glr = jnp.pad(glr, ((0, 0), (0, LANES - GLA_RANK)))
    a2 = jnp.pad(w_a2, ((0, LANES - GLA_RANK), (0, 0)))
    b16 = lambda a: a.astype(BF16)
    return dict(
        gq=b16(gq), gk=b16(gk), gv=b16(gv), gg=b16(gg), glr=b16(glr), a2=b16(a2), ba=b_a.reshape(1, GLA_DK),
        gk_t=b16(gk.T), glr_t=b16(glr.T), a2_t=b16(a2.T), ba_t=b_a.reshape(GLA_DK, 1),
        mq=b16(mq), mk=b16(mk), mv=b16(mv),
        qg=jnp.tile(q_norm_g, MOBA_HEADS).reshape(1, MOBA_W), kg=jnp.tile(k_norm_g, MOBA_HEADS).reshape(1, MOBA_W),
        ga=b16(ga), gb=b16(gb), pa=b16(w_pa), pb=b16(w_pb), o=b16(w_o))


def kernel(x_prompt, x_sample, cache_k, cache_v, state_gla, page_table, ffn1_g, ffn1_wg, ffn1_wu, ffn1_wd, mix_g, w_in, w_a2, b_a, gla_norm_g, q_norm_g, k_norm_g, w_pa, w_pb, w_o, ffn2_g, ffn2_wg, ffn2_wu, ffn2_wd):
    batch, seq, _ = x_prompt.shape
    n_dec, dec_seq, _ = x_sample.shape
    n_pages = page_table.shape[1]
    depth = w_in.shape[0]
    past_len = n_pages * PAGE_SIZE
    assert dec_seq == 1 and seq % MOBA_BLOCK == 0 and past_len % MOBA_BLOCK == 0
    assert past_len // MOBA_BLOCK >= MOBA_TOPK and n_pages % PAGES_PER_STEP == 0

    yp = x_prompt.reshape(batch * seq, D_MODEL)
    ys = x_sample.reshape(n_dec, D_MODEL)
    rope_p = _rope_tables(jnp.arange(seq, dtype=jnp.int32))
    rope_s = _rope_tables(jnp.full((n_dec,), past_len, jnp.int32))
    pt_flat = page_table.reshape(-1)
    row = lambda a: a.reshape(1, -1)
    outs = [[] for _ in range(6)]
    for l in range(depth):
        w = _layer_weights(w_in[l], w_a2[l], b_a[l], q_norm_g[l], k_norm_g[l], w_pa[l], w_pb[l], w_o[l])
        ffn1 = (row(ffn1_g[l]), ffn1_wg[l].astype(BF16), ffn1_wu[l].astype(BF16), ffn1_wd[l].astype(BF16))
        ffn2 = (row(ffn2_g[l]), ffn2_wg[l].astype(BF16), ffn2_wu[l].astype(BF16), ffn2_wd[l].astype(BF16))
        norm_g = row(gla_norm_g[l])
        ck = cache_k[l].reshape(-1, PAGE_SIZE, MOBA_W)
        cv = cache_v[l].reshape(-1, PAGE_SIZE, MOBA_W)

        x1 = _ffn(yp, *ffn1)
        gq, gk, gv, gg, la, gk_t, la_t, q, k, kb, v, vb, ksum = _proj(x1, row(mix_g[l]), w, *rope_p)
        o_gla, s_prompt = _gla_prompt(gq, gk, la, gk_t, la_t, gv, gg, norm_g, batch, seq)
        qe = _moba_select(q, ksum.reshape(batch, seq // MOBA_BLOCK, MOBA_W), batch, seq)
        o_moba = _moba_attend(qe, kb, vb, batch, seq)
        yp = _ffn(_merge(x1, row(mix_g[l]), o_gla, o_moba, w), *ffn2)
        outs[0].append(k.reshape(batch, seq, MOBA_HEADS, MOBA_HD))
        outs[1].append(v.reshape(batch, seq, MOBA_HEADS, MOBA_HD))
        outs[2].append(s_prompt)

        x1 = _ffn(ys, *ffn1)
        gq, gk, gv, gg, la, _, _, q, k, _, v, _, _ = _proj(x1, row(mix_g[l]), w, *rope_s)
        o_gla, s_sample = _gla_sample(gq, gk, la, gv, gg, state_gla[l], norm_g)
        sel = _sample_select(q, _page_block_means(ck, pt_flat, n_dec, n_pages))
        o_moba = _sample_attend(q, k, v, ck, cv, pt_flat, sel, n_pages)
        ys = _ffn(_merge(x1, row(mix_g[l]), o_gla, o_moba, w), *ffn2)
        outs[3].append(k.reshape(n_dec, 1, MOBA_HEADS, MOBA_HD))
        outs[4].append(v.reshape(n_dec, 1, MOBA_HEADS, MOBA_HD))
        outs[5].append(s_sample)

    return (yp.reshape(batch, seq, D_MODEL), ys.reshape(n_dec, 1, D_MODEL),
            jnp.stack(outs[0]), jnp.stack(outs[1]), jnp.stack(outs[2]),
            jnp.stack(outs[3]), jnp.stack(outs[4]), jnp.stack(outs[5]))
```

```python
import functools

import numpy as np
import jax
import jax.numpy as jnp
from jax import lax
from jax.experimental import pallas as pl
from jax.experimental.pallas import tpu as pltpu

F32 = jnp.float32
BF16 = jnp.bfloat16

D_MODEL = 1024
D_FF = 2816
GLA_HEADS = 4
GLA_HDK = 128
GLA_HDV = 256
GLA_DK = GLA_HEADS * GLA_HDK
GLA_DV = GLA_HEADS * GLA_HDV
GLA_RANK = 16
GLA_TAU = 16.0
MOBA_HEADS = 8
MOBA_HD = 128
MOBA_W = MOBA_HEADS * MOBA_HD
MOBA_BLOCK = 256
MOBA_TOPK = 3
ROPE_THETA = 10000.0
EPS = 1e-6
PAGE_SIZE = 128
IN_SPLITS = (GLA_DK, GLA_DK, GLA_DV, GLA_DV, GLA_RANK, MOBA_W, MOBA_W, MOBA_W, D_MODEL, D_MODEL)

LANES = 128
SUBLANES = 8
VMEM_BYTES = 64 * 1024 * 1024

FF_CHUNK = 256
FFN_ROWS = 512
PROJ_ROWS = MOBA_BLOCK
GLA_CHUNK = 128
PAGES_PER_STEP = 8
MOBA_GROUP = 4
MASK_BIG = 2.0 ** 100
NEG_INIT = -1.0e38
LOG2_E = 1.4426950408889634


def _vmem_limit(nbytes):
    return int(min(VMEM_BYTES - (4 << 20), max(nbytes, 16 << 20)))


def _resident(shape):
    return pl.BlockSpec(shape, lambda *_: (0,) * len(shape), pipeline_mode=pl.Buffered(1))


def _dot(a, b):
    return jnp.dot(a, b, preferred_element_type=F32)


def _dot_nt(a, b):
    return lax.dot_general(a, b, (((1,), (1,)), ((), ())), preferred_element_type=F32)


def _rms_norm(x, g):
    return x * lax.rsqrt(jnp.mean(x * x, axis=-1, keepdims=True) + EPS) * g


def _log_sigmoid(z):
    return jnp.minimum(z, 0.0) - jnp.log1p(jnp.exp(-jnp.abs(z)))


def _split3(x):
    hi = x.astype(BF16)
    r = x - hi.astype(F32)
    mid = r.astype(BF16)
    lo = (r - mid.astype(F32)).astype(BF16)
    return hi, mid, lo


def _ffn_body(x_ref, g_ref, wg_ref, wu_ref, wd_ref, o_ref):
    x = x_ref[...]
    h = _rms_norm(x, g_ref[...]).astype(BF16)
    acc = jnp.zeros_like(x)
    for c in range(D_FF // FF_CHUNK):
        sl = slice(c * FF_CHUNK, (c + 1) * FF_CHUNK)
        a = _dot(h, wg_ref[:, sl])
        u = _dot(h, wu_ref[:, sl])
        act = (a * jax.nn.sigmoid(a) * u).astype(BF16)
        acc = acc + _dot(act, wd_ref[sl, :])
    o_ref[...] = x + 0.5 * acc


def _ffn(x, g, wg, wu, wd):
    m = x.shape[0]
    tm = min(FFN_ROWS, m)
    row = pl.BlockSpec((tm, D_MODEL), lambda i: (i, 0))
    need = 3 * D_MODEL * D_FF * 2 + 4 * tm * D_MODEL * 4 + 6 * tm * D_MODEL * 4
    return pl.pallas_call(
        _ffn_body,
        grid=(m // tm,),
        in_specs=[row, _resident((1, D_MODEL)), _resident((D_MODEL, D_FF)), _resident((D_MODEL, D_FF)),
                  _resident((D_FF, D_MODEL))],
        out_specs=row,
        out_shape=jax.ShapeDtypeStruct((m, D_MODEL), F32),
        compiler_params=pltpu.CompilerParams(dimension_semantics=("parallel",),
                                             vmem_limit_bytes=_vmem_limit(need)),
        name="swiglu_half_step",
    )(x, g, wg, wu, wd)


def _proj_body(x_ref, g_ref, wgq, wgk, wgv, wgg, wglr, wa2, ba, wgk_t, wglr_t, wa2_t, ba_t,
               wmq, wmk, wmv, qg, kg, rc, rs,
               gq_o, gk_o, gv_o, gg_o, la_o, gkt_o, lat_o, q_o, k_o, v_o, ksum_o):
    h = _rms_norm(x_ref[...], g_ref[...]).astype(BF16)
    gq_o[...] = _dot(h, wgq[...]) * (GLA_HDK ** -0.5)
    gk_o[...] = _dot(h, wgk[...])
    gv_o[...] = _dot(h, wgv[...]).astype(BF16)
    gg_o[...] = _dot(h, wgg[...]).astype(BF16)
    glr = _dot(h, wglr[...]).astype(BF16)
    la_o[...] = _log_sigmoid(_dot(glr, wa2[...]) + ba[...]) * (1.0 / GLA_TAU)
    gkt_o[...] = _dot_nt(wgk_t[...], h)
    glr_t = _dot_nt(wglr_t[...], h).astype(BF16)
    lat_o[...] = _log_sigmoid(_dot(wa2_t[...], glr_t) + ba_t[...]) * (1.0 / GLA_TAU)
    mq = _dot(h, wmq[...])
    mk = _dot(h, wmk[...])
    mv = _dot(h, wmv[...])
    cos = rc[...]
    sin = rs[...]
    for hd in range(MOBA_HEADS):
        sl = slice(hd * MOBA_HD, (hd + 1) * MOBA_HD)
        qh = _rms_norm(mq[:, sl], qg[:, sl])
        q_o[:, sl] = qh * cos + pltpu.roll(qh, MOBA_HD // 2, 1) * sin
        kh = _rms_norm(mk[:, sl], kg[:, sl])
        k_o[:, sl] = kh * cos + pltpu.roll(kh, MOBA_HD // 2, 1) * sin
    v_o[...] = mv
    ksum_o[0] = jnp.sum(k_o[...], axis=0, keepdims=True)


def _proj(x, mix_g, w, rope_cos, rope_sin):
    m = x.shape[0]
    tm = min(PROJ_ROWS, m)
    n_tiles = m // tm
    n_rope = rope_cos.shape[0] // tm

    def rows(width):
        return pl.BlockSpec((tm, width), lambda i: (i, 0))

    def cols(height):
        return pl.BlockSpec((height, tm), lambda i: (0, i))

    rope_spec = pl.BlockSpec((tm, MOBA_HD), lambda i: (i % n_rope, 0))
    weights = [w["gq"], w["gk"], w["gv"], w["gg"], w["glr"], w["a2"], w["ba"], w["gk_t"], w["glr_t"],
               w["a2_t"], w["ba_t"], w["mq"], w["mk"], w["mv"], w["qg"], w["kg"]]
    out_shape = [
        jax.ShapeDtypeStruct((m, GLA_DK), F32), jax.ShapeDtypeStruct((m, GLA_DK), F32),
        jax.ShapeDtypeStruct((m, GLA_DV), BF16), jax.ShapeDtypeStruct((m, GLA_DV), BF16),
        jax.ShapeDtypeStruct((m, GLA_DK), F32),
        jax.ShapeDtypeStruct((GLA_DK, m), F32), jax.ShapeDtypeStruct((GLA_DK, m), F32),
        jax.ShapeDtypeStruct((m, MOBA_W), F32), jax.ShapeDtypeStruct((m, MOBA_W), F32),
        jax.ShapeDtypeStruct((m, MOBA_W), F32), jax.ShapeDtypeStruct((n_tiles, 1, MOBA_W), F32),
    ]
    out_specs = [rows(GLA_DK), rows(GLA_DK), rows(GLA_DV), rows(GLA_DV), rows(GLA_DK),
                 cols(GLA_DK), cols(GLA_DK),
                 rows(MOBA_W), rows(MOBA_W), rows(MOBA_W),
                 pl.BlockSpec((1, 1, MOBA_W), lambda i: (i, 0, 0))]
    w_bytes = sum(int(a.size) * a.dtype.itemsize for a in weights)
    out_bytes = sum(tm * (s.shape[1] if s.shape[0] == m else s.shape[0]) * s.dtype.itemsize for s in out_shape[:-1])
    need = w_bytes + 2 * out_bytes + 2 * tm * D_MODEL * 4 + 8 * tm * MOBA_W * 4
    return pl.pallas_call(
        _proj_body,
        grid=(n_tiles,),
        in_specs=[rows(D_MODEL), _resident((1, D_MODEL))] + [_resident(a.shape) for a in weights]
                 + [rope_spec, rope_spec],
        out_specs=out_specs,
        out_shape=out_shape,
        compiler_params=pltpu.CompilerParams(dimension_semantics=("parallel",),
                                             vmem_limit_bytes=_vmem_limit(need)),
        name="mixer_input_projection",
    )(x, mix_g, *weights, rope_cos, rope_sin)


def _gla_tables():
    c = GLA_CHUNK
    t = np.arange(c)
    le = t[None, :] <= t[:, None]
    gt = t[None, :] > t[:, None]
    sums = [le]
    masks = []
    s = c // 2
    while s >= 1:
        same = (t // (2 * s))[:, None] == (t // (2 * s))[None, :]
        right = (t % (2 * s)) >= s
        sums.append(same & right[:, None] & right[None, :] & le)
        sums.append(same & ~right[:, None] & ~right[None, :] & gt)
        masks.append(same & right[:, None] & ~right[None, :])
        s //= 2
    masks.append(np.eye(c, dtype=bool))
    tail = np.concatenate([gt.T, np.ones((c, c), dtype=bool)], axis=1)
    return (np.concatenate(sums, 0).astype(np.float32), np.stack(masks).astype(np.float32),
            tail.astype(np.float32))


def _gla_body(q_ref, k_ref, la_ref, kt_ref, lat_ref, v_ref, gg_ref, ng_ref, sums_ref, masks_ref, tail_ref,
              o_ref, s_ref):
    c = GLA_CHUNK
    n_lvl = masks_ref.shape[0] - 1

    @pl.when(pl.program_id(1) == 0)
    def _():
        s_ref[...] = jnp.zeros_like(s_ref)

    sums = sums_ref[...]
    tail = tail_ref[...]
    for hd in range(GLA_HEADS):
        ks = slice(hd * GLA_HDK, (hd + 1) * GLA_HDK)
        vs = slice(hd * GLA_HDV, (hd + 1) * GLA_HDV)
        q = q_ref[:, ks]
        k = k_ref[:, ks]
        v = v_ref[:, vs]
        state = s_ref[0, hd]
        e = jnp.exp(sum(_dot(sums, p) for p in _split3(la_ref[:, ks])))
        et = jnp.exp(sum(_dot(p, tail) for p in _split3(lat_ref[ks, :])))
        out = _dot((q * e[:c]).astype(BF16), state.astype(BF16))
        attn = masks_ref[n_lvl] * _dot_nt(q.astype(BF16), k.astype(BF16))
        for lv in range(n_lvl):
            eq = e[(1 + 2 * lv) * c:(2 + 2 * lv) * c]
            ek = e[(2 + 2 * lv) * c:(3 + 2 * lv) * c]
            attn = attn + masks_ref[lv] * _dot_nt((q * eq).astype(BF16), (k * ek).astype(BF16))
        out = out + _dot(attn.astype(BF16), v)
        k_dec = (kt_ref[ks, :] * et[:, :c]).astype(BF16)
        decay = et[:, c:]
        s_ref[0, hd] = jnp.concatenate([state[:, :c] * decay, state[:, c:] * decay], axis=1) + _dot(k_dec, v)
        gate = gg_ref[:, vs].astype(F32)
        o_ref[:, vs] = (_rms_norm(out, ng_ref[...]) * (gate * jax.nn.sigmoid(gate))).astype(BF16)


def _gla_prompt(gq, gk, la, gk_t, la_t, gv, gg, norm_g, batch, seq):
    c = GLA_CHUNK
    n_chunks = seq // c
    sums, masks, tail = (jnp.asarray(a, BF16 if i != 1 else F32) for i, a in enumerate(_gla_tables()))

    def rows(width):
        return pl.BlockSpec((c, width), lambda b, t: (b * n_chunks + t, 0))

    cols = pl.BlockSpec((GLA_DK, c), lambda b, t: (0, b * n_chunks + t))
    need = 2 * (5 * c * GLA_DK * 4 + 3 * c * GLA_DV * 2) + 4 * GLA_DK * GLA_HDV * 4 + (8 << 20)
    return pl.pallas_call(
        _gla_body,
        grid=(batch, n_chunks),
        in_specs=[rows(GLA_DK), rows(GLA_DK), rows(GLA_DK), cols, cols, rows(GLA_DV), rows(GLA_DV),
                  _resident((1, GLA_HDV)), _resident(sums.shape), _resident(masks.shape), _resident(tail.shape)],
        out_specs=[rows(GLA_DV),
                   pl.BlockSpec((1, GLA_HEADS, GLA_HDK, GLA_HDV), lambda b, t: (b, 0, 0, 0))],
        out_shape=[jax.ShapeDtypeStruct((batch * seq, GLA_DV), BF16),
                   jax.ShapeDtypeStruct((batch, GLA_HEADS, GLA_HDK, GLA_HDV), F32)],
        compiler_params=pltpu.CompilerParams(dimension_semantics=("parallel", "arbitrary"),
                                             vmem_limit_bytes=_vmem_limit(need)),
        name="gla_prompt_chunks",
    )(gq, gk, la, gk_t, la_t, gv, gg, norm_g, sums, masks, tail)


def _moba_select_body(q_ref, ksum_ref, qe_ref):
    own = pl.program_id(1).astype(F32)
    tq = q_ref.shape[0]
    nb = ksum_ref.shape[1]
    col = lax.broadcasted_iota(jnp.int32, (tq, LANES), 1).astype(F32)
    pad = jnp.zeros((LANES - nb, MOBA_HD), F32)
    for hd in range(MOBA_HEADS):
        sl = slice(hd * MOBA_HD, (hd + 1) * MOBA_HD)
        q = q_ref[:, sl]
        means = jnp.concatenate([ksum_ref[0, :, sl] * (1.0 / MOBA_BLOCK), pad], axis=0)
        q_hi = q.astype(BF16)
        q_lo = (q - q_hi.astype(F32)).astype(BF16)
        m_hi = means.astype(BF16)
        m_lo = (means - m_hi.astype(F32)).astype(BF16)
        gate = _dot_nt(q_hi, m_hi) + _dot_nt(q_hi, m_lo) + _dot_nt(q_lo, m_hi)
        valid = col < own
        gate = jnp.where(valid, gate, -jnp.inf)
        picked = col == own
        for _ in range(MOBA_TOPK):
            best = jnp.max(gate, axis=1, keepdims=True)
            first = jnp.min(jnp.where(gate == best, col, float(LANES)), axis=1, keepdims=True)
            pick = (col == first) & valid
            picked = picked | pick
            gate = jnp.where(pick, -jnp.inf, gate)
        qe_ref[0, hd, :, :MOBA_HD] = (q * (MOBA_HD ** -0.5 * LOG2_E)).astype(BF16)
        qe_ref[0, hd, :, MOBA_HD:] = jnp.where(picked, 0.0, -1.0).astype(BF16)


def _moba_select(q, ksum, batch, seq):
    tq = MOBA_BLOCK
    nb = seq // tq
    return pl.pallas_call(
        _moba_select_body,
        grid=(batch, nb),
        in_specs=[pl.BlockSpec((tq, MOBA_W), lambda b, i: (b * nb + i, 0)),
                  pl.BlockSpec((1, nb, MOBA_W), lambda b, i: (b, 0, 0))],
        out_specs=pl.BlockSpec((1, MOBA_HEADS, tq, 2 * MOBA_HD), lambda b, i: (b, 0, i, 0)),
        out_shape=jax.ShapeDtypeStruct((batch, MOBA_HEADS, seq, 2 * MOBA_HD), BF16),
        compiler_params=pltpu.CompilerParams(dimension_semantics=("parallel", "parallel"),
                                             vmem_limit_bytes=_vmem_limit(16 << 20)),
        name="moba_block_select",
    )(q, ksum)


def _moba_attend_body(qe_ref, k_ref, v_ref, o_ref, acc_ref, m_ref):
    j = pl.program_id(2)
    tb = MOBA_BLOCK
    n_groups = qe_ref.shape[2] // (tb * MOBA_GROUP)

    @pl.when(j == 0)
    def _():
        acc_ref[...] = jnp.zeros_like(acc_ref)
        m_ref[...] = jnp.full_like(m_ref, NEG_INIT)

    k = k_ref[...].astype(BF16)
    lane = lax.broadcasted_iota(jnp.int32, (tb, MOBA_HD), 1)
    k_ext = jnp.concatenate([k, jnp.where(lane == j, MASK_BIG, 0.0).astype(BF16)], axis=1)
    v_ext = jnp.concatenate([v_ref[...].astype(BF16), jnp.ones((tb, MOBA_HD), BF16)], axis=1)

    def update(rows, s):
        m_old = m_ref[rows, :]
        m_new = jnp.maximum(m_old, jnp.max(s, axis=1, keepdims=True))
        p = jnp.exp2(s - jnp.concatenate([m_new, m_new], axis=1)).astype(BF16)
        alpha = jnp.exp2(m_old - m_new)
        acc_new = jnp.concatenate([alpha, alpha], axis=1) * acc_ref[rows, :] + _dot(p, v_ext)
        m_ref[rows, :] = m_new
        acc_ref[rows, :] = acc_new
        return acc_new

    rows_per_group = tb * MOBA_GROUP

    def group_rows(g):
        return pl.ds(pl.multiple_of(g * rows_per_group, rows_per_group), rows_per_group)

    def scores(g):
        return _dot_nt(qe_ref[0, 0, group_rows(g), :], k_ext)

    first = j // MOBA_GROUP
    q_pos = first * rows_per_group + lax.broadcasted_iota(jnp.int32, (rows_per_group, tb), 0)
    k_pos = j * tb + lax.broadcasted_iota(jnp.int32, (rows_per_group, tb), 1)
    update(group_rows(first), jnp.where(k_pos <= q_pos, scores(first), -MASK_BIG))
    acc = acc_ref[pl.ds(pl.multiple_of(j * tb, tb), tb), :]
    o_ref[...] = (acc[:, :MOBA_HD] / acc[:, MOBA_HD:]).astype(BF16)

    def body(g, s):
        s_next = scores(jnp.minimum(g + 1, n_groups - 1))
        update(group_rows(g), s)
        return s_next

    lax.fori_loop(first + 1, n_groups, body, scores(jnp.minimum(first + 1, n_groups - 1)))


def _moba_attend(qe, k, v, batch, seq):
    tb = MOBA_BLOCK
    nb = seq // tb
    kv = pl.BlockSpec((tb, MOBA_HD), lambda b, h, j: (b * nb + j, h))
    need = 2 * seq * 2 * MOBA_HD * 2 + 3 * seq * MOBA_HD * 4 + (12 << 20)
    return pl.pallas_call(
        _moba_attend_body,
        grid=(batch, MOBA_HEADS, nb),
        in_specs=[pl.BlockSpec((1, 1, seq, 2 * MOBA_HD), lambda b, h, j: (b, h, 0, 0)), kv, kv],
        out_specs=kv,
        out_shape=jax.ShapeDtypeStruct((batch * seq, MOBA_W), BF16),
        scratch_shapes=[pltpu.VMEM((seq, 2 * MOBA_HD), F32), pltpu.VMEM((seq, MOBA_HD), F32)],
        compiler_params=pltpu.CompilerParams(dimension_semantics=("parallel", "parallel", "arbitrary"),
                                             vmem_limit_bytes=_vmem_limit(need)),
        name="moba_prompt_attention",
    )(qe, k, v)


def _merge_body(x_ref, g_ref, oa_ref, ob_ref, wga, wgb, wpa, wpb, wo, o_ref):
    x = x_ref[...]
    h = _rms_norm(x, g_ref[...]).astype(BF16)
    mix = jax.nn.sigmoid(_dot(h, wga[...])) * _dot(oa_ref[...], wpa[...])
    mix = mix + jax.nn.sigmoid(_dot(h, wgb[...])) * _dot(ob_ref[...], wpb[...])
    o_ref[...] = x + _dot(mix.astype(BF16), wo[...])


def _merge(x, mix_g, o_gla, o_moba, w):
    m = x.shape[0]
    tm = min(FFN_ROWS, m)
    row32 = pl.BlockSpec((tm, D_MODEL), lambda i: (i, 0))
    sq = _resident((D_MODEL, D_MODEL))
    need = 5 * D_MODEL * D_MODEL * 2 + 4 * tm * D_MODEL * 4 + 4 * tm * D_MODEL * 2 + 6 * tm * D_MODEL * 4
    return pl.pallas_call(
        _merge_body,
        grid=(m // tm,),
        in_specs=[row32, _resident((1, D_MODEL)), row32, row32, sq, sq, sq, sq, sq],
        out_specs=row32,
        out_shape=jax.ShapeDtypeStruct((m, D_MODEL), F32),
        compiler_params=pltpu.CompilerParams(dimension_semantics=("parallel",),
                                             vmem_limit_bytes=_vmem_limit(need)),
        name="gated_merge_projection",
    )(x, mix_g, o_gla, o_moba, w["ga"], w["gb"], w["pa"], w["pb"], w["o"])


def _gla_sample_body(q_ref, k_ref, la_ref, v_ref, gg_ref, s_ref, ng_ref, o_ref, so_ref):
    for hd in range(GLA_HEADS):
        v = v_ref[0, hd].astype(F32)
        s_new = jnp.exp(la_ref[0, hd]) * s_ref[0, hd] + k_ref[0, hd] * v
        so_ref[0, hd] = s_new
        out = jnp.sum(q_ref[0, hd] * s_new, axis=0, keepdims=True)
        gate = gg_ref[0, hd].astype(F32)
        o_ref[0, hd] = (_rms_norm(out, ng_ref[...]) * (gate * jax.nn.sigmoid(gate))).astype(BF16)


def _gla_sample(gq, gk, la, gv, gg, state, norm_g):
    n = gq.shape[0]
    col = pl.BlockSpec((1, GLA_HEADS, GLA_HDK, 1), lambda i: (i, 0, 0, 0))
    row = pl.BlockSpec((1, GLA_HEADS, 1, GLA_HDV), lambda i: (i, 0, 0, 0))
    st = pl.BlockSpec((1, GLA_HEADS, GLA_HDK, GLA_HDV), lambda i: (i, 0, 0, 0))
    as_col = lambda a: a.reshape(n, GLA_HEADS, GLA_HDK, 1)
    as_row = lambda a: a.reshape(n, GLA_HEADS, 1, GLA_HDV)
    o, s_new = pl.pallas_call(
        _gla_sample_body,
        grid=(n,),
        in_specs=[col, col, col, row, row, st, _resident((1, GLA_HDV))],
        out_specs=[row, st],
        out_shape=[jax.ShapeDtypeStruct((n, GLA_HEADS, 1, GLA_HDV), BF16),
                   jax.ShapeDtypeStruct((n, GLA_HEADS, GLA_HDK, GLA_HDV), F32)],
        compiler_params=pltpu.CompilerParams(dimension_semantics=("parallel",),
                                             vmem_limit_bytes=_vmem_limit(16 << 20)),
        name="gla_sample_step",
    )(as_col(gq), as_col(gk), as_col(la), as_row(gv), as_row(gg), state, norm_g)
    return o.reshape(n, GLA_DV), s_new


def _page_sum_body(pt_ref, *refs):
    pages, o_ref = refs[:PAGES_PER_STEP], refs[PAGES_PER_STEP]
    per_block = MOBA_BLOCK // PAGE_SIZE
    for blk in range(PAGES_PER_STEP // per_block):
        tot = jnp.sum(pages[blk * per_block][...], axis=0)
        for p in range(1, per_block):
            tot = tot + jnp.sum(pages[blk * per_block + p][...], axis=0)
        o_ref[0, 0, blk] = tot


def _page_block_sums(cache, layer, page_table_flat, n_seq, n_pages):
    groups = n_pages // PAGES_PER_STEP
    blocks_per_step = PAGES_PER_STEP * PAGE_SIZE // MOBA_BLOCK

    def page_spec(i):
        return pl.BlockSpec((None, None, PAGE_SIZE, MOBA_HEADS, MOBA_HD),
                            lambda s, g, pt: (layer, pt[s * n_pages + g * PAGES_PER_STEP + i], 0, 0, 0))

    sums = pl.pallas_call(
        _page_sum_body,
        grid_spec=pltpu.PrefetchScalarGridSpec(
            num_scalar_prefetch=1,
            grid=(n_seq, groups),
            in_specs=[page_spec(i) for i in range(PAGES_PER_STEP)],
            out_specs=pl.BlockSpec((1, 1, blocks_per_step, MOBA_HEADS, MOBA_HD), lambda s, g, pt: (s, g, 0, 0, 0)),
        ),
        out_shape=jax.ShapeDtypeStruct((n_seq, groups, blocks_per_step, MOBA_HEADS, MOBA_HD), F32),
        compiler_params=pltpu.CompilerParams(dimension_semantics=("parallel", "parallel"),
                                             vmem_limit_bytes=_vmem_limit(3 * PAGES_PER_STEP * PAGE_SIZE * MOBA_W * 4)),
        name="paged_key_block_sums",
    )(page_table_flat, *([cache] * PAGES_PER_STEP))
    return sums.reshape(n_seq, groups * blocks_per_step, MOBA_W)


def _sample_select_body(q_ref, ksum_ref, sel_ref):
    nb = ksum_ref.shape[1]
    q = q_ref[0]
    lane = lax.broadcasted_iota(jnp.int32, (nb, LANES), 1)
    row = lax.broadcasted_iota(jnp.int32, (nb, LANES), 0).astype(F32)
    gate = jnp.full((nb, LANES), -jnp.inf, F32)
    for hd in range(MOBA_HEADS):
        sl = slice(hd * MOBA_HD, (hd + 1) * MOBA_HD)
        g = jnp.sum(ksum_ref[0, :, sl] * (1.0 / MOBA_BLOCK) * q[:, sl], axis=1, keepdims=True)
        gate = jnp.where(lane == hd, g, gate)
    out_row = lax.broadcasted_iota(jnp.int32, (SUBLANES, LANES), 0)
    out = jnp.zeros((SUBLANES, LANES), jnp.int32)
    for r in range(MOBA_TOPK):
        best = jnp.max(gate, axis=0, keepdims=True)
        first = jnp.min(jnp.where(gate == best, row, float(nb)), axis=0, keepdims=True)
        out = jnp.where(out_row == r, first.astype(jnp.int32), out)
        gate = jnp.where(row == first, -jnp.inf, gate)
    sel_ref[0] = out


def _sample_select(q, ksum):
    n, nb = ksum.shape[0], ksum.shape[1]
    sel = pl.pallas_call(
        _sample_select_body,
        grid=(n,),
        in_specs=[pl.BlockSpec((1, 1, MOBA_W), lambda i: (i, 0, 0)),
                  pl.BlockSpec((1, nb, MOBA_W), lambda i: (i, 0, 0))],
        out_specs=pl.BlockSpec((1, SUBLANES, LANES), lambda i: (i, 0, 0)),
        out_shape=jax.ShapeDtypeStruct((n, SUBLANES, LANES), jnp.int32),
        compiler_params=pltpu.CompilerParams(dimension_semantics=("parallel",)),
        name="moba_sample_select",
    )(q.reshape(n, 1, MOBA_W), ksum)
    return jnp.transpose(sel[:, :MOBA_TOPK, :MOBA_HEADS], (0, 2, 1)).reshape(-1)


def _sample_attend_body(pt_ref, sel_ref, q_ref, kn_ref, vn_ref, ck_hbm, cv_hbm, o_ref, k_buf, v_buf, sem,
                        *, layer, n_pages):
    per_block = MOBA_BLOCK // PAGE_SIZE
    n_sel = MOBA_TOPK * per_block
    seq = pl.program_id(0)
    scale = MOBA_HD ** -0.5

    def page_copies(hd, i):
        blk = sel_ref[(seq * MOBA_HEADS + hd) * MOBA_TOPK + i // per_block]
        page = pt_ref[seq * n_pages + blk * per_block + i % per_block]
        slot = hd * n_sel + i
        return (pltpu.make_async_copy(ck_hbm.at[layer, page, :, hd, :], k_buf.at[slot], sem.at[0, slot]),
                pltpu.make_async_copy(cv_hbm.at[layer, page, :, hd, :], v_buf.at[slot], sem.at[1, slot]))

    for hd in range(MOBA_HEADS):
        for i in range(n_sel):
            for cp in page_copies(hd, i):
                cp.start()
    for hd in range(MOBA_HEADS):
        for i in range(n_sel):
            for cp in page_copies(hd, i):
                cp.wait()

    for hd in range(MOBA_HEADS):
        sl = slice(hd * MOBA_HD, (hd + 1) * MOBA_HD)
        q = q_ref[0, :, sl]
        s_new = jnp.sum(kn_ref[0, :, sl] * q, axis=1, keepdims=True) * scale
        scores = [jnp.sum(k_buf[hd * n_sel + i] * q, axis=1, keepdims=True) * scale for i in range(n_sel)]
        top = s_new
        for s in scores:
            top = jnp.maximum(top, jnp.max(s, axis=0, keepdims=True))
        p_new = jnp.exp(s_new - top)
        denom = p_new
        acc = p_new * vn_ref[0, :, sl]
        for i, s in enumerate(scores):
            p = jnp.exp(s - top)
            denom = denom + jnp.sum(p, axis=0, keepdims=True)
            acc = acc + jnp.sum(p * v_buf[hd * n_sel + i], axis=0, keepdims=True)
        o_ref[0, :, sl] = (acc / denom).astype(BF16)


def _sample_attend(q, k_new, v_new, cache_k, cache_v, layer, page_table_flat, sel_flat, n_pages):
    n = q.shape[0]
    n_slots = MOBA_HEADS * MOBA_TOPK * (MOBA_BLOCK // PAGE_SIZE)
    tok = pl.BlockSpec((1, 1, MOBA_W), lambda s, pt, sel: (s, 0, 0))
    hbm = pl.BlockSpec(memory_space=pl.ANY)
    as_tok = lambda a: a.reshape(n, 1, MOBA_W)
    out = pl.pallas_call(
        functools.partial(_sample_attend_body, layer=layer, n_pages=n_pages),
        grid_spec=pltpu.PrefetchScalarGridSpec(
            num_scalar_prefetch=2,
            grid=(n,),
            in_specs=[tok, tok, tok, hbm, hbm],
            out_specs=tok,
            scratch_shapes=[pltpu.VMEM((n_slots, PAGE_SIZE, MOBA_HD), F32),
                            pltpu.VMEM((n_slots, PAGE_SIZE, MOBA_HD), F32),
                            pltpu.SemaphoreType.DMA((2, n_slots))],
        ),
        out_shape=jax.ShapeDtypeStruct((n, 1, MOBA_W), BF16),
        compiler_params=pltpu.CompilerParams(dimension_semantics=("arbitrary",),
                                             vmem_limit_bytes=_vmem_limit(3 * n_slots * PAGE_SIZE * MOBA_HD * 4)),
        name="moba_sample_attention",
    )(page_table_flat, sel_flat, as_tok(q), as_tok(k_new), as_tok(v_new), cache_k, cache_v)
    return out.reshape(n, MOBA_W)


def _rope_tables(pos):
    half = MOBA_HD // 2
    inv = ROPE_THETA ** (-jnp.arange(half, dtype=F32) / half)
    ang = pos.astype(F32)[:, None] * inv[None, :]
    cos, sin = jnp.cos(ang), jnp.sin(ang)
    return jnp.concatenate([cos, cos], axis=1), jnp.concatenate([-sin, sin], axis=1)


def _layer_weights(w_in, w_a2, b_a, q_norm_g, k_norm_g, w_pa, w_pb, w_o):
    offs = np.concatenate([[0], np.cumsum(IN_SPLITS)])
    gq, gk, gv, gg, glr, mq, mk, mv, ga, gb = (w_in[:, offs[i]:offs[i + 1]] for i in range(len(IN_SPLITS)))
    glr = jnp.pad(glr, ((0, 0), (0, LANES - GLA_RANK)))
    a2 = jnp.pad(w_a2, ((0, LANES - GLA_RANK), (0, 0)))
    b16 = lambda a: a.astype(BF16)
    return dict(
        gq=b16(gq), gk=b16(gk), gv=b16(gv), gg=b16(gg), glr=b16(glr), a2=b16(a2), ba=b_a.reshape(1, GLA_DK),
        gk_t=b16(gk.T), glr_t=b16(glr.T), a2_t=b16(a2.T), ba_t=b_a.reshape(GLA_DK, 1),
        mq=b16(mq), mk=b16(mk), mv=b16(mv),
        qg=jnp.tile(q_norm_g, MOBA_HEADS).reshape(1, MOBA_W), kg=jnp.tile(k_norm_g, MOBA_HEADS).reshape(1, MOBA_W),
        ga=b16(ga), gb=b16(gb), pa=b16(w_pa), pb=b16(w_pb), o=b16(w_o))


def kernel(x_prompt, x_sample, cache_k, cache_v, state_gla, page_table, ffn1_g, ffn1_wg, ffn1_wu, ffn1_wd, mix_g, w_in, w_a2, b_a, gla_norm_g, q_norm_g, k_norm_g, w_pa, w_pb, w_o, ffn2_g, ffn2_wg, ffn2_wu, ffn2_wd):
    batch, seq, _ = x_prompt.shape
    n_dec, dec_seq, _ = x_sample.shape
    n_pages = page_table.shape[1]
    depth = w_in.shape[0]
    past_len = n_pages * PAGE_SIZE
    assert dec_seq == 1 and seq % MOBA_BLOCK == 0 and past_len % MOBA_BLOCK == 0
    assert past_len // MOBA_BLOCK >= MOBA_TOPK and n_pages % PAGES_PER_STEP == 0
    assert (seq // MOBA_BLOCK) % MOBA_GROUP == 0

    yp = x_prompt.reshape(batch * seq, D_MODEL)
    ys = x_sample.reshape(n_dec, D_MODEL)
    rope_p = _rope_tables(jnp.arange(seq, dtype=jnp.int32))
    rope_s = _rope_tables(jnp.full((n_dec,), past_len, jnp.int32))
    pt_flat = page_table.reshape(-1)
    row = lambda a: a.reshape(1, -1)
    outs = [[] for _ in range(6)]
    for l in range(depth):
        w = _layer_weights(w_in[l], w_a2[l], b_a[l], q_norm_g[l], k_norm_g[l], w_pa[l], w_pb[l], w_o[l])
        ffn1 = (row(ffn1_g[l]), ffn1_wg[l].astype(BF16), ffn1_wu[l].astype(BF16), ffn1_wd[l].astype(BF16))
        ffn2 = (row(ffn2_g[l]), ffn2_wg[l].astype(BF16), ffn2_wu[l].astype(BF16), ffn2_wd[l].astype(BF16))
        norm_g = row(gla_norm_g[l])

        x1 = _ffn(yp, *ffn1)
        gq, gk, gv, gg, la, gk_t, la_t, q, k, v, ksum = _proj(x1, row(mix_g[l]), w, *rope_p)
        o_gla, s_prompt = _gla_prompt(gq, gk, la, gk_t, la_t, gv, gg, norm_g, batch, seq)
        qe = _moba_select(q, ksum.reshape(batch, seq // MOBA_BLOCK, MOBA_W), batch, seq)
        o_moba = _moba_attend(qe, k, v, batch, seq)
        yp = _ffn(_merge(x1, row(mix_g[l]), o_gla, o_moba, w), *ffn2)
        outs[0].append(k.reshape(batch, seq, MOBA_HEADS, MOBA_HD))
        outs[1].append(v.reshape(batch, seq, MOBA_HEADS, MOBA_HD))
        outs[2].append(s_prompt)

        x1 = _ffn(ys, *ffn1)
        gq, gk, gv, gg, la, _, _, q, k, v, _ = _proj(x1, row(mix_g[l]), w, *rope_s)
        o_gla, s_sample = _gla_sample(gq, gk, la, gv, gg, state_gla[l], norm_g)
        sel = _sample_select(q, _page_block_sums(cache_k, l, pt_flat, n_dec, n_pages))
        o_moba = _sample_attend(q, k, v, cache_k, cache_v, l, pt_flat, sel, n_pages)
        ys = _ffn(_merge(x1, row(mix_g[l]), o_gla, o_moba, w), *ffn2)
        outs[3].append(k.reshape(n_dec, 1, MOBA_HEADS, MOBA_HD))
        outs[4].append(v.reshape(n_dec, 1, MOBA_HEADS, MOBA_HD))
        outs[5].append(s_sample)

    return (yp.reshape(batch, seq, D_MODEL), ys.reshape(n_dec, 1, D_MODEL),
            jnp.stack(outs[0]), jnp.stack(outs[1]), jnp.stack(outs[2]),
            jnp.stack(outs[3]), jnp.stack(outs[4]), jnp.stack(outs[5]))
```

```python
import functools

import numpy as np
import jax
import jax.numpy as jnp
from jax import lax
from jax.experimental import pallas as pl
from jax.experimental.pallas import tpu as pltpu

F32 = jnp.float32
BF16 = jnp.bfloat16

D_MODEL = 1024
D_FF = 2816
GLA_HEADS = 4
GLA_HDK = 128
GLA_HDV = 256
GLA_DK = GLA_HEADS * GLA_HDK
GLA_DV = GLA_HEADS * GLA_HDV
GLA_RANK = 16
GLA_TAU = 16.0
MOBA_HEADS = 8
MOBA_HD = 128
MOBA_W = MOBA_HEADS * MOBA_HD
MOBA_BLOCK = 256
MOBA_TOPK = 3
ROPE_THETA = 10000.0
EPS = 1e-6
PAGE_SIZE = 128
IN_SPLITS = (GLA_DK, GLA_DK, GLA_DV, GLA_DV, GLA_RANK, MOBA_W, MOBA_W, MOBA_W, D_MODEL, D_MODEL)

LANES = 128
SUBLANES = 8
VMEM_BYTES = 64 * 1024 * 1024

FF_CHUNK = 256
FFN_ROWS = 512
PROJ_ROWS = MOBA_BLOCK
GLA_CHUNK = 128
MOBA_GROUP = 4
MOBA_KV_BLOCKS = 2
MASK_BIG = 2.0 ** 100
NEG_INIT = -1.0e38
LOG2_E = 1.4426950408889634


def _vmem_limit(nbytes):
    return int(min(VMEM_BYTES - (4 << 20), max(nbytes, 16 << 20)))


def _resident(shape):
    return pl.BlockSpec(shape, lambda *_: (0,) * len(shape), pipeline_mode=pl.Buffered(1))


def _dot(a, b):
    return jnp.dot(a, b, preferred_element_type=F32)


def _dot_nt(a, b):
    return lax.dot_general(a, b, (((1,), (1,)), ((), ())), preferred_element_type=F32)


def _rms_norm(x, g):
    return x * lax.rsqrt(jnp.mean(x * x, axis=-1, keepdims=True) + EPS) * g


def _log_sigmoid(z):
    return jnp.minimum(z, 0.0) - jnp.log1p(jnp.exp(-jnp.abs(z)))


def _split3(x):
    hi = x.astype(BF16)
    r = x - hi.astype(F32)
    mid = r.astype(BF16)
    lo = (r - mid.astype(F32)).astype(BF16)
    return hi, mid, lo


def _ffn_body(x_ref, g_ref, wg_ref, wu_ref, wd_ref, o_ref):
    x = x_ref[...]
    h = _rms_norm(x, g_ref[...]).astype(BF16)
    acc = jnp.zeros_like(x)
    for c in range(D_FF // FF_CHUNK):
        sl = slice(c * FF_CHUNK, (c + 1) * FF_CHUNK)
        a = _dot(h, wg_ref[:, sl])
        u = _dot(h, wu_ref[:, sl])
        act = (a * jax.nn.sigmoid(a) * u).astype(BF16)
        acc = acc + _dot(act, wd_ref[sl, :])
    o_ref[...] = x + 0.5 * acc


def _ffn(x, g, wg, wu, wd):
    m = x.shape[0]
    tm = min(FFN_ROWS, m)
    row = pl.BlockSpec((tm, D_MODEL), lambda i: (i, 0))
    need = 3 * D_MODEL * D_FF * 2 + 4 * tm * D_MODEL * 4 + 6 * tm * D_MODEL * 4
    return pl.pallas_call(
        _ffn_body,
        grid=(m // tm,),
        in_specs=[row, _resident((1, D_MODEL)), _resident((D_MODEL, D_FF)), _resident((D_MODEL, D_FF)),
                  _resident((D_FF, D_MODEL))],
        out_specs=row,
        out_shape=jax.ShapeDtypeStruct((m, D_MODEL), F32),
        compiler_params=pltpu.CompilerParams(dimension_semantics=("parallel",),
                                             vmem_limit_bytes=_vmem_limit(need)),
        name="swiglu_half_step",
    )(x, g, wg, wu, wd)


def _proj_body(x_ref, g_ref, wgq, wgk, wgv, wgg, wglr, wa2, ba, wgk_t, wglr_t, wa2_t, ba_t,
               wmq, wmk, wmv, qg, kg, rc, rs,
               gq_o, gk_o, gv_o, gg_o, la_o, gkt_o, lat_o, q_o, k_o, v_o, ksum_o):
    h = _rms_norm(x_ref[...], g_ref[...]).astype(BF16)
    gq_o[...] = _dot(h, wgq[...]) * (GLA_HDK ** -0.5)
    gk_o[...] = _dot(h, wgk[...])
    gv_o[...] = _dot(h, wgv[...]).astype(BF16)
    gg_o[...] = _dot(h, wgg[...]).astype(BF16)
    glr = _dot(h, wglr[...]).astype(BF16)
    la_o[...] = _log_sigmoid(_dot(glr, wa2[...]) + ba[...]) * (1.0 / GLA_TAU)
    gkt_o[...] = _dot_nt(wgk_t[...], h)
    glr_t = _dot_nt(wglr_t[...], h).astype(BF16)
    lat_o[...] = _log_sigmoid(_dot(wa2_t[...], glr_t) + ba_t[...]) * (1.0 / GLA_TAU)
    mq = _dot(h, wmq[...])
    mk = _dot(h, wmk[...])
    mv = _dot(h, wmv[...])
    cos = rc[...]
    sin = rs[...]
    for hd in range(MOBA_HEADS):
        sl = slice(hd * MOBA_HD, (hd + 1) * MOBA_HD)
        qh = _rms_norm(mq[:, sl], qg[:, sl])
        q_o[:, sl] = qh * cos + pltpu.roll(qh, MOBA_HD // 2, 1) * sin
        kh = _rms_norm(mk[:, sl], kg[:, sl])
        k_o[:, sl] = kh * cos + pltpu.roll(kh, MOBA_HD // 2, 1) * sin
    v_o[...] = mv
    ksum_o[0] = jnp.sum(k_o[...], axis=0, keepdims=True)


def _proj(x, mix_g, w, rope_cos, rope_sin):
    m = x.shape[0]
    tm = min(PROJ_ROWS, m)
    n_tiles = m // tm
    n_rope = rope_cos.shape[0] // tm

    def rows(width):
        return pl.BlockSpec((tm, width), lambda i: (i, 0))

    def cols(height):
        return pl.BlockSpec((height, tm), lambda i: (0, i))

    rope_spec = pl.BlockSpec((tm, MOBA_HD), lambda i: (i % n_rope, 0))
    weights = [w["gq"], w["gk"], w["gv"], w["gg"], w["glr"], w["a2"], w["ba"], w["gk_t"], w["glr_t"],
               w["a2_t"], w["ba_t"], w["mq"], w["mk"], w["mv"], w["qg"], w["kg"]]
    out_shape = [
        jax.ShapeDtypeStruct((m, GLA_DK), F32), jax.ShapeDtypeStruct((m, GLA_DK), F32),
        jax.ShapeDtypeStruct((m, GLA_DV), BF16), jax.ShapeDtypeStruct((m, GLA_DV), BF16),
        jax.ShapeDtypeStruct((m, GLA_DK), F32),
        jax.ShapeDtypeStruct((GLA_DK, m), F32), jax.ShapeDtypeStruct((GLA_DK, m), F32),
        jax.ShapeDtypeStruct((m, MOBA_W), F32), jax.ShapeDtypeStruct((m, MOBA_W), F32),
        jax.ShapeDtypeStruct((m, MOBA_W), F32), jax.ShapeDtypeStruct((n_tiles, 1, MOBA_W), F32),
    ]
    out_specs = [rows(GLA_DK), rows(GLA_DK), rows(GLA_DV), rows(GLA_DV), rows(GLA_DK),
                 cols(GLA_DK), cols(GLA_DK),
                 rows(MOBA_W), rows(MOBA_W), rows(MOBA_W),
                 pl.BlockSpec((1, 1, MOBA_W), lambda i: (i, 0, 0))]
    w_bytes = sum(int(a.size) * a.dtype.itemsize for a in weights)
    out_bytes = sum(tm * (s.shape[1] if s.shape[0] == m else s.shape[0]) * s.dtype.itemsize for s in out_shape[:-1])
    need = w_bytes + 2 * out_bytes + 2 * tm * D_MODEL * 4 + 8 * tm * MOBA_W * 4
    return pl.pallas_call(
        _proj_body,
        grid=(n_tiles,),
        in_specs=[rows(D_MODEL), _resident((1, D_MODEL))] + [_resident(a.shape) for a in weights]
                 + [rope_spec, rope_spec],
        out_specs=out_specs,
        out_shape=out_shape,
        compiler_params=pltpu.CompilerParams(dimension_semantics=("parallel",),
                                             vmem_limit_bytes=_vmem_limit(need)),
        name="mixer_input_projection",
    )(x, mix_g, *weights, rope_cos, rope_sin)


def _gla_tables():
    c = GLA_CHUNK
    t = np.arange(c)
    le = t[None, :] <= t[:, None]
    gt = t[None, :] > t[:, None]
    masks = []
    s = c // 2
    while s >= 1:
        same = (t // (2 * s))[:, None] == (t // (2 * s))[None, :]
        right = (t % (2 * s)) >= s
        masks.append(same & right[:, None] & ~right[None, :])
        s //= 2
    masks.append(np.eye(c, dtype=bool))
    tail = np.concatenate([gt.T, np.ones((c, c), dtype=bool)], axis=1)
    return le.astype(np.float32), np.stack(masks).astype(np.float32), tail.astype(np.float32)


def _level_reference(b, s):
    c, dk = b.shape
    if s >= SUBLANES:
        blocks = b.reshape(c // (2 * s), 2 * s, dk)
        return jnp.broadcast_to(blocks[:, s - 1:s, :], blocks.shape).reshape(c, dk)
    rows8 = b.reshape(c // SUBLANES, SUBLANES, dk)
    sub = lax.broadcasted_iota(jnp.int32, rows8.shape, 1)
    ref = jnp.broadcast_to(rows8[:, SUBLANES - s - 1:SUBLANES - s, :], rows8.shape)
    for first in range(SUBLANES - 4 * s, -1, -2 * s):
        ref = jnp.where(sub < first + 2 * s, jnp.broadcast_to(rows8[:, first + s - 1:first + s, :], rows8.shape), ref)
    return ref.reshape(c, dk)


def _gla_body(q_ref, k_ref, la_ref, kt_ref, lat_ref, v_ref, gg_ref, ng_ref, le_ref, masks_ref, tail_ref,
              o_ref, s_ref):
    c = GLA_CHUNK
    n_lvl = masks_ref.shape[0] - 1

    @pl.when(pl.program_id(1) == 0)
    def _():
        s_ref[...] = jnp.zeros_like(s_ref)

    le = le_ref[...]
    tail = tail_ref[...]
    for hd in range(GLA_HEADS):
        ks = slice(hd * GLA_HDK, (hd + 1) * GLA_HDK)
        vs = slice(hd * GLA_HDV, (hd + 1) * GLA_HDV)
        q = q_ref[:, ks]
        k = k_ref[:, ks]
        v = v_ref[:, vs]
        state = s_ref[0, hd]
        b = sum(_dot(le, p) for p in _split3(la_ref[:, ks]))
        et = jnp.exp(sum(_dot(p, tail) for p in _split3(lat_ref[ks, :])))
        out = _dot((q * jnp.exp(b)).astype(BF16), state.astype(BF16))
        attn = masks_ref[n_lvl] * _dot_nt(q.astype(BF16), k.astype(BF16))
        for lv in range(n_lvl):
            e = jnp.exp(-jnp.abs(b - _level_reference(b, c >> (lv + 1))))
            attn = attn + masks_ref[lv] * _dot_nt((q * e).astype(BF16), (k * e).astype(BF16))
        out = out + _dot(attn.astype(BF16), v)
        k_dec = (kt_ref[ks, :] * et[:, :c]).astype(BF16)
        decay = et[:, c:]
        s_ref[0, hd] = jnp.concatenate([state[:, :c] * decay, state[:, c:] * decay], axis=1) + _dot(k_dec, v)
        gate = gg_ref[:, vs].astype(F32)
        o_ref[:, vs] = (_rms_norm(out, ng_ref[...]) * (gate * jax.nn.sigmoid(gate))).astype(BF16)


def _gla_prompt(gq, gk, la, gk_t, la_t, gv, gg, norm_g, batch, seq):
    c = GLA_CHUNK
    n_chunks = seq // c
    le, masks, tail = (jnp.asarray(a, BF16 if i != 1 else F32) for i, a in enumerate(_gla_tables()))

    def rows(width):
        return pl.BlockSpec((c, width), lambda b, t: (b * n_chunks + t, 0))

    cols = pl.BlockSpec((GLA_DK, c), lambda b, t: (0, b * n_chunks + t))
    need = 2 * (5 * c * GLA_DK * 4 + 3 * c * GLA_DV * 2) + 4 * GLA_DK * GLA_HDV * 4 + (8 << 20)
    return pl.pallas_call(
        _gla_body,
        grid=(batch, n_chunks),
        in_specs=[rows(GLA_DK), rows(GLA_DK), rows(GLA_DK), cols, cols, rows(GLA_DV), rows(GLA_DV),
                  _resident((1, GLA_HDV)), _resident(le.shape), _resident(masks.shape), _resident(tail.shape)],
        out_specs=[rows(GLA_DV),
                   pl.BlockSpec((1, GLA_HEADS, GLA_HDK, GLA_HDV), lambda b, t: (b, 0, 0, 0))],
        out_shape=[jax.ShapeDtypeStruct((batch * seq, GLA_DV), BF16),
                   jax.ShapeDtypeStruct((batch, GLA_HEADS, GLA_HDK, GLA_HDV), F32)],
        compiler_params=pltpu.CompilerParams(dimension_semantics=("parallel", "arbitrary"),
                                             vmem_limit_bytes=_vmem_limit(need)),
        name="gla_prompt_chunks",
    )(gq, gk, la, gk_t, la_t, gv, gg, norm_g, le, masks, tail)


def _moba_select_body(q_ref, ksum_ref, qe_ref):
    own = pl.program_id(1).astype(F32)
    tq = q_ref.shape[0]
    nb = ksum_ref.shape[1]
    gates = []
    for hd in range(MOBA_HEADS):
        sl = slice(hd * MOBA_HD, (hd + 1) * MOBA_HD)
        q = q_ref[:, sl]
        means = ksum_ref[0, :, sl] * (1.0 / MOBA_BLOCK)
        q_hi = q.astype(BF16)
        q_lo = (q - q_hi.astype(F32)).astype(BF16)
        m_hi = means.astype(BF16)
        m_lo = (means - m_hi.astype(F32)).astype(BF16)
        gates.append(_dot_nt(m_hi, q_hi) + _dot_nt(m_lo, q_hi) + _dot_nt(m_hi, q_lo))
        qe_ref[0, hd, :, :MOBA_HD] = (q * (MOBA_HD ** -0.5 * LOG2_E)).astype(BF16)
    gate = jnp.concatenate(gates, axis=1)
    blk = lax.broadcasted_iota(jnp.int32, gate.shape, 0).astype(F32)
    gate = jnp.where(blk < own, gate, -jnp.inf)
    mask = jnp.where(blk == own, 0.0, -1.0)
    for _ in range(MOBA_TOPK):
        best = jnp.max(gate, axis=0, keepdims=True)
        first = jnp.min(jnp.where(gate == best, blk, float(nb)), axis=0, keepdims=True)
        first = jnp.where(best > -jnp.inf, first, -1.0)
        pick = blk == first
        mask = jnp.where(pick, 0.0, mask)
        gate = jnp.where(pick, -jnp.inf, gate)
    mask = jnp.concatenate([mask, jnp.full((LANES - nb, mask.shape[1]), -1.0, F32)], axis=0).astype(BF16)
    eye = (lax.broadcasted_iota(jnp.int32, (tq, tq), 0) == lax.broadcasted_iota(jnp.int32, (tq, tq), 1))
    eye = jnp.where(eye, 1.0, 0.0).astype(BF16)
    for hd in range(MOBA_HEADS):
        qe_ref[0, hd, :, MOBA_HD:] = _dot_nt(eye, mask[:, hd * tq:(hd + 1) * tq]).astype(BF16)


def _moba_select(q, ksum, batch, seq):
    tq = MOBA_BLOCK
    nb = seq // tq
    return pl.pallas_call(
        _moba_select_body,
        grid=(batch, nb),
        in_specs=[pl.BlockSpec((tq, MOBA_W), lambda b, i: (b * nb + i, 0)),
                  pl.BlockSpec((1, nb, MOBA_W), lambda b, i: (b, 0, 0))],
        out_specs=pl.BlockSpec((1, MOBA_HEADS, tq, 2 * MOBA_HD), lambda b, i: (b, 0, i, 0)),
        out_shape=jax.ShapeDtypeStruct((batch, MOBA_HEADS, seq, 2 * MOBA_HD), BF16),
        compiler_params=pltpu.CompilerParams(dimension_semantics=("parallel", "parallel"),
                                             vmem_limit_bytes=_vmem_limit(16 << 20)),
        name="moba_block_select",
    )(q, ksum)


def _moba_attend_body(pt_ref, qe_ref, k_ref, v_ref, *refs):
    pages, (o_ref, ksum_ref, acc_ref, m_ref) = refs[:-4], refs[-4:]
    j = pl.program_id(2)
    tb = MOBA_BLOCK
    tkv = k_ref.shape[0]

    per_block = MOBA_BLOCK // PAGE_SIZE
    for blk in range(len(pages) // per_block):
        tot = jnp.sum(pages[blk * per_block][...], axis=0)
        for p in range(1, per_block):
            tot = tot + jnp.sum(pages[blk * per_block + p][...], axis=0)
        ksum_ref[0, blk] = tot
    n_groups = qe_ref.shape[2] // (tb * MOBA_GROUP)

    @pl.when(j == 0)
    def _():
        acc_ref[...] = jnp.zeros_like(acc_ref)
        m_ref[...] = jnp.full_like(m_ref, NEG_INIT)

    k = k_ref[...].astype(BF16)
    lane = lax.broadcasted_iota(jnp.int32, (tkv, MOBA_HD), 1)
    key_block = j * (tkv // tb) + lax.broadcasted_iota(jnp.int32, (tkv, MOBA_HD), 0) // tb
    k_ext = jnp.concatenate([k, jnp.where(lane == key_block, MASK_BIG, 0.0).astype(BF16)], axis=1)
    v_ext = jnp.concatenate([v_ref[...].astype(BF16), jnp.ones((tkv, MOBA_HD), BF16)], axis=1)

    def update(rows, s):
        m_old = m_ref[rows, :]
        m_new = jnp.maximum(m_old, jnp.max(s, axis=1, keepdims=True))
        p = jnp.exp2(s - jnp.concatenate([m_new] * (tkv // MOBA_HD), axis=1)).astype(BF16)
        alpha = jnp.exp2(m_old - m_new)
        acc_new = jnp.concatenate([alpha, alpha], axis=1) * acc_ref[rows, :] + _dot(p, v_ext)
        m_ref[rows, :] = m_new
        acc_ref[rows, :] = acc_new
        return acc_new

    rows_per_group = tb * MOBA_GROUP

    def group_rows(g):
        return pl.ds(pl.multiple_of(g * rows_per_group, rows_per_group), rows_per_group)

    def scores(g):
        return _dot_nt(qe_ref[0, 0, group_rows(g), :], k_ext)

    first = (j * tkv) // rows_per_group
    q_pos = first * rows_per_group + lax.broadcasted_iota(jnp.int32, (rows_per_group, tkv), 0)
    k_pos = j * tkv + lax.broadcasted_iota(jnp.int32, (rows_per_group, tkv), 1)
    update(group_rows(first), jnp.where(k_pos <= q_pos, scores(first), -MASK_BIG))
    acc = acc_ref[pl.ds(pl.multiple_of(j * tkv, tkv), tkv), :]
    o_ref[...] = (acc[:, :MOBA_HD] / acc[:, MOBA_HD:]).astype(BF16)

    def body(g, s):
        s_next = scores(jnp.minimum(g + 1, n_groups - 1))
        update(group_rows(g), s)
        return s_next

    lax.fori_loop(first + 1, n_groups, body, scores(jnp.minimum(first + 1, n_groups - 1)))


def _moba_attend(qe, k, v, batch, seq, cache, layer, page_table_flat):
    tkv = MOBA_BLOCK * MOBA_KV_BLOCKS
    nb = seq // tkv
    n_steps = batch * MOBA_HEADS * nb
    per_block = MOBA_BLOCK // PAGE_SIZE
    pages_per_step = page_table_flat.shape[0] // n_steps
    assert pages_per_step * n_steps == page_table_flat.shape[0] and pages_per_step % per_block == 0
    kv = pl.BlockSpec((tkv, MOBA_HD), lambda b, h, j, pt: (b * nb + j, h))

    def step(b, h, j):
        return (b * MOBA_HEADS + h) * nb + j

    def page_spec(i):
        return pl.BlockSpec((None, None, PAGE_SIZE, MOBA_HEADS, MOBA_HD),
                            lambda b, h, j, pt: (layer, pt[step(b, h, j) * pages_per_step + i], 0, 0, 0))

    blocks_per_step = pages_per_step // per_block
    need = (2 * seq * 2 * MOBA_HD * 2 + 3 * seq * MOBA_HD * 4 + 2 * pages_per_step * PAGE_SIZE * MOBA_W * 4
            + (16 << 20))
    out, sums = pl.pallas_call(
        _moba_attend_body,
        grid_spec=pltpu.PrefetchScalarGridSpec(
            num_scalar_prefetch=1,
            grid=(batch, MOBA_HEADS, nb),
            in_specs=[pl.BlockSpec((1, 1, seq, 2 * MOBA_HD), lambda b, h, j, pt: (b, h, 0, 0)), kv, kv]
                     + [page_spec(i) for i in range(pages_per_step)],
            out_specs=[kv, pl.BlockSpec((1, blocks_per_step, MOBA_HEADS, MOBA_HD),
                                        lambda b, h, j, pt: (step(b, h, j), 0, 0, 0))],
            scratch_shapes=[pltpu.VMEM((seq, 2 * MOBA_HD), F32), pltpu.VMEM((seq, MOBA_HD), F32)],
        ),
        out_shape=[jax.ShapeDtypeStruct((batch * seq, MOBA_W), BF16),
                   jax.ShapeDtypeStruct((n_steps, blocks_per_step, MOBA_HEADS, MOBA_HD), F32)],
        compiler_params=pltpu.CompilerParams(dimension_semantics=("parallel", "parallel", "arbitrary"),
                                             vmem_limit_bytes=_vmem_limit(need)),
        name="moba_prompt_attention",
    )(page_table_flat, qe, k, v, *([cache] * pages_per_step))
    return out, sums


def _merge_body(x_ref, g_ref, oa_ref, ob_ref, wga, wgb, wpa, wpb, wo, o_ref):
    x = x_ref[...]
    h = _rms_norm(x, g_ref[...]).astype(BF16)
    mix = jax.nn.sigmoid(_dot(h, wga[...])) * _dot(oa_ref[...], wpa[...])
    mix = mix + jax.nn.sigmoid(_dot(h, wgb[...])) * _dot(ob_ref[...], wpb[...])
    o_ref[...] = x + _dot(mix.astype(BF16), wo[...])


def _merge(x, mix_g, o_gla, o_moba, w):
    m = x.shape[0]
    tm = min(FFN_ROWS, m)
    row32 = pl.BlockSpec((tm, D_MODEL), lambda i: (i, 0))
    sq = _resident((D_MODEL, D_MODEL))
    need = 5 * D_MODEL * D_MODEL * 2 + 4 * tm * D_MODEL * 4 + 4 * tm * D_MODEL * 2 + 6 * tm * D_MODEL * 4
    return pl.pallas_call(
        _merge_body,
        grid=(m // tm,),
        in_specs=[row32, _resident((1, D_MODEL)), row32, row32, sq, sq, sq, sq, sq],
        out_specs=row32,
        out_shape=jax.ShapeDtypeStruct((m, D_MODEL), F32),
        compiler_params=pltpu.CompilerParams(dimension_semantics=("parallel",),
                                             vmem_limit_bytes=_vmem_limit(need)),
        name="gated_merge_projection",
    )(x, mix_g, o_gla, o_moba, w["ga"], w["gb"], w["pa"], w["pb"], w["o"])


def _gla_sample_body(q_ref, k_ref, la_ref, v_ref, gg_ref, s_ref, ng_ref, o_ref, so_ref):
    for hd in range(GLA_HEADS):
        v = v_ref[0, hd].astype(F32)
        s_new = jnp.exp(la_ref[0, hd]) * s_ref[0, hd] + k_ref[0, hd] * v
        so_ref[0, hd] = s_new
        out = jnp.sum(q_ref[0, hd] * s_new, axis=0, keepdims=True)
        gate = gg_ref[0, hd].astype(F32)
        o_ref[0, hd] = (_rms_norm(out, ng_ref[...]) * (gate * jax.nn.sigmoid(gate))).astype(BF16)


def _gla_sample(gq, gk, la, gv, gg, state, norm_g):
    n = gq.shape[0]
    col = pl.BlockSpec((1, GLA_HEADS, GLA_HDK, 1), lambda i: (i, 0, 0, 0))
    row = pl.BlockSpec((1, GLA_HEADS, 1, GLA_HDV), lambda i: (i, 0, 0, 0))
    st = pl.BlockSpec((1, GLA_HEADS, GLA_HDK, GLA_HDV), lambda i: (i, 0, 0, 0))
    as_col = lambda a: a.reshape(n, GLA_HEADS, GLA_HDK, 1)
    as_row = lambda a: a.reshape(n, GLA_HEADS, 1, GLA_HDV)
    o, s_new = pl.pallas_call(
        _gla_sample_body,
        grid=(n,),
        in_specs=[col, col, col, row, row, st, _resident((1, GLA_HDV))],
        out_specs=[row, st],
        out_shape=[jax.ShapeDtypeStruct((n, GLA_HEADS, 1, GLA_HDV), BF16),
                   jax.ShapeDtypeStruct((n, GLA_HEADS, GLA_HDK, GLA_HDV), F32)],
        compiler_params=pltpu.CompilerParams(dimension_semantics=("parallel",),
                                             vmem_limit_bytes=_vmem_limit(16 << 20)),
        name="gla_sample_step",
    )(as_col(gq), as_col(gk), as_col(la), as_row(gv), as_row(gg), state, norm_g)
    return o.reshape(n, GLA_DV), s_new


def _sample_select_body(q_ref, ksum_ref, sel_ref):
    nb = ksum_ref.shape[1]
    q = q_ref[0]
    lane = lax.broadcasted_iota(jnp.int32, (nb, LANES), 1)
    row = lax.broadcasted_iota(jnp.int32, (nb, LANES), 0).astype(F32)
    gate = jnp.full((nb, LANES), -jnp.inf, F32)
    for hd in range(MOBA_HEADS):
        sl = slice(hd * MOBA_HD, (hd + 1) * MOBA_HD)
        g = jnp.sum(ksum_ref[0, :, sl] * (1.0 / MOBA_BLOCK) * q[:, sl], axis=1, keepdims=True)
        gate = jnp.where(lane == hd, g, gate)
    out_row = lax.broadcasted_iota(jnp.int32, (SUBLANES, LANES), 0)
    out = jnp.zeros((SUBLANES, LANES), jnp.int32)
    for r in range(MOBA_TOPK):
        best = jnp.max(gate, axis=0, keepdims=True)
        first = jnp.min(jnp.where(gate == best, row, float(nb)), axis=0, keepdims=True)
        out = jnp.where(out_row == r, first.astype(jnp.int32), out)
        gate = jnp.where(row == first, -jnp.inf, gate)
    sel_ref[0] = out


def _sample_select(q, ksum):
    n, nb = ksum.shape[0], ksum.shape[1]
    sel = pl.pallas_call(
        _sample_select_body,
        grid=(n,),
        in_specs=[pl.BlockSpec((1, 1, MOBA_W), lambda i: (i, 0, 0)),
                  pl.BlockSpec((1, nb, MOBA_W), lambda i: (i, 0, 0))],
        out_specs=pl.BlockSpec((1, SUBLANES, LANES), lambda i: (i, 0, 0)),
        out_shape=jax.ShapeDtypeStruct((n, SUBLANES, LANES), jnp.int32),
        compiler_params=pltpu.CompilerParams(dimension_semantics=("parallel",)),
        name="moba_sample_select",
    )(q.reshape(n, 1, MOBA_W), ksum)
    return jnp.transpose(sel[:, :MOBA_TOPK, :MOBA_HEADS], (0, 2, 1)).reshape(-1)


def _sample_attend_body(pt_ref, sel_ref, q_ref, kn_ref, vn_ref, ck_hbm, cv_hbm, o_ref, k_buf, v_buf, sem,
                        *, layer, n_pages):
    per_block = MOBA_BLOCK // PAGE_SIZE
    n_sel = MOBA_TOPK * per_block
    seq = pl.program_id(0)
    scale = MOBA_HD ** -0.5

    def page_copies(hd, i):
        blk = sel_ref[(seq * MOBA_HEADS + hd) * MOBA_TOPK + i // per_block]
        page = pt_ref[seq * n_pages + blk * per_block + i % per_block]
        slot = hd * n_sel + i
        return (pltpu.make_async_copy(ck_hbm.at[layer, page, :, hd, :], k_buf.at[slot], sem.at[0, slot]),
                pltpu.make_async_copy(cv_hbm.at[layer, page, :, hd, :], v_buf.at[slot], sem.at[1, slot]))

    for hd in range(MOBA_HEADS):
        for i in range(n_sel):
            for cp in page_copies(hd, i):
                cp.start()
    for hd in range(MOBA_HEADS):
        for i in range(n_sel):
            for cp in page_copies(hd, i):
                cp.wait()

    for hd in range(MOBA_HEADS):
        sl = slice(hd * MOBA_HD, (hd + 1) * MOBA_HD)
        q = q_ref[0, :, sl]
        s_new = jnp.sum(kn_ref[0, :, sl] * q, axis=1, keepdims=True) * scale
        scores = [jnp.sum(k_buf[hd * n_sel + i] * q, axis=1, keepdims=True) * scale for i in range(n_sel)]
        top = s_new
        for s in scores:
            top = jnp.maximum(top, jnp.max(s, axis=0, keepdims=True))
        p_new = jnp.exp(s_new - top)
        denom = p_new
        acc = p_new * vn_ref[0, :, sl]
        for i, s in enumerate(scores):
            p = jnp.exp(s - top)
            denom = denom + jnp.sum(p, axis=0, keepdims=True)
            acc = acc + jnp.sum(p * v_buf[hd * n_sel + i], axis=0, keepdims=True)
        o_ref[0, :, sl] = (acc / denom).astype(BF16)


def _sample_attend(q, k_new, v_new, cache_k, cache_v, layer, page_table_flat, sel_flat, n_pages):
    n = q.shape[0]
    n_slots = MOBA_HEADS * MOBA_TOPK * (MOBA_BLOCK // PAGE_SIZE)
    tok = pl.BlockSpec((1, 1, MOBA_W), lambda s, pt, sel: (s, 0, 0))
    hbm = pl.BlockSpec(memory_space=pl.ANY)
    as_tok = lambda a: a.reshape(n, 1, MOBA_W)
    out = pl.pallas_call(
        functools.partial(_sample_attend_body, layer=layer, n_pages=n_pages),
        grid_spec=pltpu.PrefetchScalarGridSpec(
            num_scalar_prefetch=2,
            grid=(n,),
            in_specs=[tok, tok, tok, hbm, hbm],
            out_specs=tok,
            scratch_shapes=[pltpu.VMEM((n_slots, PAGE_SIZE, MOBA_HD), F32),
                            pltpu.VMEM((n_slots, PAGE_SIZE, MOBA_HD), F32),
                            pltpu.SemaphoreType.DMA((2, n_slots))],
        ),
        out_shape=jax.ShapeDtypeStruct((n, 1, MOBA_W), BF16),
        compiler_params=pltpu.CompilerParams(dimension_semantics=("arbitrary",),
                                             vmem_limit_bytes=_vmem_limit(3 * n_slots * PAGE_SIZE * MOBA_HD * 4)),
        name="moba_sample_attention",
    )(page_table_flat, sel_flat, as_tok(q), as_tok(k_new), as_tok(v_new), cache_k, cache_v)
    return out.reshape(n, MOBA_W)


def _rope_tables(pos):
    half = MOBA_HD // 2
    inv = ROPE_THETA ** (-jnp.arange(half, dtype=F32) / half)
    ang = pos.astype(F32)[:, None] * inv[None, :]
    cos, sin = jnp.cos(ang), jnp.sin(ang)
    return jnp.concatenate([cos, cos], axis=1), jnp.concatenate([-sin, sin], axis=1)


def _layer_weights(w_in, w_a2, b_a, q_norm_g, k_norm_g, w_pa, w_pb, w_o):
    offs = np.concatenate([[0], np.cumsum(IN_SPLITS)])
    gq, gk, gv, gg, glr, mq, mk, mv, ga, gb = (w_in[:, offs[i]:offs[i + 1]] for i in range(len(IN_SPLITS)))
    glr = jnp.pad(glr, ((0, 0), (0, LANES - GLA_RANK)))
    a2 = jnp.pad(w_a2, ((0, LANES - GLA_RANK), (0, 0)))
    b16 = lambda a: a.astype(BF16)
    return dict(
        gq=b16(gq), gk=b16(gk), gv=b16(gv), gg=b16(gg), glr=b16(glr), a2=b16(a2), ba=b_a.reshape(1, GLA_DK),
        gk_t=b16(gk.T), glr_t=b16(glr.T), a2_t=b16(a2.T), ba_t=b_a.reshape(GLA_DK, 1),
        mq=b16(mq), mk=b16(mk), mv=b16(mv),
        qg=jnp.tile(q_norm_g, MOBA_HEADS).reshape(1, MOBA_W), kg=jnp.tile(k_norm_g, MOBA_HEADS).reshape(1, MOBA_W),
        ga=b16(ga), gb=b16(gb), pa=b16(w_pa), pb=b16(w_pb), o=b16(w_o))


def kernel(x_prompt, x_sample, cache_k, cache_v, state_gla, page_table, ffn1_g, ffn1_wg, ffn1_wu, ffn1_wd, mix_g, w_in, w_a2, b_a, gla_norm_g, q_norm_g, k_norm_g, w_pa, w_pb, w_o, ffn2_g, ffn2_wg, ffn2_wu, ffn2_wd):
    batch, seq, _ = x_prompt.shape
    n_dec, dec_seq, _ = x_sample.shape
    n_pages = page_table.shape[1]
    depth = w_in.shape[0]
    past_len = n_pages * PAGE_SIZE
    assert dec_seq == 1 and seq % MOBA_BLOCK == 0 and past_len % MOBA_BLOCK == 0
    assert past_len // MOBA_BLOCK >= MOBA_TOPK
    assert (seq // MOBA_BLOCK) % MOBA_GROUP == 0 and MOBA_GROUP % MOBA_KV_BLOCKS == 0

    yp = x_prompt.reshape(batch * seq, D_MODEL)
    ys = x_sample.reshape(n_dec, D_MODEL)
    rope_p = _rope_tables(jnp.arange(seq, dtype=jnp.int32))
    rope_s = _rope_tables(jnp.full((n_dec,), past_len, jnp.int32))
    pt_flat = page_table.reshape(-1)
    row = lambda a: a.reshape(1, -1)
    outs = [[] for _ in range(6)]
    for l in range(depth):
        w = _layer_weights(w_in[l], w_a2[l], b_a[l], q_norm_g[l], k_norm_g[l], w_pa[l], w_pb[l], w_o[l])
        ffn1 = (row(ffn1_g[l]), ffn1_wg[l].astype(BF16), ffn1_wu[l].astype(BF16), ffn1_wd[l].astype(BF16))
        ffn2 = (row(ffn2_g[l]), ffn2_wg[l].astype(BF16), ffn2_wu[l].astype(BF16), ffn2_wd[l].astype(BF16))
        norm_g = row(gla_norm_g[l])

        x1 = _ffn(yp, *ffn1)
        gq, gk, gv, gg, la, gk_t, la_t, q, k, v, ksum = _proj(x1, row(mix_g[l]), w, *rope_p)
        o_gla, s_prompt = _gla_prompt(gq, gk, la, gk_t, la_t, gv, gg, norm_g, batch, seq)
        qe = _moba_select(q, ksum.reshape(batch, seq // MOBA_BLOCK, MOBA_W), batch, seq)
        o_moba, page_sums = _moba_attend(qe, k, v, batch, seq, cache_k, l, pt_flat)
        yp = _ffn(_merge(x1, row(mix_g[l]), o_gla, o_moba, w), *ffn2)
        outs[0].append(k.reshape(batch, seq, MOBA_HEADS, MOBA_HD))
        outs[1].append(v.reshape(batch, seq, MOBA_HEADS, MOBA_HD))
        outs[2].append(s_prompt)

        x1 = _ffn(ys, *ffn1)
        gq, gk, gv, gg, la, _, _, q, k, v, _ = _proj(x1, row(mix_g[l]), w, *rope_s)
        o_gla, s_sample = _gla_sample(gq, gk, la, gv, gg, state_gla[l], norm_g)
        sel = _sample_select(q, page_sums.reshape(n_dec, n_pages * PAGE_SIZE // MOBA_BLOCK, MOBA_W))
        o_moba = _sample_attend(q, k, v, cache_k, cache_v, l, pt_flat, sel, n_pages)
        ys = _ffn(_merge(x1, row(mix_g[l]), o_gla, o_moba, w), *ffn2)
        outs[3].append(k.reshape(n_dec, 1, MOBA_HEADS, MOBA_HD))
        outs[4].append(v.reshape(n_dec, 1, MOBA_HEADS, MOBA_HD))
        outs[5].append(s_sample)

    return (yp.reshape(batch, seq, D_MODEL), ys.reshape(n_dec, 1, D_MODEL),
            jnp.stack(outs[0]), jnp.stack(outs[1]), jnp.stack(outs[2]),
            jnp.stack(outs[3]), jnp.stack(outs[4]), jnp.stack(outs[5]))
```

```python
import functools

import numpy as np
import jax
import jax.numpy as jnp
from jax import lax
from jax.experimental import pallas as pl
from jax.experimental.pallas import tpu as pltpu

F32 = jnp.float32
BF16 = jnp.bfloat16

D_MODEL = 1024
D_FF = 2816
GLA_HEADS = 4
GLA_HDK = 128
GLA_HDV = 256
GLA_DK = GLA_HEADS * GLA_HDK
GLA_DV = GLA_HEADS * GLA_HDV
GLA_RANK = 16
GLA_TAU = 16.0
MOBA_HEADS = 8
MOBA_HD = 128
MOBA_W = MOBA_HEADS * MOBA_HD
MOBA_BLOCK = 256
MOBA_TOPK = 3
ROPE_THETA = 10000.0
EPS = 1e-6
PAGE_SIZE = 128
IN_SPLITS = (GLA_DK, GLA_DK, GLA_DV, GLA_DV, GLA_RANK, MOBA_W, MOBA_W, MOBA_W, D_MODEL, D_MODEL)

LANES = 128
SUBLANES = 8
VMEM_BYTES = 64 * 1024 * 1024

FF_CHUNK = 256
FFN_ROWS = 512
PROJ_ROWS = 512
GLA_CHUNK = 128
MOBA_GROUP = 4
MOBA_KV_BLOCKS = 2
MASK_BIG = 2.0 ** 100
NEG_INIT = -1.0e38
LOG2_E = 1.4426950408889634


def _vmem_limit(nbytes):
    return int(min(VMEM_BYTES - (4 << 20), max(nbytes, 16 << 20)))


def _resident(shape):
    return pl.BlockSpec(shape, lambda *_: (0,) * len(shape), pipeline_mode=pl.Buffered(1))


def _dot(a, b):
    return jnp.dot(a, b, preferred_element_type=F32)


def _dot_nt(a, b):
    return lax.dot_general(a, b, (((1,), (1,)), ((), ())), preferred_element_type=F32)


def _rms_norm(x, g):
    return x * lax.rsqrt(jnp.mean(x * x, axis=-1, keepdims=True) + EPS) * g


def _log_sigmoid(z):
    return jnp.minimum(z, 0.0) - jnp.log1p(jnp.exp(-jnp.abs(z)))


def _split3(x):
    hi = x.astype(BF16)
    r = x - hi.astype(F32)
    mid = r.astype(BF16)
    lo = (r - mid.astype(F32)).astype(BF16)
    return hi, mid, lo


def _ffn_body(x_ref, g_ref, wg_ref, wu_ref, wd_ref, o_ref):
    x = x_ref[...]
    h = _rms_norm(x, g_ref[...]).astype(BF16)
    acc = jnp.zeros_like(x)
    for c in range(D_FF // FF_CHUNK):
        sl = slice(c * FF_CHUNK, (c + 1) * FF_CHUNK)
        a = _dot(h, wg_ref[:, sl])
        u = _dot(h, wu_ref[:, sl])
        act = (a * jax.nn.sigmoid(a) * u).astype(BF16)
        acc = acc + _dot(act, wd_ref[sl, :])
    o_ref[...] = x + 0.5 * acc


def _ffn(x, g, wg, wu, wd):
    m = x.shape[0]
    tm = min(FFN_ROWS, m)
    row = pl.BlockSpec((tm, D_MODEL), lambda i: (i, 0))
    need = 3 * D_MODEL * D_FF * 2 + 4 * tm * D_MODEL * 4 + 6 * tm * D_MODEL * 4
    return pl.pallas_call(
        _ffn_body,
        grid=(m // tm,),
        in_specs=[row, _resident((1, D_MODEL)), _resident((D_MODEL, D_FF)), _resident((D_MODEL, D_FF)),
                  _resident((D_FF, D_MODEL))],
        out_specs=row,
        out_shape=jax.ShapeDtypeStruct((m, D_MODEL), F32),
        compiler_params=pltpu.CompilerParams(dimension_semantics=("parallel",),
                                             vmem_limit_bytes=_vmem_limit(need)),
        name="swiglu_half_step",
    )(x, g, wg, wu, wd)


def _gla_proj_body(x_ref, g_ref, wgq, wgk, wgv, wgg, wglr, wa2, ba, wgk_t, wglr_t, wa2_t, ba_t,
                   gq_o, gk_o, gv_o, gg_o, la_o, gkt_o, lat_o):
    h = _rms_norm(x_ref[...], g_ref[...]).astype(BF16)
    gq_o[...] = _dot(h, wgq[...]) * (GLA_HDK ** -0.5)
    gk_o[...] = _dot(h, wgk[...])
    gv_o[...] = _dot(h, wgv[...]).astype(BF16)
    gg_o[...] = _dot(h, wgg[...]).astype(BF16)
    glr = _dot(h, wglr[...]).astype(BF16)
    la_o[...] = _log_sigmoid(_dot(glr, wa2[...]) + ba[...]) * (1.0 / GLA_TAU)
    gkt_o[...] = _dot_nt(wgk_t[...], h)
    glr_t = _dot_nt(wglr_t[...], h).astype(BF16)
    lat_o[...] = _log_sigmoid(_dot(wa2_t[...], glr_t) + ba_t[...]) * (1.0 / GLA_TAU)


def _gla_proj(x, mix_g, w):
    m = x.shape[0]
    tm = min(PROJ_ROWS, m)

    def rows(width):
        return pl.BlockSpec((tm, width), lambda i: (i, 0))

    cols = pl.BlockSpec((GLA_DK, tm), lambda i: (0, i))
    weights = [w["gq"], w["gk"], w["gv"], w["gg"], w["glr"], w["a2"], w["ba"], w["gk_t"], w["glr_t"],
               w["a2_t"], w["ba_t"]]
    out_shape = [
        jax.ShapeDtypeStruct((m, GLA_DK), F32), jax.ShapeDtypeStruct((m, GLA_DK), F32),
        jax.ShapeDtypeStruct((m, GLA_DV), BF16), jax.ShapeDtypeStruct((m, GLA_DV), BF16),
        jax.ShapeDtypeStruct((m, GLA_DK), F32),
        jax.ShapeDtypeStruct((GLA_DK, m), F32), jax.ShapeDtypeStruct((GLA_DK, m), F32),
    ]
    w_bytes = sum(int(a.size) * a.dtype.itemsize for a in weights)
    need = w_bytes + 2 * tm * (5 * GLA_DK * 4 + 2 * GLA_DV * 2) + 2 * tm * D_MODEL * 4 + 6 * tm * GLA_DV * 4
    return pl.pallas_call(
        _gla_proj_body,
        grid=(m // tm,),
        in_specs=[rows(D_MODEL), _resident((1, D_MODEL))] + [_resident(a.shape) for a in weights],
        out_specs=[rows(GLA_DK), rows(GLA_DK), rows(GLA_DV), rows(GLA_DV), rows(GLA_DK), cols, cols],
        out_shape=out_shape,
        compiler_params=pltpu.CompilerParams(dimension_semantics=("parallel",),
                                             vmem_limit_bytes=_vmem_limit(need)),
        name="gla_input_projection",
    )(x, mix_g, *weights)


def _moba_proj_body(x_ref, g_ref, wmq, wmk, wmv, qg, kg, rc, rs, q_o, k_o, v_o, qe_o, ksum_ref, *,
                    select, tiles_per_seq):
    h = _rms_norm(x_ref[...], g_ref[...]).astype(BF16)
    mq = _dot(h, wmq[...])
    mk = _dot(h, wmk[...])
    v_o[...] = _dot(h, wmv[...])
    cos = rc[...]
    sin = rs[...]
    for hd in range(MOBA_HEADS):
        sl = slice(hd * MOBA_HD, (hd + 1) * MOBA_HD)
        qh = _rms_norm(mq[:, sl], qg[:, sl])
        q_o[:, sl] = qh * cos + pltpu.roll(qh, MOBA_HD // 2, 1) * sin
        kh = _rms_norm(mk[:, sl], kg[:, sl])
        k_o[:, sl] = kh * cos + pltpu.roll(kh, MOBA_HD // 2, 1) * sin
    if not select:
        qe_o[...] = jnp.zeros_like(qe_o)
        return

    tq = MOBA_BLOCK
    nb = ksum_ref.shape[0]
    tile = pl.program_id(0) % tiles_per_seq

    @pl.when(tile == 0)
    def _():
        ksum_ref[...] = jnp.zeros_like(ksum_ref)

    eye = (lax.broadcasted_iota(jnp.int32, (tq, tq), 0) == lax.broadcasted_iota(jnp.int32, (tq, tq), 1))
    eye = jnp.where(eye, 1.0, 0.0).astype(BF16)
    for c in range(q_o.shape[0] // tq):
        rows = slice(c * tq, (c + 1) * tq)
        own_i = tile * (q_o.shape[0] // tq) + c
        own = own_i.astype(F32)
        ksum_ref[pl.ds(own_i, 1), :] = jnp.sum(k_o[rows, :], axis=0, keepdims=True)
        gates = []
        for hd in range(MOBA_HEADS):
            sl = slice(hd * MOBA_HD, (hd + 1) * MOBA_HD)
            q = q_o[rows, sl]
            means = ksum_ref[:, sl] * (1.0 / MOBA_BLOCK)
            q_hi = q.astype(BF16)
            q_lo = (q - q_hi.astype(F32)).astype(BF16)
            m_hi = means.astype(BF16)
            m_lo = (means - m_hi.astype(F32)).astype(BF16)
            gates.append(_dot_nt(m_hi, q_hi) + _dot_nt(m_lo, q_hi) + _dot_nt(m_hi, q_lo))
            qe_o[0, hd, rows, :MOBA_HD] = (q * (MOBA_HD ** -0.5 * LOG2_E)).astype(BF16)
        gate = jnp.concatenate(gates, axis=1)
        blk = lax.broadcasted_iota(jnp.int32, gate.shape, 0).astype(F32)
        gate = jnp.where(blk < own, gate, -jnp.inf)
        mask = jnp.where(blk == own, 0.0, -1.0)
        for _ in range(MOBA_TOPK):
            best = jnp.max(gate, axis=0, keepdims=True)
            first = jnp.min(jnp.where(gate == best, blk, float(nb)), axis=0, keepdims=True)
            first = jnp.where(best > -jnp.inf, first, -1.0)
            pick = blk == first
            mask = jnp.where(pick, 0.0, mask)
            gate = jnp.where(pick, -jnp.inf, gate)
        mask = jnp.concatenate([mask, jnp.full((LANES - nb, mask.shape[1]), -1.0, F32)], axis=0).astype(BF16)
        for hd in range(MOBA_HEADS):
            qe_o[0, hd, rows, MOBA_HD:] = _dot_nt(eye, mask[:, hd * tq:(hd + 1) * tq]).astype(BF16)


def _moba_proj(x, mix_g, w, rope_cos, rope_sin, n_seq, select):
    m = x.shape[0]
    seq = m // n_seq
    tm = min(PROJ_ROWS, seq if select else m)
    n_rope = rope_cos.shape[0] // tm
    per_seq = seq // tm
    nb = max(seq // MOBA_BLOCK, SUBLANES)
    rows = pl.BlockSpec((tm, MOBA_W), lambda i: (i, 0))
    rope_spec = pl.BlockSpec((tm, MOBA_HD), lambda i: (i % n_rope, 0))
    weights = [w["mq"], w["mk"], w["mv"], w["qg"], w["kg"]]
    w_bytes = sum(int(a.size) * a.dtype.itemsize for a in weights)
    out_bytes = tm * MOBA_W * 3 * 4 + tm * MOBA_HEADS * 2 * MOBA_HD * 2
    need = w_bytes + 2 * out_bytes + 2 * tm * D_MODEL * 4 + 8 * tm * MOBA_W * 4 + (4 << 20)
    return pl.pallas_call(
        functools.partial(_moba_proj_body, select=select, tiles_per_seq=per_seq),
        grid=(m // tm,),
        in_specs=[pl.BlockSpec((tm, D_MODEL), lambda i: (i, 0)), _resident((1, D_MODEL))]
                 + [_resident(a.shape) for a in weights] + [rope_spec, rope_spec],
        out_specs=[rows, rows, rows,
                   pl.BlockSpec((1, MOBA_HEADS, tm, 2 * MOBA_HD), lambda i: (i // per_seq, 0, i % per_seq, 0))],
        out_shape=[jax.ShapeDtypeStruct((m, MOBA_W), F32)] * 3
                  + [jax.ShapeDtypeStruct((n_seq if select else 1, MOBA_HEADS, seq if select else m, 2 * MOBA_HD), BF16)],
        scratch_shapes=[pltpu.VMEM((nb, MOBA_W), F32)],
        compiler_params=pltpu.CompilerParams(dimension_semantics=("arbitrary",),
                                             vmem_limit_bytes=_vmem_limit(need)),
        name="moba_input_projection",
    )(x, mix_g, *weights, rope_cos, rope_sin)


def _gla_tables():
    c = GLA_CHUNK
    t = np.arange(c)
    le = t[None, :] <= t[:, None]
    gt = t[None, :] > t[:, None]
    masks = []
    s = c // 2
    while s >= 1:
        same = (t // (2 * s))[:, None] == (t // (2 * s))[None, :]
        right = (t % (2 * s)) >= s
        masks.append(same & right[:, None] & ~right[None, :])
        s //= 2
    masks.append(np.eye(c, dtype=bool))
    tail = np.concatenate([gt.T, np.ones((c, c), dtype=bool)], axis=1)
    return le.astype(np.float32), np.stack(masks).astype(np.float32), tail.astype(np.float32)


def _level_reference(b, s):
    c, dk = b.shape
    if s >= SUBLANES:
        blocks = b.reshape(c // (2 * s), 2 * s, dk)
        return jnp.broadcast_to(blocks[:, s - 1:s, :], blocks.shape).reshape(c, dk)
    rows8 = b.reshape(c // SUBLANES, SUBLANES, dk)
    sub = lax.broadcasted_iota(jnp.int32, rows8.shape, 1)
    ref = jnp.broadcast_to(rows8[:, SUBLANES - s - 1:SUBLANES - s, :], rows8.shape)
    for first in range(SUBLANES - 4 * s, -1, -2 * s):
        ref = jnp.where(sub < first + 2 * s, jnp.broadcast_to(rows8[:, first + s - 1:first + s, :], rows8.shape), ref)
    return ref.reshape(c, dk)


def _gla_body(q_ref, k_ref, la_ref, kt_ref, lat_ref, v_ref, gg_ref, ng_ref, le_ref, masks_ref, tail_ref,
              o_ref, s_ref):
    c = GLA_CHUNK
    n_lvl = masks_ref.shape[0] - 1

    @pl.when(pl.program_id(1) == 0)
    def _():
        s_ref[...] = jnp.zeros_like(s_ref)

    le = le_ref[...]
    tail = tail_ref[...]
    for hd in range(GLA_HEADS):
        ks = slice(hd * GLA_HDK, (hd + 1) * GLA_HDK)
        vs = slice(hd * GLA_HDV, (hd + 1) * GLA_HDV)
        q = q_ref[:, ks]
        k = k_ref[:, ks]
        v = v_ref[:, vs]
        state = s_ref[0, hd]
        b = sum(_dot(le, p) for p in _split3(la_ref[:, ks]))
        et = jnp.exp(sum(_dot(p, tail) for p in _split3(lat_ref[ks, :])))
        out = _dot((q * jnp.exp(b)).astype(BF16), state.astype(BF16))
        attn = masks_ref[n_lvl] * _dot_nt(q.astype(BF16), k.astype(BF16))
        for lv in range(n_lvl):
            e = jnp.exp(-jnp.abs(b - _level_reference(b, c >> (lv + 1))))
            attn = attn + masks_ref[lv] * _dot_nt((q * e).astype(BF16), (k * e).astype(BF16))
        out = out + _dot(attn.astype(BF16), v)
        k_dec = (kt_ref[ks, :] * et[:, :c]).astype(BF16)
        decay = et[:, c:]
        s_ref[0, hd] = jnp.concatenate([state[:, :c] * decay, state[:, c:] * decay], axis=1) + _dot(k_dec, v)
        gate = gg_ref[:, vs].astype(F32)
        o_ref[:, vs] = (_rms_norm(out, ng_ref[...]) * (gate * jax.nn.sigmoid(gate))).astype(BF16)


def _gla_prompt(gq, gk, la, gk_t, la_t, gv, gg, norm_g, batch, seq):
    c = GLA_CHUNK
    n_chunks = seq // c
    le, masks, tail = (jnp.asarray(a, BF16 if i != 1 else F32) for i, a in enumerate(_gla_tables()))

    def rows(width):
        return pl.BlockSpec((c, width), lambda b, t: (b * n_chunks + t, 0))

    cols = pl.BlockSpec((GLA_DK, c), lambda b, t: (0, b * n_chunks + t))
    need = 2 * (5 * c * GLA_DK * 4 + 3 * c * GLA_DV * 2) + 4 * GLA_DK * GLA_HDV * 4 + (8 << 20)
    return pl.pallas_call(
        _gla_body,
        grid=(batch, n_chunks),
        in_specs=[rows(GLA_DK), rows(GLA_DK), rows(GLA_DK), cols, cols, rows(GLA_DV), rows(GLA_DV),
                  _resident((1, GLA_HDV)), _resident(le.shape), _resident(masks.shape), _resident(tail.shape)],
        out_specs=[rows(GLA_DV),
                   pl.BlockSpec((1, GLA_HEADS, GLA_HDK, GLA_HDV), lambda b, t: (b, 0, 0, 0))],
        out_shape=[jax.ShapeDtypeStruct((batch * seq, GLA_DV), BF16),
                   jax.ShapeDtypeStruct((batch, GLA_HEADS, GLA_HDK, GLA_HDV), F32)],
        compiler_params=pltpu.CompilerParams(dimension_semantics=("parallel", "arbitrary"),
                                             vmem_limit_bytes=_vmem_limit(need)),
        name="gla_prompt_chunks",
    )(gq, gk, la, gk_t, la_t, gv, gg, norm_g, le, masks, tail)


def _moba_attend_body(pt_ref, qe_ref, k_ref, v_ref, *refs):
    pages, (o_ref, ksum_ref, acc_ref, m_ref) = refs[:-4], refs[-4:]
    j = pl.program_id(2)
    tb = MOBA_BLOCK
    tkv = k_ref.shape[0]

    per_block = MOBA_BLOCK // PAGE_SIZE
    for blk in range(len(pages) // per_block):
        tot = jnp.sum(pages[blk * per_block][...], axis=0)
        for p in range(1, per_block):
            tot = tot + jnp.sum(pages[blk * per_block + p][...], axis=0)
        ksum_ref[0, blk] = tot
    n_groups = qe_ref.shape[2] // (tb * MOBA_GROUP)

    @pl.when(j == 0)
    def _():
        acc_ref[...] = jnp.zeros_like(acc_ref)
        m_ref[...] = jnp.full_like(m_ref, NEG_INIT)

    k = k_ref[...].astype(BF16)
    lane = lax.broadcasted_iota(jnp.int32, (tkv, MOBA_HD), 1)
    key_block = j * (tkv // tb) + lax.broadcasted_iota(jnp.int32, (tkv, MOBA_HD), 0) // tb
    k_ext = jnp.concatenate([k, jnp.where(lane == key_block, MASK_BIG, 0.0).astype(BF16)], axis=1)
    v_ext = jnp.concatenate([v_ref[...].astype(BF16), jnp.ones((tkv, MOBA_HD), BF16)], axis=1)

    def update(rows, s):
        m_old = m_ref[rows, :]
        m_new = jnp.maximum(m_old, jnp.max(s, axis=1, keepdims=True))
        p = jnp.exp2(s - jnp.concatenate([m_new] * (tkv // MOBA_HD), axis=1)).astype(BF16)
        alpha = jnp.exp2(m_old - m_new)
        acc_new = jnp.concatenate([alpha, alpha], axis=1) * acc_ref[rows, :] + _dot(p, v_ext)
        m_ref[rows, :] = m_new
        acc_ref[rows, :] = acc_new
        return acc_new

    rows_per_group = tb * MOBA_GROUP

    def group_rows(g):
        return pl.ds(pl.multiple_of(g * rows_per_group, rows_per_group), rows_per_group)

    def scores(g):
        return _dot_nt(qe_ref[0, 0, group_rows(g), :], k_ext)

    first = (j * tkv) // rows_per_group
    q_pos = first * rows_per_group + lax.broadcasted_iota(jnp.int32, (rows_per_group, tkv), 0)
    k_pos = j * tkv + lax.broadcasted_iota(jnp.int32, (rows_per_group, tkv), 1)
    update(group_rows(first), jnp.where(k_pos <= q_pos, scores(first), -MASK_BIG))
    acc = acc_ref[pl.ds(pl.multiple_of(j * tkv, tkv), tkv), :]
    o_ref[...] = (acc[:, :MOBA_HD] / acc[:, MOBA_HD:]).astype(BF16)

    def body(g, s):
        s_next = scores(jnp.minimum(g + 1, n_groups - 1))
        update(group_rows(g), s)
        return s_next

    lax.fori_loop(first + 1, n_groups, body, scores(jnp.minimum(first + 1, n_groups - 1)))


def _moba_attend(qe, k, v, batch, seq, cache, layer, page_table_flat):
    tkv = MOBA_BLOCK * MOBA_KV_BLOCKS
    nb = seq // tkv
    n_steps = batch * MOBA_HEADS * nb
    per_block = MOBA_BLOCK // PAGE_SIZE
    pages_per_step = page_table_flat.shape[0] // n_steps
    assert pages_per_step * n_steps == page_table_flat.shape[0] and pages_per_step % per_block == 0
    kv = pl.BlockSpec((tkv, MOBA_HD), lambda b, h, j, pt: (b * nb + j, h))

    def step(b, h, j):
        return (b * MOBA_HEADS + h) * nb + j

    def page_spec(i):
        return pl.BlockSpec((None, None, PAGE_SIZE, MOBA_HEADS, MOBA_HD),
                            lambda b, h, j, pt: (layer, pt[step(b, h, j) * pages_per_step + i], 0, 0, 0))

    blocks_per_step = pages_per_step // per_block
    need = (2 * seq * 2 * MOBA_HD * 2 + 3 * seq * MOBA_HD * 4 + 2 * pages_per_step * PAGE_SIZE * MOBA_W * 4
            + (16 << 20))
    out, sums = pl.pallas_call(
        _moba_attend_body,
        grid_spec=pltpu.PrefetchScalarGridSpec(
            num_scalar_prefetch=1,
            grid=(batch, MOBA_HEADS, nb),
            in_specs=[pl.BlockSpec((1, 1, seq, 2 * MOBA_HD), lambda b, h, j, pt: (b, h, 0, 0)), kv, kv]
                     + [page_spec(i) for i in range(pages_per_step)],
            out_specs=[kv, pl.BlockSpec((1, blocks_per_step, MOBA_HEADS, MOBA_HD),
                                        lambda b, h, j, pt: (step(b, h, j), 0, 0, 0))],
            scratch_shapes=[pltpu.VMEM((seq, 2 * MOBA_HD), F32), pltpu.VMEM((seq, MOBA_HD), F32)],
        ),
        out_shape=[jax.ShapeDtypeStruct((batch * seq, MOBA_W), BF16),
                   jax.ShapeDtypeStruct((n_steps, blocks_per_step, MOBA_HEADS, MOBA_HD), F32)],
        compiler_params=pltpu.CompilerParams(dimension_semantics=("parallel", "parallel", "arbitrary"),
                                             vmem_limit_bytes=_vmem_limit(need)),
        name="moba_prompt_attention",
    )(page_table_flat, qe, k, v, *([cache] * pages_per_step))
    return out, sums


def _merge_body(x_ref, g_ref, oa_ref, ob_ref, wga, wgb, wpa, wpb, wo, o_ref):
    x = x_ref[...]
    h = _rms_norm(x, g_ref[...]).astype(BF16)
    mix = jax.nn.sigmoid(_dot(h, wga[...])) * _dot(oa_ref[...], wpa[...])
    mix = mix + jax.nn.sigmoid(_dot(h, wgb[...])) * _dot(ob_ref[...], wpb[...])
    o_ref[...] = x + _dot(mix.astype(BF16), wo[...])


def _merge(x, mix_g, o_gla, o_moba, w):
    m = x.shape[0]
    tm = min(FFN_ROWS, m)
    row32 = pl.BlockSpec((tm, D_MODEL), lambda i: (i, 0))
    sq = _resident((D_MODEL, D_MODEL))
    need = 5 * D_MODEL * D_MODEL * 2 + 4 * tm * D_MODEL * 4 + 4 * tm * D_MODEL * 2 + 6 * tm * D_MODEL * 4
    return pl.pallas_call(
        _merge_body,
        grid=(m // tm,),
        in_specs=[row32, _resident((1, D_MODEL)), row32, row32, sq, sq, sq, sq, sq],
        out_specs=row32,
        out_shape=jax.ShapeDtypeStruct((m, D_MODEL), F32),
        compiler_params=pltpu.CompilerParams(dimension_semantics=("parallel",),
                                             vmem_limit_bytes=_vmem_limit(need)),
        name="gated_merge_projection",
    )(x, mix_g, o_gla, o_moba, w["ga"], w["gb"], w["pa"], w["pb"], w["o"])


def _gla_sample_body(q_ref, k_ref, la_ref, v_ref, gg_ref, s_ref, ng_ref, o_ref, so_ref):
    for hd in range(GLA_HEADS):
        v = v_ref[0, hd].astype(F32)
        s_new = jnp.exp(la_ref[0, hd]) * s_ref[0, hd] + k_ref[0, hd] * v
        so_ref[0, hd] = s_new
        out = jnp.sum(q_ref[0, hd] * s_new, axis=0, keepdims=True)
        gate = gg_ref[0, hd].astype(F32)
        o_ref[0, hd] = (_rms_norm(out, ng_ref[...]) * (gate * jax.nn.sigmoid(gate))).astype(BF16)


def _gla_sample(gq, gk, la, gv, gg, state, norm_g):
    n = gq.shape[0]
    col = pl.BlockSpec((1, GLA_HEADS, GLA_HDK, 1), lambda i: (i, 0, 0, 0))
    row = pl.BlockSpec((1, GLA_HEADS, 1, GLA_HDV), lambda i: (i, 0, 0, 0))
    st = pl.BlockSpec((1, GLA_HEADS, GLA_HDK, GLA_HDV), lambda i: (i, 0, 0, 0))
    as_col = lambda a: a.reshape(n, GLA_HEADS, GLA_HDK, 1)
    as_row = lambda a: a.reshape(n, GLA_HEADS, 1, GLA_HDV)
    o, s_new = pl.pallas_call(
        _gla_sample_body,
        grid=(n,),
        in_specs=[col, col, col, row, row, st, _resident((1, GLA_HDV))],
        out_specs=[row, st],
        out_shape=[jax.ShapeDtypeStruct((n, GLA_HEADS, 1, GLA_HDV), BF16),
                   jax.ShapeDtypeStruct((n, GLA_HEADS, GLA_HDK, GLA_HDV), F32)],
        compiler_params=pltpu.CompilerParams(dimension_semantics=("parallel",),
                                             vmem_limit_bytes=_vmem_limit(16 << 20)),
        name="gla_sample_step",
    )(as_col(gq), as_col(gk), as_col(la), as_row(gv), as_row(gg), state, norm_g)
    return o.reshape(n, GLA_DV), s_new


def _sample_select_body(q_ref, ksum_ref, sel_ref):
    nb = ksum_ref.shape[1]
    q = q_ref[0]
    lane = lax.broadcasted_iota(jnp.int32, (nb, LANES), 1)
    row = lax.broadcasted_iota(jnp.int32, (nb, LANES), 0).astype(F32)
    gate = jnp.full((nb, LANES), -jnp.inf, F32)
    for hd in range(MOBA_HEADS):
        sl = slice(hd * MOBA_HD, (hd + 1) * MOBA_HD)
        g = jnp.sum(ksum_ref[0, :, sl] * (1.0 / MOBA_BLOCK) * q[:, sl], axis=1, keepdims=True)
        gate = jnp.where(lane == hd, g, gate)
    out_row = lax.broadcasted_iota(jnp.int32, (SUBLANES, LANES), 0)
    out = jnp.zeros((SUBLANES, LANES), jnp.int32)
    for r in range(MOBA_TOPK):
        best = jnp.max(gate, axis=0, keepdims=True)
        first = jnp.min(jnp.where(gate == best, row, float(nb)), axis=0, keepdims=True)
        out = jnp.where(out_row == r, first.astype(jnp.int32), out)
        gate = jnp.where(row == first, -jnp.inf, gate)
    sel_ref[0] = out


def _sample_select(q, ksum):
    n, nb = ksum.shape[0], ksum.shape[1]
    sel = pl.pallas_call(
        _sample_select_body,
        grid=(n,),
        in_specs=[pl.BlockSpec((1, 1, MOBA_W), lambda i: (i, 0, 0)),
                  pl.BlockSpec((1, nb, MOBA_W), lambda i: (i, 0, 0))],
        out_specs=pl.BlockSpec((1, SUBLANES, LANES), lambda i: (i, 0, 0)),
        out_shape=jax.ShapeDtypeStruct((n, SUBLANES, LANES), jnp.int32),
        compiler_params=pltpu.CompilerParams(dimension_semantics=("parallel",)),
        name="moba_sample_select",
    )(q.reshape(n, 1, MOBA_W), ksum)
    return jnp.transpose(sel[:, :MOBA_TOPK, :MOBA_HEADS], (0, 2, 1)).reshape(-1)


def _sample_attend_body(pt_ref, sel_ref, q_ref, kn_ref, vn_ref, ck_hbm, cv_hbm, o_ref, k_buf, v_buf, sem,
                        *, layer, n_pages):
    per_block = MOBA_BLOCK // PAGE_SIZE
    n_sel = MOBA_TOPK * per_block
    n_slots = MOBA_HEADS * n_sel
    seq = pl.program_id(0)
    scale = MOBA_HD ** -0.5

    def page_copies(s, hd, i):
        blk = sel_ref[(s * MOBA_HEADS + hd) * MOBA_TOPK + i // per_block]
        page = pt_ref[s * n_pages + blk * per_block + i % per_block]
        slot = (s % 2) * n_slots + hd * n_sel + i
        return (pltpu.make_async_copy(ck_hbm.at[layer, page, :, hd, :], k_buf.at[slot], sem.at[0, slot]),
                pltpu.make_async_copy(cv_hbm.at[layer, page, :, hd, :], v_buf.at[slot], sem.at[1, slot]))

    def start_fetch(s):
        for hd in range(MOBA_HEADS):
            for i in range(n_sel):
                for cp in page_copies(s, hd, i):
                    cp.start()

    @pl.when(seq == 0)
    def _():
        start_fetch(seq)

    @pl.when(seq + 1 < pl.num_programs(0))
    def _():
        start_fetch(seq + 1)

    for hd in range(MOBA_HEADS):
        for i in range(n_sel):
            for cp in page_copies(seq, hd, i):
                cp.wait()

    base = (seq % 2) * n_slots
    for hd in range(MOBA_HEADS):
        sl = slice(hd * MOBA_HD, (hd + 1) * MOBA_HD)
        q = q_ref[0, :, sl]
        s_new = jnp.sum(kn_ref[0, :, sl] * q, axis=1, keepdims=True) * scale
        scores = [jnp.sum(k_buf[base + hd * n_sel + i] * q, axis=1, keepdims=True) * scale for i in range(n_sel)]
        top = s_new
        for s in scores:
            top = jnp.maximum(top, jnp.max(s, axis=0, keepdims=True))
        p_new = jnp.exp(s_new - top)
        denom = p_new
        acc = p_new * vn_ref[0, :, sl]
        for i, s in enumerate(scores):
            p = jnp.exp(s - top)
            denom = denom + jnp.sum(p, axis=0, keepdims=True)
            acc = acc + jnp.sum(p * v_buf[base + hd * n_sel + i], axis=0, keepdims=True)
        o_ref[0, :, sl] = (acc / denom).astype(BF16)


def _sample_attend(q, k_new, v_new, cache_k, cache_v, layer, page_table_flat, sel_flat, n_pages):
    n = q.shape[0]
    n_slots = MOBA_HEADS * MOBA_TOPK * (MOBA_BLOCK // PAGE_SIZE)
    tok = pl.BlockSpec((1, 1, MOBA_W), lambda s, pt, sel: (s, 0, 0))
    hbm = pl.BlockSpec(memory_space=pl.ANY)
    as_tok = lambda a: a.reshape(n, 1, MOBA_W)
    out = pl.pallas_call(
        functools.partial(_sample_attend_body, layer=layer, n_pages=n_pages),
        grid_spec=pltpu.PrefetchScalarGridSpec(
            num_scalar_prefetch=2,
            grid=(n,),
            in_specs=[tok, tok, tok, hbm, hbm],
            out_specs=tok,
            scratch_shapes=[pltpu.VMEM((2 * n_slots, PAGE_SIZE, MOBA_HD), F32),
                            pltpu.VMEM((2 * n_slots, PAGE_SIZE, MOBA_HD), F32),
                            pltpu.SemaphoreType.DMA((2, 2 * n_slots))],
        ),
        out_shape=jax.ShapeDtypeStruct((n, 1, MOBA_W), BF16),
        compiler_params=pltpu.CompilerParams(dimension_semantics=("arbitrary",),
                                             vmem_limit_bytes=_vmem_limit(5 * n_slots * PAGE_SIZE * MOBA_HD * 4)),
        name="moba_sample_attention",
    )(page_table_flat, sel_flat, as_tok(q), as_tok(k_new), as_tok(v_new), cache_k, cache_v)
    return out.reshape(n, MOBA_W)


def _rope_tables(pos):
    half = MOBA_HD // 2
    inv = ROPE_THETA ** (-jnp.arange(half, dtype=F32) / half)
    ang = pos.astype(F32)[:, None] * inv[None, :]
    cos, sin = jnp.cos(ang), jnp.sin(ang)
    return jnp.concatenate([cos, cos], axis=1), jnp.concatenate([-sin, sin], axis=1)


def _layer_weights(w_in, w_a2, b_a, q_norm_g, k_norm_g, w_pa, w_pb, w_o):
    offs = np.concatenate([[0], np.cumsum(IN_SPLITS)])
    gq, gk, gv, gg, glr, mq, mk, mv, ga, gb = (w_in[:, offs[i]:offs[i + 1]] for i in range(len(IN_SPLITS)))
    glr = jnp.pad(glr, ((0, 0), (0, LANES - GLA_RANK)))
    a2 = jnp.pad(w_a2, ((0, LANES - GLA_RANK), (0, 0)))
    b16 = lambda a: a.astype(BF16)
    return dict(
        gq=b16(gq), gk=b16(gk), gv=b16(gv), gg=b16(gg), glr=b16(glr), a2=b16(a2), ba=b_a.reshape(1, GLA_DK),
        gk_t=b16(gk.T), glr_t=b16(glr.T), a2_t=b16(a2.T), ba_t=b_a.reshape(GLA_DK, 1),
        mq=b16(mq), mk=b16(mk), mv=b16(mv),
        qg=jnp.tile(q_norm_g, MOBA_HEADS).reshape(1, MOBA_W), kg=jnp.tile(k_norm_g, MOBA_HEADS).reshape(1, MOBA_W),
        ga=b16(ga), gb=b16(gb), pa=b16(w_pa), pb=b16(w_pb), o=b16(w_o))


def kernel(x_prompt, x_sample, cache_k, cache_v, state_gla, page_table, ffn1_g, ffn1_wg, ffn1_wu, ffn1_wd, mix_g, w_in, w_a2, b_a, gla_norm_g, q_norm_g, k_norm_g, w_pa, w_pb, w_o, ffn2_g, ffn2_wg, ffn2_wu, ffn2_wd):
    batch, seq, _ = x_prompt.shape
    n_dec, dec_seq, _ = x_sample.shape
    n_pages = page_table.shape[1]
    depth = w_in.shape[0]
    past_len = n_pages * PAGE_SIZE
    assert dec_seq == 1 and seq % MOBA_BLOCK == 0 and past_len % MOBA_BLOCK == 0
    assert past_len // MOBA_BLOCK >= MOBA_TOPK
    assert (seq // MOBA_BLOCK) % MOBA_GROUP == 0 and MOBA_GROUP % MOBA_KV_BLOCKS == 0

    yp = x_prompt.reshape(batch * seq, D_MODEL)
    ys = x_sample.reshape(n_dec, D_MODEL)
    rope_p = _rope_tables(jnp.arange(seq, dtype=jnp.int32))
    rope_s = _rope_tables(jnp.full((n_dec,), past_len, jnp.int32))
    pt_flat = page_table.reshape(-1)
    row = lambda a: a.reshape(1, -1)
    outs = [[] for _ in range(6)]
    for l in range(depth):
        w = _layer_weights(w_in[l], w_a2[l], b_a[l], q_norm_g[l], k_norm_g[l], w_pa[l], w_pb[l], w_o[l])
        ffn1 = (row(ffn1_g[l]), ffn1_wg[l].astype(BF16), ffn1_wu[l].astype(BF16), ffn1_wd[l].astype(BF16))
        ffn2 = (row(ffn2_g[l]), ffn2_wg[l].astype(BF16), ffn2_wu[l].astype(BF16), ffn2_wd[l].astype(BF16))
        norm_g = row(gla_norm_g[l])

        x1 = _ffn(yp, *ffn1)
        gq, gk, gv, gg, la, gk_t, la_t = _gla_proj(x1, row(mix_g[l]), w)
        o_gla, s_prompt = _gla_prompt(gq, gk, la, gk_t, la_t, gv, gg, norm_g, batch, seq)
        _, k, v, qe = _moba_proj(x1, row(mix_g[l]), w, *rope_p, n_seq=batch, select=True)
        o_moba, page_sums = _moba_attend(qe, k, v, batch, seq, cache_k, l, pt_flat)
        yp = _ffn(_merge(x1, row(mix_g[l]), o_gla, o_moba, w), *ffn2)
        outs[0].append(k.reshape(batch, seq, MOBA_HEADS, MOBA_HD))
        outs[1].append(v.reshape(batch, seq, MOBA_HEADS, MOBA_HD))
        outs[2].append(s_prompt)

        x1 = _ffn(ys, *ffn1)
        gq, gk, gv, gg, la, _, _ = _gla_proj(x1, row(mix_g[l]), w)
        q, k, v, _ = _moba_proj(x1, row(mix_g[l]), w, *rope_s, n_seq=1, select=False)
        o_gla, s_sample = _gla_sample(gq, gk, la, gv, gg, state_gla[l], norm_g)
        sel = _sample_select(q, page_sums.reshape(n_dec, n_pages * PAGE_SIZE // MOBA_BLOCK, MOBA_W))
        o_moba = _sample_attend(q, k, v, cache_k, cache_v, l, pt_flat, sel, n_pages)
        ys = _ffn(_merge(x1, row(mix_g[l]), o_gla, o_moba, w), *ffn2)
        outs[3].append(k.reshape(n_dec, 1, MOBA_HEADS, MOBA_HD))
        outs[4].append(v.reshape(n_dec, 1, MOBA_HEADS, MOBA_HD))
        outs[5].append(s_sample)

    return (yp.reshape(batch, seq, D_MODEL), ys.reshape(n_dec, 1, D_MODEL),
            jnp.stack(outs[0]), jnp.stack(outs[1]), jnp.stack(outs[2]),
            jnp.stack(outs[3]), jnp.stack(outs[4]), jnp.stack(outs[5]))
```

```python
import functools

import numpy as np
import jax
import jax.numpy as jnp
from jax import lax
from jax.experimental import pallas as pl
from jax.experimental.pallas import tpu as pltpu

F32 = jnp.float32
BF16 = jnp.bfloat16

D_MODEL = 1024
D_FF = 2816
GLA_HEADS = 4
GLA_HDK = 128
GLA_HDV = 256
GLA_DK = GLA_HEADS * GLA_HDK
GLA_DV = GLA_HEADS * GLA_HDV
GLA_RANK = 16
GLA_TAU = 16.0
MOBA_HEADS = 8
MOBA_HD = 128
MOBA_W = MOBA_HEADS * MOBA_HD
MOBA_BLOCK = 256
MOBA_TOPK = 3
ROPE_THETA = 10000.0
EPS = 1e-6
PAGE_SIZE = 128
IN_SPLITS = (GLA_DK, GLA_DK, GLA_DV, GLA_DV, GLA_RANK, MOBA_W, MOBA_W, MOBA_W, D_MODEL, D_MODEL)

LANES = 128
SUBLANES = 8
VMEM_BYTES = 64 * 1024 * 1024

FF_CHUNK = 256
FFN_ROWS = 512
PROJ_ROWS = 512
GLA_CHUNK = 128
MOBA_GROUP = 4
MOBA_KV_BLOCKS = 2
MASK_BIG = 2.0 ** 100
NEG_INIT = -1.0e38
LOG2_E = 1.4426950408889634


def _vmem_limit(nbytes):
    return int(min(VMEM_BYTES - (4 << 20), max(nbytes, 16 << 20)))


def _resident(shape):
    return pl.BlockSpec(shape, lambda *_: (0,) * len(shape), pipeline_mode=pl.Buffered(1))


def _dot(a, b):
    return jnp.dot(a, b, preferred_element_type=F32)


def _dot_nt(a, b):
    return lax.dot_general(a, b, (((1,), (1,)), ((), ())), preferred_element_type=F32)


def _rms_norm(x, g):
    return x * lax.rsqrt(jnp.mean(x * x, axis=-1, keepdims=True) + EPS) * g


def _log_sigmoid(z):
    return jnp.minimum(z, 0.0) - jnp.log1p(jnp.exp(-jnp.abs(z)))


def _split3(x):
    hi = x.astype(BF16)
    r = x - hi.astype(F32)
    mid = r.astype(BF16)
    lo = (r - mid.astype(F32)).astype(BF16)
    return hi, mid, lo


def _ffn_body(x_ref, g_ref, wg_ref, wu_ref, wd_ref, o_ref):
    x = x_ref[...]
    h = _rms_norm(x, g_ref[...]).astype(BF16)
    acc = jnp.zeros_like(x)
    for c in range(D_FF // FF_CHUNK):
        sl = slice(c * FF_CHUNK, (c + 1) * FF_CHUNK)
        a = _dot(h, wg_ref[:, sl])
        u = _dot(h, wu_ref[:, sl])
        act = (a * jax.nn.sigmoid(a) * u).astype(BF16)
        acc = acc + _dot(act, wd_ref[sl, :])
    o_ref[...] = x + 0.5 * acc


def _ffn(x, g, wg, wu, wd):
    m = x.shape[0]
    tm = min(FFN_ROWS, m)
    row = pl.BlockSpec((tm, D_MODEL), lambda i: (i, 0))
    need = 3 * D_MODEL * D_FF * 2 + 4 * tm * D_MODEL * 4 + 6 * tm * D_MODEL * 4
    return pl.pallas_call(
        _ffn_body,
        grid=(m // tm,),
        in_specs=[row, _resident((1, D_MODEL)), _resident((D_MODEL, D_FF)), _resident((D_MODEL, D_FF)),
                  _resident((D_FF, D_MODEL))],
        out_specs=row,
        out_shape=jax.ShapeDtypeStruct((m, D_MODEL), F32),
        compiler_params=pltpu.CompilerParams(dimension_semantics=("parallel",),
                                             vmem_limit_bytes=_vmem_limit(need)),
        name="swiglu_half_step",
    )(x, g, wg, wu, wd)


def _gla_proj_body(x_ref, g_ref, wgq, wgk, wgv, wgg, wglr, wa2, ba, wgk_t, wglr_t, wa2_t, ba_t,
                   gq_o, gk_o, gv_o, gg_o, la_o, gkt_o, lat_o):
    h = _rms_norm(x_ref[...], g_ref[...]).astype(BF16)
    gq_o[...] = _dot(h, wgq[...]) * (GLA_HDK ** -0.5)
    gk_o[...] = _dot(h, wgk[...])
    gv_o[...] = _dot(h, wgv[...]).astype(BF16)
    gg_o[...] = _dot(h, wgg[...]).astype(BF16)
    glr = _dot(h, wglr[...]).astype(BF16)
    la_o[...] = _log_sigmoid(_dot(glr, wa2[...]) + ba[...]) * (1.0 / GLA_TAU)
    gkt_o[...] = _dot_nt(wgk_t[...], h)
    glr_t = _dot_nt(wglr_t[...], h).astype(BF16)
    lat_o[...] = _log_sigmoid(_dot(wa2_t[...], glr_t) + ba_t[...]) * (1.0 / GLA_TAU)


def _gla_proj(x, mix_g, w):
    m = x.shape[0]
    tm = min(PROJ_ROWS, m)

    def rows(width):
        return pl.BlockSpec((tm, width), lambda i: (i, 0))

    cols = pl.BlockSpec((GLA_DK, tm), lambda i: (0, i))
    weights = [w["gq"], w["gk"], w["gv"], w["gg"], w["glr"], w["a2"], w["ba"], w["gk_t"], w["glr_t"],
               w["a2_t"], w["ba_t"]]
    out_shape = [
        jax.ShapeDtypeStruct((m, GLA_DK), F32), jax.ShapeDtypeStruct((m, GLA_DK), F32),
        jax.ShapeDtypeStruct((m, GLA_DV), BF16), jax.ShapeDtypeStruct((m, GLA_DV), BF16),
        jax.ShapeDtypeStruct((m, GLA_DK), F32),
        jax.ShapeDtypeStruct((GLA_DK, m), F32), jax.ShapeDtypeStruct((GLA_DK, m), F32),
    ]
    w_bytes = sum(int(a.size) * a.dtype.itemsize for a in weights)
    need = w_bytes + 2 * tm * (5 * GLA_DK * 4 + 2 * GLA_DV * 2) + 2 * tm * D_MODEL * 4 + 6 * tm * GLA_DV * 4
    return pl.pallas_call(
        _gla_proj_body,
        grid=(m // tm,),
        in_specs=[rows(D_MODEL), _resident((1, D_MODEL))] + [_resident(a.shape) for a in weights],
        out_specs=[rows(GLA_DK), rows(GLA_DK), rows(GLA_DV), rows(GLA_DV), rows(GLA_DK), cols, cols],
        out_shape=out_shape,
        compiler_params=pltpu.CompilerParams(dimension_semantics=("parallel",),
                                             vmem_limit_bytes=_vmem_limit(need)),
        name="gla_input_projection",
    )(x, mix_g, *weights)


def _moba_proj_body(x_ref, g_ref, wmq, wmk, wmv, qg, kg, rc, rs, q_o, k_o, v_o, qe_o, ksum_ref, *,
                    select, tiles_per_seq):
    tile = pl.program_id(0) % tiles_per_seq
    if select:
        @pl.when(tile == 0)
        def _():
            ksum_ref[...] = jnp.zeros_like(ksum_ref)

    h = _rms_norm(x_ref[...], g_ref[...]).astype(BF16)
    mq = _dot(h, wmq[...])
    mk = _dot(h, wmk[...])
    v_o[...] = _dot(h, wmv[...])
    cos = rc[...]
    sin = rs[...]
    for hd in range(MOBA_HEADS):
        sl = slice(hd * MOBA_HD, (hd + 1) * MOBA_HD)
        qh = _rms_norm(mq[:, sl], qg[:, sl])
        q_o[:, sl] = qh * cos + pltpu.roll(qh, MOBA_HD // 2, 1) * sin
        kh = _rms_norm(mk[:, sl], kg[:, sl])
        k_o[:, sl] = kh * cos + pltpu.roll(kh, MOBA_HD // 2, 1) * sin
    if not select:
        qe_o[...] = jnp.zeros_like(qe_o)
        return

    tq = MOBA_BLOCK
    nb = ksum_ref.shape[0]
    eye =(lax.broadcasted_iota(jnp.int32, (tq, tq), 0) == lax.broadcasted_iota(jnp.int32, (tq, tq), 1))
    eye = jnp.where(eye, 1.0, 0.0).astype(BF16)
    for c in range(q_o.shape[0] // tq):
        rows = slice(c * tq, (c + 1) * tq)
        own_i = tile * (q_o.shape[0] // tq) + c
        own = own_i.astype(F32)
        ksum_ref[pl.ds(own_i, 1), :] = jnp.sum(k_o[rows, :], axis=0, keepdims=True)
        gates = []
        for hd in range(MOBA_HEADS):
            sl = slice(hd * MOBA_HD, (hd + 1) * MOBA_HD)
            q = q_o[rows, sl]
            means = ksum_ref[:, sl] * (1.0 / MOBA_BLOCK)
            q_hi = q.astype(BF16)
            q_lo = (q - q_hi.astype(F32)).astype(BF16)
            m_hi = means.astype(BF16)
            m_lo = (means - m_hi.astype(F32)).astype(BF16)
            gates.append(_dot_nt(m_hi, q_hi) + _dot_nt(m_lo, q_hi) + _dot_nt(m_hi, q_lo))
            qe_o[0, hd, rows, :MOBA_HD] = (q * (MOBA_HD ** -0.5 * LOG2_E)).astype(BF16)
        gate = jnp.concatenate(gates, axis=1)
        blk = lax.broadcasted_iota(jnp.int32, gate.shape, 0).astype(F32)
        gate = jnp.where(blk < own, gate, -jnp.inf)
        mask = jnp.where(blk == own, 0.0, -1.0)
        for _ in range(MOBA_TOPK):
            best = jnp.max(gate, axis=0, keepdims=True)
            first = jnp.min(jnp.where(gate == best, blk, float(nb)), axis=0, keepdims=True)
            first = jnp.where(best > -jnp.inf, first, -1.0)
            pick = blk == first
            mask = jnp.where(pick, 0.0, mask)
            gate = jnp.where(pick, -jnp.inf, gate)
        mask = jnp.concatenate([mask, jnp.full((LANES - nb, mask.shape[1]), -1.0, F32)], axis=0).astype(BF16)
        for hd in range(MOBA_HEADS):
            qe_o[0, hd, rows, MOBA_HD:] = _dot_nt(eye, mask[:, hd * tq:(hd + 1) * tq]).astype(BF16)


def _moba_proj(x, mix_g, w, rope_cos, rope_sin, n_seq, select):
    m = x.shape[0]
    seq = m // n_seq
    tm = min(PROJ_ROWS, seq if select else m)
    n_rope = rope_cos.shape[0] // tm
    per_seq = seq // tm
    nb = max(seq // MOBA_BLOCK, SUBLANES)
    rows = pl.BlockSpec((tm, MOBA_W), lambda i: (i, 0))
    rope_spec = pl.BlockSpec((tm, MOBA_HD), lambda i: (i % n_rope, 0))
    weights = [w["mq"], w["mk"], w["mv"], w["qg"], w["kg"]]
    w_bytes = sum(int(a.size) * a.dtype.itemsize for a in weights)
    out_bytes = tm * MOBA_W * 3 * 4 + tm * MOBA_HEADS * 2 * MOBA_HD * 2
    need = w_bytes + 2 * out_bytes + 2 * tm * D_MODEL * 4 + 8 * tm * MOBA_W * 4 + (4 << 20)
    return pl.pallas_call(
        functools.partial(_moba_proj_body, select=select, tiles_per_seq=per_seq),
        grid=(m // tm,),
        in_specs=[pl.BlockSpec((tm, D_MODEL), lambda i: (i, 0)), _resident((1, D_MODEL))]
                 + [_resident(a.shape) for a in weights] + [rope_spec, rope_spec],
        out_specs=[rows, rows, rows,
                   pl.BlockSpec((1, MOBA_HEADS, tm, 2 * MOBA_HD), lambda i: (i // per_seq, 0, i % per_seq, 0))],
        out_shape=[jax.ShapeDtypeStruct((m, MOBA_W), F32)] * 3
                  + [jax.ShapeDtypeStruct((n_seq if select else 1, MOBA_HEADS, seq if select else m, 2 * MOBA_HD), BF16)],
        scratch_shapes=[pltpu.VMEM((nb, MOBA_W), F32)],
        compiler_params=pltpu.CompilerParams(dimension_semantics=("arbitrary",),
                                             vmem_limit_bytes=_vmem_limit(need)),
        name="moba_input_projection",
    )(x, mix_g, *weights, rope_cos, rope_sin)


def _gla_tables():
    c = GLA_CHUNK
    t = np.arange(c)
    le = t[None, :] <= t[:, None]
    gt = t[None, :] > t[:, None]
    masks = []
    s = c // 2
    while s >= 1:
        same = (t // (2 * s))[:, None] == (t // (2 * s))[None, :]
        right = (t % (2 * s)) >= s
        masks.append(same & right[:, None] & ~right[None, :])
        s //= 2
    masks.append(np.eye(c, dtype=bool))
    tail = np.concatenate([gt.T, np.ones((c, c), dtype=bool)], axis=1)
    return le.astype(np.float32), np.stack(masks).astype(np.float32), tail.astype(np.float32)


def _level_reference(b, s):
    c, dk = b.shape
    if s >= SUBLANES:
        blocks = b.reshape(c // (2 * s), 2 * s, dk)
        return jnp.broadcast_to(blocks[:, s - 1:s, :], blocks.shape).reshape(c, dk)
    rows8 = b.reshape(c // SUBLANES, SUBLANES, dk)
    sub = lax.broadcasted_iota(jnp.int32, rows8.shape, 1)
    ref = jnp.broadcast_to(rows8[:, SUBLANES - s - 1:SUBLANES - s, :], rows8.shape)
    for first in range(SUBLANES - 4 * s, -1, -2 * s):
        ref = jnp.where(sub < first + 2 * s, jnp.broadcast_to(rows8[:, first + s - 1:first + s, :], rows8.shape), ref)
    return ref.reshape(c, dk)


def _gla_body(q_ref, k_ref, la_ref, kt_ref, lat_ref, v_ref, gg_ref, ng_ref, le_ref, masks_ref, tail_ref,
              o_ref, s_ref):
    c = GLA_CHUNK
    n_lvl = masks_ref.shape[0] - 1

    @pl.when(pl.program_id(1) == 0)
    def _():
        s_ref[...] = jnp.zeros_like(s_ref)

    le = le_ref[...]
    tail = tail_ref[...]
    for hd in range(GLA_HEADS):
        ks = slice(hd * GLA_HDK, (hd + 1) * GLA_HDK)
        vs = slice(hd * GLA_HDV, (hd + 1) * GLA_HDV)
        q = q_ref[:, ks]
        k = k_ref[:, ks]
        v = v_ref[:, vs]
        state = s_ref[0, hd]
        b = sum(_dot(le, p) for p in _split3(la_ref[:, ks]))
        et = jnp.exp(sum(_dot(p, tail) for p in _split3(lat_ref[ks, :])))
        out = _dot((q * jnp.exp(b)).astype(BF16), state.astype(BF16))
        attn = masks_ref[n_lvl] * _dot_nt(q.astype(BF16), k.astype(BF16))
        for lv in range(n_lvl):
            e = jnp.exp(-jnp.abs(b - _level_reference(b, c >> (lv + 1))))
            attn = attn + masks_ref[lv] * _dot_nt((q * e).astype(BF16), (k * e).astype(BF16))
        out = out + _dot(attn.astype(BF16), v)
        k_dec = (kt_ref[ks, :] * et[:, :c]).astype(BF16)
        decay = et[:, c:]
        s_ref[0, hd] = jnp.concatenate([state[:, :c] * decay, state[:, c:] * decay], axis=1) + _dot(k_dec, v)
        gate = gg_ref[:, vs].astype(F32)
        o_ref[:, vs] = (_rms_norm(out, ng_ref[...]) * (gate * jax.nn.sigmoid(gate))).astype(BF16)


def _gla_prompt(gq, gk, la, gk_t, la_t, gv, gg, norm_g, batch, seq):
    c = GLA_CHUNK
    n_chunks = seq // c
    le, masks, tail = (jnp.asarray(a, BF16 if i != 1 else F32) for i, a in enumerate(_gla_tables()))

    def rows(width):
        return pl.BlockSpec((c, width), lambda b, t: (b * n_chunks + t, 0))

    cols = pl.BlockSpec((GLA_DK, c), lambda b, t: (0, b * n_chunks + t))
    need = 2 * (5 * c * GLA_DK * 4 + 3 * c * GLA_DV * 2) + 4 * GLA_DK * GLA_HDV * 4 + (8 << 20)
    return pl.pallas_call(
        _gla_body,
        grid=(batch, n_chunks),
        in_specs=[rows(GLA_DK), rows(GLA_DK), rows(GLA_DK), cols, cols, rows(GLA_DV), rows(GLA_DV),
                  _resident((1, GLA_HDV)), _resident(le.shape), _resident(masks.shape), _resident(tail.shape)],
        out_specs=[rows(GLA_DV),
                   pl.BlockSpec((1, GLA_HEADS, GLA_HDK, GLA_HDV), lambda b, t: (b, 0, 0, 0))],
        out_shape=[jax.ShapeDtypeStruct((batch * seq, GLA_DV), BF16),
                   jax.ShapeDtypeStruct((batch, GLA_HEADS, GLA_HDK, GLA_HDV), F32)],
        compiler_params=pltpu.CompilerParams(dimension_semantics=("parallel", "arbitrary"),
                                             vmem_limit_bytes=_vmem_limit(need)),
        name="gla_prompt_chunks",
    )(gq, gk, la, gk_t, la_t, gv, gg, norm_g, le, masks, tail)


def _moba_attend_body(pt_ref, qe_ref, k_ref, v_ref, *refs):
    pages, (o_ref, ksum_ref, acc_ref, m_ref) = refs[:-4], refs[-4:]
    j = pl.program_id(2)
    tb = MOBA_BLOCK
    tkv = k_ref.shape[0]
    n_groups = qe_ref.shape[2] // (tb * MOBA_GROUP)

    @pl.when(j == 0)
    def _():
        acc_ref[...] = jnp.zeros_like(acc_ref)
        m_ref[...] = jnp.full_like(m_ref, NEG_INIT)

    k = k_ref[...].astype(BF16)
    lane = lax.broadcasted_iota(jnp.int32, (tkv, MOBA_HD), 1)
    key_block = j * (tkv // tb) + lax.broadcasted_iota(jnp.int32, (tkv, MOBA_HD), 0) // tb
    k_ext = jnp.concatenate([k, jnp.where(lane == key_block, MASK_BIG, 0.0).astype(BF16)], axis=1)
    v_ext = jnp.concatenate([v_ref[...].astype(BF16), jnp.ones((tkv, MOBA_HD), BF16)], axis=1)

    def update(rows, s):
        m_old = m_ref[rows, :]
        m_new = jnp.maximum(m_old, jnp.max(s, axis=1, keepdims=True))
        p = jnp.exp2(s - jnp.concatenate([m_new] * (tkv // MOBA_HD), axis=1)).astype(BF16)
        alpha = jnp.exp2(m_old - m_new)
        acc_new = jnp.concatenate([alpha, alpha], axis=1) * acc_ref[rows, :] + _dot(p, v_ext)
        m_ref[rows, :] = m_new
        acc_ref[rows, :] = acc_new
        return acc_new

    rows_per_group = tb * MOBA_GROUP

    def group_rows(g):
        return pl.ds(pl.multiple_of(g * rows_per_group, rows_per_group), rows_per_group)

    def scores(g):
        return _dot_nt(qe_ref[0, 0, group_rows(g), :], k_ext)

    first = (j * tkv) // rows_per_group
    q_pos = first * rows_per_group + lax.broadcasted_iota(jnp.int32, (rows_per_group, tkv), 0)
    k_pos = j * tkv + lax.broadcasted_iota(jnp.int32, (rows_per_group, tkv), 1)
    update(group_rows(first), jnp.where(k_pos <= q_pos, scores(first), -MASK_BIG))
    acc = acc_ref[pl.ds(pl.multiple_of(j * tkv, tkv), tkv), :]
    o_ref[...] = (acc[:, :MOBA_HD] / acc[:, MOBA_HD:]).astype(BF16)

    per_block = MOBA_BLOCK // PAGE_SIZE

    def page_sum(page_ref):
        return jnp.sum(jnp.sum(page_ref[...].reshape(4, PAGE_SIZE // 4, MOBA_HEADS, MOBA_HD), axis=1), axis=0)

    for blk in range(len(pages) // per_block):
        tot = page_sum(pages[blk * per_block])
        for p in range(1, per_block):
            tot = tot + page_sum(pages[blk * per_block + p])
        ksum_ref[0, blk] = tot

    def body(g, s):
        s_next = scores(jnp.minimum(g + 1, n_groups - 1))
        update(group_rows(g), s)
        return s_next

    lax.fori_loop(first + 1, n_groups, body, scores(jnp.minimum(first + 1, n_groups - 1)))


def _moba_attend(qe, k, v, batch, seq, cache, layer, page_table_flat):
    tkv = MOBA_BLOCK * MOBA_KV_BLOCKS
    nb = seq // tkv
    n_steps = batch * MOBA_HEADS * nb
    per_block = MOBA_BLOCK // PAGE_SIZE
    pages_per_step = page_table_flat.shape[0] // n_steps
    assert pages_per_step * n_steps == page_table_flat.shape[0] and pages_per_step % per_block == 0
    kv = pl.BlockSpec((tkv, MOBA_HD), lambda b, h, j, pt: (b * nb + j, h))

    def step(b, h, j):
        return (b * MOBA_HEADS + h) * nb + j

    def page_spec(i):
        return pl.BlockSpec((None, None, PAGE_SIZE, MOBA_HEADS, MOBA_HD),
                            lambda b, h, j, pt: (layer, pt[step(b, h, j) * pages_per_step + i], 0, 0, 0))

    blocks_per_step = pages_per_step // per_block
    need = (2 * seq * 2 * MOBA_HD * 2 + 3 * seq * MOBA_HD * 4 + 2 * pages_per_step * PAGE_SIZE * MOBA_W * 4
            + (16 << 20))
    out, sums = pl.pallas_call(
        _moba_attend_body,
        grid_spec=pltpu.PrefetchScalarGridSpec(
            num_scalar_prefetch=1,
            grid=(batch, MOBA_HEADS, nb),
            in_specs=[pl.BlockSpec((1, 1, seq, 2 * MOBA_HD), lambda b, h, j, pt: (b, h, 0, 0)), kv, kv]
                     + [page_spec(i) for i in range(pages_per_step)],
            out_specs=[kv, pl.BlockSpec((1, blocks_per_step, MOBA_HEADS, MOBA_HD),
                                        lambda b, h, j, pt: (step(b, h, j), 0, 0, 0))],
            scratch_shapes=[pltpu.VMEM((seq, 2 * MOBA_HD), F32), pltpu.VMEM((seq, MOBA_HD), F32)],
        ),
        out_shape=[jax.ShapeDtypeStruct((batch * seq, MOBA_W), BF16),
                   jax.ShapeDtypeStruct((n_steps, blocks_per_step, MOBA_HEADS, MOBA_HD), F32)],
        compiler_params=pltpu.CompilerParams(dimension_semantics=("parallel", "parallel", "arbitrary"),
                                             vmem_limit_bytes=_vmem_limit(need)),
        name="moba_prompt_attention",
    )(page_table_flat, qe, k, v, *([cache] * pages_per_step))
    return out, sums


def _merge_body(x_ref, g_ref, oa_ref, ob_ref, wga, wgb, wpa, wpb, wo, o_ref):
    x = x_ref[...]
    h = _rms_norm(x, g_ref[...]).astype(BF16)
    mix = jax.nn.sigmoid(_dot(h, wga[...])) * _dot(oa_ref[...], wpa[...])
    mix = mix + jax.nn.sigmoid(_dot(h, wgb[...])) * _dot(ob_ref[...], wpb[...])
    o_ref[...] = x + _dot(mix.astype(BF16), wo[...])


def _merge(x, mix_g, o_gla, o_moba, w):
    m = x.shape[0]
    tm = min(FFN_ROWS, m)
    row32 = pl.BlockSpec((tm, D_MODEL), lambda i: (i, 0))
    sq = _resident((D_MODEL, D_MODEL))
    need = 5 * D_MODEL * D_MODEL * 2 + 4 * tm * D_MODEL * 4 + 4 * tm * D_MODEL * 2 + 6 * tm * D_MODEL * 4
    return pl.pallas_call(
        _merge_body,
        grid=(m // tm,),
        in_specs=[row32, _resident((1, D_MODEL)), row32, row32, sq, sq, sq, sq, sq],
        out_specs=row32,
        out_shape=jax.ShapeDtypeStruct((m, D_MODEL), F32),
        compiler_params=pltpu.CompilerParams(dimension_semantics=("parallel",),
                                             vmem_limit_bytes=_vmem_limit(need)),
        name="gated_merge_projection",
    )(x, mix_g, o_gla, o_moba, w["ga"], w["gb"], w["pa"], w["pb"], w["o"])


def _gla_sample_body(q_ref, k_ref, la_ref, v_ref, gg_ref, s_ref, ng_ref, o_ref, so_ref):
    def column(ref, hd):
        row = ref[0, :, hd * GLA_HDK:(hd + 1) * GLA_HDK]
        col = jnp.broadcast_to(row, (GLA_HDK, GLA_HDK)).T
        return jnp.concatenate([col] * (GLA_HDV // GLA_HDK), axis=1)

    for hd in range(GLA_HEADS):
        vs = slice(hd * GLA_HDV, (hd + 1) * GLA_HDV)
        v = v_ref[0, :, vs].astype(F32)
        s_new = jnp.exp(column(la_ref, hd)) * s_ref[0, hd] + column(k_ref, hd) * v
        so_ref[0, hd] = s_new
        out = jnp.sum(column(q_ref, hd) * s_new, axis=0, keepdims=True)
        gate = gg_ref[0, :, vs].astype(F32)
        o_ref[0, :, vs] = (_rms_norm(out, ng_ref[...]) * (gate * jax.nn.sigmoid(gate))).astype(BF16)


def _gla_sample(gq, gk, la, gv, gg, state, norm_g):
    n = gq.shape[0]
    assert GLA_HDV % GLA_HDK == 0
    key_row = pl.BlockSpec((1, 1, GLA_DK), lambda i: (i, 0, 0))
    val_row = pl.BlockSpec((1, 1, GLA_DV), lambda i: (i, 0, 0))
    st = pl.BlockSpec((1, GLA_HEADS, GLA_HDK, GLA_HDV), lambda i: (i, 0, 0, 0))
    as_row = lambda a: a.reshape(n, 1, a.shape[-1])
    o, s_new = pl.pallas_call(
        _gla_sample_body,
        grid=(n,),
        in_specs=[key_row, key_row, key_row, val_row, val_row, st, _resident((1, GLA_HDV))],
        out_specs=[val_row, st],
        out_shape=[jax.ShapeDtypeStruct((n, 1, GLA_DV), BF16),
                   jax.ShapeDtypeStruct((n, GLA_HEADS, GLA_HDK, GLA_HDV), F32)],
        compiler_params=pltpu.CompilerParams(dimension_semantics=("parallel",),
                                             vmem_limit_bytes=_vmem_limit(16 << 20)),
        name="gla_sample_step",
    )(as_row(gq), as_row(gk), as_row(la), as_row(gv), as_row(gg), state, norm_g)
    return o.reshape(n, GLA_DV), s_new


def _sample_select_body(q_ref, ksum_ref, sel_ref):
    nb = ksum_ref.shape[1]
    q = q_ref[0]
    lane = lax.broadcasted_iota(jnp.int32, (nb, LANES), 1)
    row = lax.broadcasted_iota(jnp.int32, (nb, LANES), 0).astype(F32)
    gate = jnp.full((nb, LANES), -jnp.inf, F32)
    for hd in range(MOBA_HEADS):
        sl = slice(hd * MOBA_HD, (hd + 1) * MOBA_HD)
        g = jnp.sum(ksum_ref[0, :, sl] * (1.0 / MOBA_BLOCK) * q[:, sl], axis=1, keepdims=True)
        gate = jnp.where(lane == hd, g, gate)
    out_row = lax.broadcasted_iota(jnp.int32, (SUBLANES, LANES), 0)
    out = jnp.zeros((SUBLANES, LANES), jnp.int32)
    for r in range(MOBA_TOPK):
        best = jnp.max(gate, axis=0, keepdims=True)
        first = jnp.min(jnp.where(gate == best, row, float(nb)), axis=0, keepdims=True)
        out = jnp.where(out_row == r, first.astype(jnp.int32), out)
        gate = jnp.where(row == first, -jnp.inf, gate)
    sel_ref[0] = out


def _sample_select(q, ksum):
    n, nb = ksum.shape[0], ksum.shape[1]
    sel = pl.pallas_call(
        _sample_select_body,
        grid=(n,),
        in_specs=[pl.BlockSpec((1, 1, MOBA_W), lambda i: (i, 0, 0)),
                  pl.BlockSpec((1, nb, MOBA_W), lambda i: (i, 0, 0))],
        out_specs=pl.BlockSpec((1, SUBLANES, LANES), lambda i: (i, 0, 0)),
        out_shape=jax.ShapeDtypeStruct((n, SUBLANES, LANES), jnp.int32),
        compiler_params=pltpu.CompilerParams(dimension_semantics=("parallel",)),
        name="moba_sample_select",
    )(q.reshape(n, 1, MOBA_W), ksum)
    return jnp.transpose(sel[:, :MOBA_TOPK, :MOBA_HEADS], (0, 2, 1)).reshape(-1)


def _sample_attend_body(pt_ref, sel_ref, q_ref, kn_ref, vn_ref, ck_hbm, cv_hbm, o_ref, k_buf, v_buf, sem,
                        *, layer, n_pages):
    per_block = MOBA_BLOCK // PAGE_SIZE
    n_sel = MOBA_TOPK * per_block
    n_slots = MOBA_HEADS * n_sel
    seq = pl.program_id(0)
    scale = MOBA_HD ** -0.5

    def page_copies(s, hd, i):
        blk = sel_ref[(s * MOBA_HEADS + hd) * MOBA_TOPK + i // per_block]
        page = pt_ref[s * n_pages + blk * per_block + i % per_block]
        slot = (s % 2) * n_slots + hd * n_sel + i
        return (pltpu.make_async_copy(ck_hbm.at[layer, page, :, hd, :], k_buf.at[slot], sem.at[0, slot]),
                pltpu.make_async_copy(cv_hbm.at[layer, page, :, hd, :], v_buf.at[slot], sem.at[1, slot]))

    def start_fetch(s):
        for hd in range(MOBA_HEADS):
            for i in range(n_sel):
                for cp in page_copies(s, hd, i):
                    cp.start()

    @pl.when(seq == 0)
    def _():
        start_fetch(seq)

    @pl.when(seq + 1 < pl.num_programs(0))
    def _():
        start_fetch(seq + 1)

    for hd in range(MOBA_HEADS):
        for i in range(n_sel):
            for cp in page_copies(seq, hd, i):
                cp.wait()

    base = (seq % 2) * n_slots
    for hd in range(MOBA_HEADS):
        sl = slice(hd * MOBA_HD, (hd + 1) * MOBA_HD)
        q = q_ref[0, :, sl]
        s_new = jnp.sum(kn_ref[0, :, sl] * q, axis=1, keepdims=True) * scale
        scores = [jnp.sum(k_buf[base + hd * n_sel + i] * q, axis=1, keepdims=True) * scale for i in range(n_sel)]
        top = s_new
        for s in scores:
            top = jnp.maximum(top, jnp.max(s, axis=0, keepdims=True))
        p_new = jnp.exp(s_new - top)
        denom = p_new
        acc = p_new * vn_ref[0, :, sl]
        for i, s in enumerate(scores):
            p = jnp.exp(s - top)
            denom = denom + jnp.sum(p, axis=0, keepdims=True)
            acc = acc + jnp.sum(p * v_buf[base + hd * n_sel + i], axis=0, keepdims=True)
        o_ref[0, :, sl] = (acc / denom).astype(BF16)


def _sample_attend(q, k_new, v_new, cache_k, cache_v, layer, page_table_flat, sel_flat, n_pages):
    n = q.shape[0]
    n_slots = MOBA_HEADS * MOBA_TOPK * (MOBA_BLOCK // PAGE_SIZE)
    tok = pl.BlockSpec((1, 1, MOBA_W), lambda s, pt, sel: (s, 0, 0))
    hbm = pl.BlockSpec(memory_space=pl.ANY)
    as_tok = lambda a: a.reshape(n, 1, MOBA_W)
    out = pl.pallas_call(
        functools.partial(_sample_attend_body, layer=layer, n_pages=n_pages),
        grid_spec=pltpu.PrefetchScalarGridSpec(
            num_scalar_prefetch=2,
            grid=(n,),
            in_specs=[tok, tok, tok, hbm, hbm],
            out_specs=tok,
            scratch_shapes=[pltpu.VMEM((2 * n_slots, PAGE_SIZE, MOBA_HD), F32),
                            pltpu.VMEM((2 * n_slots, PAGE_SIZE, MOBA_HD), F32),
                            pltpu.SemaphoreType.DMA((2, 2 * n_slots))],
        ),
        out_shape=jax.ShapeDtypeStruct((n, 1, MOBA_W), BF16),
        compiler_params=pltpu.CompilerParams(dimension_semantics=("arbitrary",),
                                             vmem_limit_bytes=_vmem_limit(5 * n_slots * PAGE_SIZE * MOBA_HD * 4)),
        name="moba_sample_attention",
    )(page_table_flat, sel_flat, as_tok(q), as_tok(k_new), as_tok(v_new), cache_k, cache_v)
    return out.reshape(n, MOBA_W)


def _rope_tables(pos):
    half = MOBA_HD // 2
    inv = ROPE_THETA ** (-jnp.arange(half, dtype=F32) / half)
    ang = pos.astype(F32)[:, None] * inv[None, :]
    cos, sin = jnp.cos(ang), jnp.sin(ang)
    return jnp.concatenate([cos, cos], axis=1), jnp.concatenate([-sin, sin], axis=1)


def _layer_weights(w_in, w_a2, b_a, q_norm_g, k_norm_g, w_pa, w_pb, w_o):
    offs = np.concatenate([[0], np.cumsum(IN_SPLITS)])
    gq, gk, gv, gg, glr, mq, mk, mv, ga, gb = (w_in[:, offs[i]:offs[i + 1]] for i in range(len(IN_SPLITS)))
    glr = jnp.pad(glr, ((0, 0), (0, LANES - GLA_RANK)))
    a2 = jnp.pad(w_a2, ((0, LANES - GLA_RANK), (0, 0)))
    b16 = lambda a: a.astype(BF16)
    return dict(
        gq=b16(gq), gk=b16(gk), gv=b16(gv), gg=b16(gg), glr=b16(glr), a2=b16(a2), ba=b_a.reshape(1, GLA_DK),
        gk_t=b16(gk.T), glr_t=b16(glr.T), a2_t=b16(a2.T), ba_t=b_a.reshape(GLA_DK, 1),
        mq=b16(mq), mk=b16(mk), mv=b16(mv),
        qg=jnp.tile(q_norm_g, MOBA_HEADS).reshape(1, MOBA_W), kg=jnp.tile(k_norm_g, MOBA_HEADS).reshape(1, MOBA_W),
        ga=b16(ga), gb=b16(gb), pa=b16(w_pa), pb=b16(w_pb), o=b16(w_o))


def kernel(x_prompt, x_sample, cache_k, cache_v, state_gla, page_table, ffn1_g, ffn1_wg, ffn1_wu, ffn1_wd, mix_g, w_in, w_a2, b_a, gla_norm_g, q_norm_g, k_norm_g, w_pa, w_pb, w_o, ffn2_g, ffn2_wg, ffn2_wu, ffn2_wd):
    batch, seq, _ = x_prompt.shape
    n_dec, dec_seq, _ = x_sample.shape
    n_pages = page_table.shape[1]
    depth = w_in.shape[0]
    past_len = n_pages * PAGE_SIZE
    assert dec_seq == 1 and seq % MOBA_BLOCK == 0 and past_len % MOBA_BLOCK == 0
    assert past_len // MOBA_BLOCK >= MOBA_TOPK
    assert (seq // MOBA_BLOCK) % MOBA_GROUP == 0 and MOBA_GROUP % MOBA_KV_BLOCKS == 0

    yp = x_prompt.reshape(batch * seq, D_MODEL)
    ys = x_sample.reshape(n_dec, D_MODEL)
    rope_p = _rope_tables(jnp.arange(seq, dtype=jnp.int32))
    rope_s = _rope_tables(jnp.full((n_dec,), past_len, jnp.int32))
    pt_flat = page_table.reshape(-1)
    row = lambda a: a.reshape(1, -1)
    outs = [[] for _ in range(6)]
    for l in range(depth):
        w = _layer_weights(w_in[l], w_a2[l], b_a[l], q_norm_g[l], k_norm_g[l], w_pa[l], w_pb[l], w_o[l])
        ffn1 = (row(ffn1_g[l]), ffn1_wg[l].astype(BF16), ffn1_wu[l].astype(BF16), ffn1_wd[l].astype(BF16))
        ffn2 = (row(ffn2_g[l]), ffn2_wg[l].astype(BF16), ffn2_wu[l].astype(BF16), ffn2_wd[l].astype(BF16))
        norm_g = row(gla_norm_g[l])

        x1 = _ffn(yp, *ffn1)
        gq, gk, gv, gg, la, gk_t, la_t = _gla_proj(x1, row(mix_g[l]), w)
        o_gla, s_prompt = _gla_prompt(gq, gk, la, gk_t, la_t, gv, gg, norm_g, batch, seq)
        _, k, v, qe = _moba_proj(x1, row(mix_g[l]), w, *rope_p, n_seq=batch, select=True)
        o_moba, page_sums = _moba_attend(qe, k, v, batch, seq, cache_k, l, pt_flat)
        yp = _ffn(_merge(x1, row(mix_g[l]), o_gla, o_moba, w), *ffn2)
        outs[0].append(k.reshape(batch, seq, MOBA_HEADS, MOBA_HD))
        outs[1].append(v.reshape(batch, seq, MOBA_HEADS, MOBA_HD))
        outs[2].append(s_prompt)

        x1 = _ffn(ys, *ffn1)
        gq, gk, gv, gg, la, _, _ = _gla_proj(x1, row(mix_g[l]), w)
        q, k, v, _ = _moba_proj(x1, row(mix_g[l]), w, *rope_s, n_seq=1, select=False)
        o_gla, s_sample = _gla_sample(gq, gk, la, gv, gg, state_gla[l], norm_g)
        sel = _sample_select(q, page_sums.reshape(n_dec, n_pages * PAGE_SIZE // MOBA_BLOCK, MOBA_W))
        o_moba = _sample_attend(q, k, v, cache_k, cache_v, l, pt_flat, sel, n_pages)
        ys = _ffn(_merge(x1, row(mix_g[l]), o_gla, o_moba, w), *ffn2)
        outs[3].append(k.reshape(n_dec, 1, MOBA_HEADS, MOBA_HD))
        outs[4].append(v.reshape(n_dec, 1, MOBA_HEADS, MOBA_HD))
        outs[5].append(s_sample)

    return (yp.reshape(batch, seq, D_MODEL), ys.reshape(n_dec, 1, D_MODEL),
            jnp.stack(outs[0]), jnp.stack(outs[1]), jnp.stack(outs[2]),
            jnp.stack(outs[3]), jnp.stack(outs[4]), jnp.stack(outs[5]))
```

```python
import functools

import numpy as np
import jax
import jax.numpy as jnp
from jax import lax
from jax.experimental import pallas as pl
from jax.experimental.pallas import tpu as pltpu

F32 = jnp.float32
BF16 = jnp.bfloat16

D_MODEL = 1024
D_FF = 2816
GLA_HEADS = 4
GLA_HDK = 128
GLA_HDV = 256
GLA_DK = GLA_HEADS * GLA_HDK
GLA_DV = GLA_HEADS * GLA_HDV
GLA_RANK = 16
GLA_TAU = 16.0
MOBA_HEADS = 8
MOBA_HD = 128
MOBA_W = MOBA_HEADS * MOBA_HD
MOBA_BLOCK = 256
MOBA_TOPK = 3
ROPE_THETA = 10000.0
EPS = 1e-6
PAGE_SIZE = 128
IN_SPLITS = (GLA_DK, GLA_DK, GLA_DV, GLA_DV, GLA_RANK, MOBA_W, MOBA_W, MOBA_W, D_MODEL, D_MODEL)

LANES = 128
SUBLANES = 8
VMEM_BYTES = 64 * 1024 * 1024

FF_CHUNK = 256
FFN_ROWS = 512
PROJ_ROWS = 512
GLA_CHUNK = 128
MOBA_GROUP = 4
MOBA_KV_BLOCKS = 2
MOBA_VISITS = 4
MASK_BIG = 2.0 ** 100
NEG_INIT = -1.0e38
LOG2_E = 1.4426950408889634


def _vmem_limit(nbytes):
    return int(min(VMEM_BYTES - (4 << 20), max(nbytes, 16 << 20)))


def _resident(shape):
    return pl.BlockSpec(shape, lambda *_: (0,) * len(shape), pipeline_mode=pl.Buffered(1))


def _dot(a, b):
    return jnp.dot(a, b, preferred_element_type=F32)


def _dot_nt(a, b):
    return lax.dot_general(a, b, (((1,), (1,)), ((), ())), preferred_element_type=F32)


def _rms_norm(x, g):
    return x * lax.rsqrt(jnp.mean(x * x, axis=-1, keepdims=True) + EPS) * g


def _log_sigmoid(z):
    return jnp.minimum(z, 0.0) - jnp.log1p(jnp.exp(-jnp.abs(z)))


def _split3(x):
    hi = x.astype(BF16)
    r = x - hi.astype(F32)
    mid = r.astype(BF16)
    lo = (r - mid.astype(F32)).astype(BF16)
    return hi, mid, lo


def _ffn_body(x_ref, g_ref, wg_ref, wu_ref, wd_ref, o_ref):
    x = x_ref[...]
    h = _rms_norm(x, g_ref[...]).astype(BF16)
    acc = jnp.zeros_like(x)
    for c in range(D_FF // FF_CHUNK):
        sl = slice(c * FF_CHUNK, (c + 1) * FF_CHUNK)
        a = _dot(h, wg_ref[:, sl])
        u = _dot(h, wu_ref[:, sl])
        act = (a * jax.nn.sigmoid(a) * u).astype(BF16)
        acc = acc + _dot(act, wd_ref[sl, :])
    o_ref[...] = x + 0.5 * acc


def _ffn(x, g, wg, wu, wd):
    m = x.shape[0]
    tm = min(FFN_ROWS, m)
    row = pl.BlockSpec((tm, D_MODEL), lambda i: (i, 0))
    need = 3 * D_MODEL * D_FF * 2 + 4 * tm * D_MODEL * 4 + 6 * tm * D_MODEL * 4
    return pl.pallas_call(
        _ffn_body,
        grid=(m // tm,),
        in_specs=[row, _resident((1, D_MODEL)), _resident((D_MODEL, D_FF)), _resident((D_MODEL, D_FF)),
                  _resident((D_FF, D_MODEL))],
        out_specs=row,
        out_shape=jax.ShapeDtypeStruct((m, D_MODEL), F32),
        compiler_params=pltpu.CompilerParams(dimension_semantics=("parallel",),
                                             vmem_limit_bytes=_vmem_limit(need)),
        name="swiglu_half_step",
    )(x, g, wg, wu, wd)


def _gla_proj_body(x_ref, g_ref, wgq, wgk, wgv, wgg, wglr, wa2, ba, wgk_t, wglr_t, wa2_t, ba_t,
                   gq_o, gk_o, gv_o, gg_o, la_o, gkt_o, lat_o):
    h = _rms_norm(x_ref[...], g_ref[...]).astype(BF16)
    gq_o[...] = _dot(h, wgq[...]) * (GLA_HDK ** -0.5)
    gk_o[...] = _dot(h, wgk[...])
    gv_o[...] = _dot(h, wgv[...]).astype(BF16)
    gg_o[...] = _dot(h, wgg[...]).astype(BF16)
    glr = _dot(h, wglr[...]).astype(BF16)
    la_o[...] = _log_sigmoid(_dot(glr, wa2[...]) + ba[...]) * (1.0 / GLA_TAU)
    gkt_o[...] = _dot_nt(wgk_t[...], h)
    glr_t = _dot_nt(wglr_t[...], h).astype(BF16)
    lat_o[...] = _log_sigmoid(_dot(wa2_t[...], glr_t) + ba_t[...]) * (1.0 / GLA_TAU)


def _gla_proj(x, mix_g, w):
    m = x.shape[0]
    tm = min(PROJ_ROWS, m)

    def rows(width):
        return pl.BlockSpec((tm, width), lambda i: (i, 0))

    cols = pl.BlockSpec((GLA_DK, tm), lambda i: (0, i))
    weights = [w["gq"], w["gk"], w["gv"], w["gg"], w["glr"], w["a2"], w["ba"], w["gk_t"], w["glr_t"],
               w["a2_t"], w["ba_t"]]
    out_shape = [
        jax.ShapeDtypeStruct((m, GLA_DK), F32), jax.ShapeDtypeStruct((m, GLA_DK), F32),
        jax.ShapeDtypeStruct((m, GLA_DV), BF16), jax.ShapeDtypeStruct((m, GLA_DV), BF16),
        jax.ShapeDtypeStruct((m, GLA_DK), F32),
        jax.ShapeDtypeStruct((GLA_DK, m), F32), jax.ShapeDtypeStruct((GLA_DK, m), F32),
    ]
    w_bytes = sum(int(a.size) * a.dtype.itemsize for a in weights)
    need = w_bytes + 2 * tm * (5 * GLA_DK * 4 + 2 * GLA_DV * 2) + 2 * tm * D_MODEL * 4 + 6 * tm * GLA_DV * 4
    return pl.pallas_call(
        _gla_proj_body,
        grid=(m // tm,),
        in_specs=[rows(D_MODEL), _resident((1, D_MODEL))] + [_resident(a.shape) for a in weights],
        out_specs=[rows(GLA_DK), rows(GLA_DK), rows(GLA_DV), rows(GLA_DV), rows(GLA_DK), cols, cols],
        out_shape=out_shape,
        compiler_params=pltpu.CompilerParams(dimension_semantics=("parallel",),
                                             vmem_limit_bytes=_vmem_limit(need)),
        name="gla_input_projection",
    )(x, mix_g, *weights)


def _moba_proj_body(x_ref, g_ref, wmq, wmk, wmv, qg, kg, rc, rs, q_o, k_o, v_o, qe_o, ksum_ref, *,
                    select, tiles_per_seq):
    tile = pl.program_id(0) % tiles_per_seq
    if select:
        @pl.when(tile == 0)
        def _():
            ksum_ref[...] = jnp.zeros_like(ksum_ref)

    h = _rms_norm(x_ref[...], g_ref[...]).astype(BF16)
    mq = _dot(h, wmq[...])
    mk = _dot(h, wmk[...])
    v_o[...] = _dot(h, wmv[...])
    cos = rc[...]
    sin = rs[...]
    for hd in range(MOBA_HEADS):
        sl = slice(hd * MOBA_HD, (hd + 1) * MOBA_HD)
        qh = _rms_norm(mq[:, sl], qg[:, sl])
        q_o[:, sl] = qh * cos + pltpu.roll(qh, MOBA_HD // 2, 1) * sin
        kh = _rms_norm(mk[:, sl], kg[:, sl])
        k_o[:, sl] = kh * cos + pltpu.roll(kh, MOBA_HD // 2, 1) * sin
    if not select:
        qe_o[...] = jnp.zeros_like(qe_o)
        return

    tq = MOBA_BLOCK
    nb = ksum_ref.shape[0]
    eye =(lax.broadcasted_iota(jnp.int32, (tq, tq), 0) == lax.broadcasted_iota(jnp.int32, (tq, tq), 1))
    eye = jnp.where(eye, 1.0, 0.0).astype(BF16)
    for c in range(q_o.shape[0] // tq):
        rows = slice(c * tq, (c + 1) * tq)
        own_i = tile * (q_o.shape[0] // tq) + c
        own = own_i.astype(F32)
        ksum_ref[pl.ds(own_i, 1), :] = jnp.sum(k_o[rows, :], axis=0, keepdims=True)
        gates = []
        for hd in range(MOBA_HEADS):
            sl = slice(hd * MOBA_HD, (hd + 1) * MOBA_HD)
            q = q_o[rows, sl]
            means = ksum_ref[:, sl] * (1.0 / MOBA_BLOCK)
            q_hi = q.astype(BF16)
            q_lo = (q - q_hi.astype(F32)).astype(BF16)
            m_hi = means.astype(BF16)
            m_lo = (means - m_hi.astype(F32)).astype(BF16)
            gates.append(_dot_nt(m_hi, q_hi) + _dot_nt(m_lo, q_hi) + _dot_nt(m_hi, q_lo))
            qe_o[0, hd, rows, :MOBA_HD] = (q * (MOBA_HD ** -0.5 * LOG2_E)).astype(BF16)
        gate = jnp.concatenate(gates, axis=1)
        blk = lax.broadcasted_iota(jnp.int32, gate.shape, 0).astype(F32)
        gate = jnp.where(blk < own, gate, -jnp.inf)
        mask = jnp.where(blk == own, 0.0, -1.0)
        for _ in range(MOBA_TOPK):
            best = jnp.max(gate, axis=0, keepdims=True)
            first = jnp.min(jnp.where(gate == best, blk, float(nb)), axis=0, keepdims=True)
            first = jnp.where(best > -jnp.inf, first, -1.0)
            pick = blk == first
            mask = jnp.where(pick, 0.0, mask)
            gate = jnp.where(pick, -jnp.inf, gate)
        mask = jnp.concatenate([mask, jnp.full((LANES - nb, mask.shape[1]), -1.0, F32)], axis=0).astype(BF16)
        for hd in range(MOBA_HEADS):
            qe_o[0, hd, rows, MOBA_HD:] = _dot_nt(eye, mask[:, hd * tq:(hd + 1) * tq]).astype(BF16)


def _moba_proj(x, mix_g, w, rope_cos, rope_sin, n_seq, select):
    m = x.shape[0]
    seq = m // n_seq
    tm = min(PROJ_ROWS, seq if select else m)
    n_rope = rope_cos.shape[0] // tm
    per_seq = seq // tm
    nb = max(seq // MOBA_BLOCK, SUBLANES)
    rows = pl.BlockSpec((tm, MOBA_W), lambda i: (i, 0))
    rope_spec = pl.BlockSpec((tm, MOBA_HD), lambda i: (i % n_rope, 0))
    weights = [w["mq"], w["mk"], w["mv"], w["qg"], w["kg"]]
    w_bytes = sum(int(a.size) * a.dtype.itemsize for a in weights)
    out_bytes = tm * MOBA_W * 3 * 4 + tm * MOBA_HEADS * 2 * MOBA_HD * 2
    need = w_bytes + 2 * out_bytes + 2 * tm * D_MODEL * 4 + 8 * tm * MOBA_W * 4 + (4 << 20)
    return pl.pallas_call(
        functools.partial(_moba_proj_body, select=select, tiles_per_seq=per_seq),
        grid=(m // tm,),
        in_specs=[pl.BlockSpec((tm, D_MODEL), lambda i: (i, 0)), _resident((1, D_MODEL))]
                 + [_resident(a.shape) for a in weights] + [rope_spec, rope_spec],
        out_specs=[rows, rows, rows,
                   pl.BlockSpec((1, MOBA_HEADS, tm, 2 * MOBA_HD), lambda i: (i // per_seq, 0, i % per_seq, 0))],
        out_shape=[jax.ShapeDtypeStruct((m, MOBA_W), F32)] * 3
                  + [jax.ShapeDtypeStruct((n_seq if select else 1, MOBA_HEADS, seq if select else m, 2 * MOBA_HD), BF16)],
        scratch_shapes=[pltpu.VMEM((nb, MOBA_W), F32)],
        compiler_params=pltpu.CompilerParams(dimension_semantics=("arbitrary",),
                                             vmem_limit_bytes=_vmem_limit(need)),
        name="moba_input_projection",
    )(x, mix_g, *weights, rope_cos, rope_sin)


def _gla_tables():
    c = GLA_CHUNK
    t = np.arange(c)
    le = t[None, :] <= t[:, None]
    gt = t[None, :] > t[:, None]
    masks = []
    s = c // 2
    while s >= 1:
        same = (t // (2 * s))[:, None] == (t // (2 * s))[None, :]
        right = (t % (2 * s)) >= s
        masks.append(same & right[:, None] & ~right[None, :])
        s //= 2
    masks.append(np.eye(c, dtype=bool))
    tail = np.concatenate([gt.T, np.ones((c, c), dtype=bool)], axis=1)
    return le.astype(np.float32), np.stack(masks).astype(np.float32), tail.astype(np.float32)


def _level_reference(b, s):
    c, dk = b.shape
    if s >= SUBLANES:
        blocks = b.reshape(c // (2 * s), 2 * s, dk)
        return jnp.broadcast_to(blocks[:, s - 1:s, :], blocks.shape).reshape(c, dk)
    rows8 = b.reshape(c // SUBLANES, SUBLANES, dk)
    sub = lax.broadcasted_iota(jnp.int32, rows8.shape, 1)
    ref = jnp.broadcast_to(rows8[:, SUBLANES - s - 1:SUBLANES - s, :], rows8.shape)
    for first in range(SUBLANES - 4 * s, -1, -2 * s):
        ref = jnp.where(sub < first + 2 * s, jnp.broadcast_to(rows8[:, first + s - 1:first + s, :], rows8.shape), ref)
    return ref.reshape(c, dk)


def _gla_body(q_ref, k_ref, la_ref, kt_ref, lat_ref, v_ref, gg_ref, ng_ref, le_ref, masks_ref, tail_ref,
              o_ref, s_ref):
    c = GLA_CHUNK
    n_lvl = masks_ref.shape[0] - 1

    @pl.when(pl.program_id(1) == 0)
    def _():
        s_ref[...] = jnp.zeros_like(s_ref)

    le = le_ref[...]
    tail = tail_ref[...]
    for hd in range(GLA_HEADS):
        ks = slice(hd * GLA_HDK, (hd + 1) * GLA_HDK)
        vs = slice(hd * GLA_HDV, (hd + 1) * GLA_HDV)
        q = q_ref[:, ks]
        k = k_ref[:, ks]
        v = v_ref[:, vs]
        state = s_ref[0, hd]
        b = sum(_dot(le, p) for p in _split3(la_ref[:, ks]))
        et = jnp.exp(sum(_dot(p, tail) for p in _split3(lat_ref[ks, :])))
        out = _dot((q * jnp.exp(b)).astype(BF16), state.astype(BF16))
        attn = masks_ref[n_lvl] * _dot_nt(q.astype(BF16), k.astype(BF16))
        for lv in range(n_lvl):
            e = jnp.exp(-jnp.abs(b - _level_reference(b, c >> (lv + 1))))
            attn = attn + masks_ref[lv] * _dot_nt((q * e).astype(BF16), (k * e).astype(BF16))
        out = out + _dot(attn.astype(BF16), v)
        k_dec = (kt_ref[ks, :] * et[:, :c]).astype(BF16)
        decay = et[:, c:]
        s_ref[0, hd] = jnp.concatenate([state[:, :c] * decay, state[:, c:] * decay], axis=1) + _dot(k_dec, v)
        gate = gg_ref[:, vs].astype(F32)
        o_ref[:, vs] = (_rms_norm(out, ng_ref[...]) * (gate * jax.nn.sigmoid(gate))).astype(BF16)


def _gla_prompt(gq, gk, la, gk_t, la_t, gv, gg, norm_g, batch, seq):
    c = GLA_CHUNK
    n_chunks = seq // c
    le, masks, tail = (jnp.asarray(a, BF16 if i != 1 else F32) for i, a in enumerate(_gla_tables()))

    def rows(width):
        return pl.BlockSpec((c, width), lambda b, t: (b * n_chunks + t, 0))

    cols = pl.BlockSpec((GLA_DK, c), lambda b, t: (0, b * n_chunks + t))
    need = 2 * (5 * c * GLA_DK * 4 + 3 * c * GLA_DV * 2) + 4 * GLA_DK * GLA_HDV * 4 + (8 << 20)
    return pl.pallas_call(
        _gla_body,
        grid=(batch, n_chunks),
        in_specs=[rows(GLA_DK), rows(GLA_DK), rows(GLA_DK), cols, cols, rows(GLA_DV), rows(GLA_DV),
                  _resident((1, GLA_HDV)), _resident(le.shape), _resident(masks.shape), _resident(tail.shape)],
        out_specs=[rows(GLA_DV),
                   pl.BlockSpec((1, GLA_HEADS, GLA_HDK, GLA_HDV), lambda b, t: (b, 0, 0, 0))],
        out_shape=[jax.ShapeDtypeStruct((batch * seq, GLA_DV), BF16),
                   jax.ShapeDtypeStruct((batch, GLA_HEADS, GLA_HDK, GLA_HDV), F32)],
        compiler_params=pltpu.CompilerParams(dimension_semantics=("parallel", "arbitrary"),
                                             vmem_limit_bytes=_vmem_limit(need)),
        name="gla_prompt_chunks",
    )(gq, gk, la, gk_t, la_t, gv, gg, norm_g, le, masks, tail)


def _moba_attend_body(pt_ref, qe_ref, k_ref, v_ref, *refs):
    pages, (o_ref, ksum_ref, acc_ref, m_ref) = refs[:-4], refs[-4:]
    j = pl.program_id(2)
    tb = MOBA_BLOCK
    tkv = k_ref.shape[0]
    n_groups = qe_ref.shape[2] // (tb * MOBA_GROUP)

    @pl.when(j == 0)
    def _():
        acc_ref[...] = jnp.zeros_like(acc_ref)
        m_ref[...] = jnp.full_like(m_ref, NEG_INIT)

    k = k_ref[...].astype(BF16)
    lane = lax.broadcasted_iota(jnp.int32, (tkv, MOBA_HD), 1)
    key_block = j * (tkv // tb) + lax.broadcasted_iota(jnp.int32, (tkv, MOBA_HD), 0) // tb
    k_ext = jnp.concatenate([k, jnp.where(lane == key_block, MASK_BIG, 0.0).astype(BF16)], axis=1)
    v_ext = jnp.concatenate([v_ref[...].astype(BF16), jnp.ones((tkv, MOBA_HD), BF16)], axis=1)

    def update(rows, s):
        m_old = m_ref[rows, :]
        m_new = jnp.maximum(m_old, jnp.max(s, axis=1, keepdims=True))
        p = jnp.exp2(s - jnp.concatenate([m_new] * (tkv // MOBA_HD), axis=1)).astype(BF16)
        alpha = jnp.exp2(m_old - m_new)
        acc_new = jnp.concatenate([alpha, alpha], axis=1) * acc_ref[rows, :] + _dot(p, v_ext)
        m_ref[rows, :] = m_new
        acc_ref[rows, :] = acc_new
        return acc_new

    rows_per_group = tb * MOBA_GROUP

    def group_rows(g):
        return pl.ds(pl.multiple_of(g * rows_per_group, rows_per_group), rows_per_group)

    def scores(g):
        return _dot_nt(qe_ref[0, 0, group_rows(g), :], k_ext)

    first = (j * tkv) // rows_per_group
    n_later = n_groups - 1 - first
    n_extra = n_later % MOBA_VISITS

    def page_sum(page_ref):
        return jnp.sum(jnp.sum(page_ref[...].reshape(4, PAGE_SIZE // 4, MOBA_HEADS, MOBA_HD), axis=1), axis=0)

    def first_groups(extra):
        q_pos = first * rows_per_group + lax.broadcasted_iota(jnp.int32, (rows_per_group, tkv), 0)
        k_pos = j * tkv + lax.broadcasted_iota(jnp.int32, (rows_per_group, tkv), 1)
        ss = [jnp.where(k_pos <= q_pos, scores(first), -MASK_BIG)] + [scores(first + u) for u in range(1, extra + 1)]
        for u, s in enumerate(ss):
            update(group_rows(first + u), s)
        acc = acc_ref[pl.ds(pl.multiple_of(j * tkv, tkv), tkv), :]
        o_ref[...] = (acc[:, :MOBA_HD] / acc[:, MOBA_HD:]).astype(BF16)
        per_block = MOBA_BLOCK // PAGE_SIZE
        for blk in range(len(pages) // per_block):
            tot = page_sum(pages[blk * per_block])
            for p in range(1, per_block):
                tot = tot + page_sum(pages[blk * per_block + p])
            ksum_ref[0, blk] = tot

    for extra in range(MOBA_VISITS):
        pl.when(n_extra == extra)(functools.partial(first_groups, extra))

    def body(i, carry):
        g0 = first + 1 + n_extra + MOBA_VISITS * i
        ss = [scores(g0 + u) for u in range(MOBA_VISITS)]
        for u, s in enumerate(ss):
            update(group_rows(g0 + u), s)
        return carry

    lax.fori_loop(0, n_later // MOBA_VISITS, body, 0)


def _moba_attend(qe, k, v, batch, seq, cache, layer, page_table_flat):
    tkv = MOBA_BLOCK * MOBA_KV_BLOCKS
    nb = seq // tkv
    n_steps = batch * MOBA_HEADS * nb
    per_block = MOBA_BLOCK // PAGE_SIZE
    pages_per_step = page_table_flat.shape[0] // n_steps
    assert pages_per_step * n_steps == page_table_flat.shape[0] and pages_per_step % per_block == 0
    kv = pl.BlockSpec((tkv, MOBA_HD), lambda b, h, j, pt: (b * nb + j, h))

    def step(b, h, j):
        return (b * MOBA_HEADS + h) * nb + j

    def page_spec(i):
        return pl.BlockSpec((None, None, PAGE_SIZE, MOBA_HEADS, MOBA_HD),
                            lambda b, h, j, pt: (layer, pt[step(b, h, j) * pages_per_step + i], 0, 0, 0))

    blocks_per_step = pages_per_step // per_block
    need = (2 * seq * 2 * MOBA_HD * 2 + 3 * seq * MOBA_HD * 4 + 2 * pages_per_step * PAGE_SIZE * MOBA_W * 4
            + (16 << 20))
    out, sums = pl.pallas_call(
        _moba_attend_body,
        grid_spec=pltpu.PrefetchScalarGridSpec(
            num_scalar_prefetch=1,
            grid=(batch, MOBA_HEADS, nb),
            in_specs=[pl.BlockSpec((1, 1, seq, 2 * MOBA_HD), lambda b, h, j, pt: (b, h, 0, 0)), kv, kv]
                     + [page_spec(i) for i in range(pages_per_step)],
            out_specs=[kv, pl.BlockSpec((1, blocks_per_step, MOBA_HEADS, MOBA_HD),
                                        lambda b, h, j, pt: (step(b, h, j), 0, 0, 0))],
            scratch_shapes=[pltpu.VMEM((seq, 2 * MOBA_HD), F32), pltpu.VMEM((seq, MOBA_HD), F32)],
        ),
        out_shape=[jax.ShapeDtypeStruct((batch * seq, MOBA_W), BF16),
                   jax.ShapeDtypeStruct((n_steps, blocks_per_step, MOBA_HEADS, MOBA_HD), F32)],
        compiler_params=pltpu.CompilerParams(dimension_semantics=("parallel", "parallel", "arbitrary"),
                                             vmem_limit_bytes=_vmem_limit(need)),
        name="moba_prompt_attention",
    )(page_table_flat, qe, k, v, *([cache] * pages_per_step))
    return out, sums


def _merge_body(x_ref, g_ref, oa_ref, ob_ref, wga, wgb, wpa, wpb, wo, o_ref):
    x = x_ref[...]
    h = _rms_norm(x, g_ref[...]).astype(BF16)
    mix = jax.nn.sigmoid(_dot(h, wga[...])) * _dot(oa_ref[...], wpa[...])
    mix = mix + jax.nn.sigmoid(_dot(h, wgb[...])) * _dot(ob_ref[...], wpb[...])
    o_ref[...] = x + _dot(mix.astype(BF16), wo[...])


def _merge(x, mix_g, o_gla, o_moba, w):
    m = x.shape[0]
    tm = min(FFN_ROWS, m)
    row32 = pl.BlockSpec((tm, D_MODEL), lambda i: (i, 0))
    sq = _resident((D_MODEL, D_MODEL))
    need = 5 * D_MODEL * D_MODEL * 2 + 4 * tm * D_MODEL * 4 + 4 * tm * D_MODEL * 2 + 6 * tm * D_MODEL * 4
    return pl.pallas_call(
        _merge_body,
        grid=(m // tm,),
        in_specs=[row32, _resident((1, D_MODEL)), row32, row32, sq, sq, sq, sq, sq],
        out_specs=row32,
        out_shape=jax.ShapeDtypeStruct((m, D_MODEL), F32),
        compiler_params=pltpu.CompilerParams(dimension_semantics=("parallel",),
                                             vmem_limit_bytes=_vmem_limit(need)),
        name="gated_merge_projection",
    )(x, mix_g, o_gla, o_moba, w["ga"], w["gb"], w["pa"], w["pb"], w["o"])


def _gla_sample_body(q_ref, k_ref, la_ref, v_ref, gg_ref, s_ref, ng_ref, o_ref, so_ref):
    def column(ref, hd):
        row = ref[0, :, hd * GLA_HDK:(hd + 1) * GLA_HDK]
        col = jnp.broadcast_to(row, (GLA_HDK, GLA_HDK)).T
        return jnp.concatenate([col] * (GLA_HDV // GLA_HDK), axis=1)

    for hd in range(GLA_HEADS):
        vs = slice(hd * GLA_HDV, (hd + 1) * GLA_HDV)
        v = v_ref[0, :, vs].astype(F32)
        s_new = jnp.exp(column(la_ref, hd)) * s_ref[0, hd] + column(k_ref, hd) * v
        so_ref[0, hd] = s_new
        out = jnp.sum(column(q_ref, hd) * s_new, axis=0, keepdims=True)
        gate = gg_ref[0, :, vs].astype(F32)
        o_ref[0, :, vs] = (_rms_norm(out, ng_ref[...]) * (gate * jax.nn.sigmoid(gate))).astype(BF16)


def _gla_sample(gq, gk, la, gv, gg, state, norm_g):
    n = gq.shape[0]
    assert GLA_HDV % GLA_HDK == 0
    key_row = pl.BlockSpec((1, 1, GLA_DK), lambda i: (i, 0, 0))
    val_row = pl.BlockSpec((1, 1, GLA_DV), lambda i: (i, 0, 0))
    st = pl.BlockSpec((1, GLA_HEADS, GLA_HDK, GLA_HDV), lambda i: (i, 0, 0, 0))
    as_row = lambda a: a.reshape(n, 1, a.shape[-1])
    o, s_new = pl.pallas_call(
        _gla_sample_body,
        grid=(n,),
        in_specs=[key_row, key_row, key_row, val_row, val_row, st, _resident((1, GLA_HDV))],
        out_specs=[val_row, st],
        out_shape=[jax.ShapeDtypeStruct((n, 1, GLA_DV), BF16),
                   jax.ShapeDtypeStruct((n, GLA_HEADS, GLA_HDK, GLA_HDV), F32)],
        compiler_params=pltpu.CompilerParams(dimension_semantics=("parallel",),
                                             vmem_limit_bytes=_vmem_limit(16 << 20)),
        name="gla_sample_step",
    )(as_row(gq), as_row(gk), as_row(la), as_row(gv), as_row(gg), state, norm_g)
    return o.reshape(n, GLA_DV), s_new


def _sample_select_body(q_ref, ksum_ref, sel_ref):
    nb = ksum_ref.shape[1]
    q = q_ref[0]
    lane = lax.broadcasted_iota(jnp.int32, (nb, LANES), 1)
    row = lax.broadcasted_iota(jnp.int32, (nb, LANES), 0).astype(F32)
    gate = jnp.full((nb, LANES), -jnp.inf, F32)
    for hd in range(MOBA_HEADS):
        sl = slice(hd * MOBA_HD, (hd + 1) * MOBA_HD)
        g = jnp.sum(ksum_ref[0, :, sl] * (1.0 / MOBA_BLOCK) * q[:, sl], axis=1, keepdims=True)
        gate = jnp.where(lane == hd, g, gate)
    out_row = lax.broadcasted_iota(jnp.int32, (SUBLANES, LANES), 0)
    out = jnp.zeros((SUBLANES, LANES), jnp.int32)
    for r in range(MOBA_TOPK):
        best = jnp.max(gate, axis=0, keepdims=True)
        first = jnp.min(jnp.where(gate == best, row, float(nb)), axis=0, keepdims=True)
        out = jnp.where(out_row == r, first.astype(jnp.int32), out)
        gate = jnp.where(row == first, -jnp.inf, gate)
    sel_ref[0] = out


def _sample_select(q, ksum):
    n, nb = ksum.shape[0], ksum.shape[1]
    sel = pl.pallas_call(
        _sample_select_body,
        grid=(n,),
        in_specs=[pl.BlockSpec((1, 1, MOBA_W), lambda i: (i, 0, 0)),
                  pl.BlockSpec((1, nb, MOBA_W), lambda i: (i, 0, 0))],
        out_specs=pl.BlockSpec((1, SUBLANES, LANES), lambda i: (i, 0, 0)),
        out_shape=jax.ShapeDtypeStruct((n, SUBLANES, LANES), jnp.int32),
        compiler_params=pltpu.CompilerParams(dimension_semantics=("parallel",)),
        name="moba_sample_select",
    )(q.reshape(n, 1, MOBA_W), ksum)
    return jnp.transpose(sel[:, :MOBA_TOPK, :MOBA_HEADS], (0, 2, 1)).reshape(-1)


def _sample_attend_body(pt_ref, sel_ref, q_ref, kn_ref, vn_ref, ck_hbm, cv_hbm, o_ref, k_buf, v_buf, sem,
                        *, layer, n_pages):
    per_block = MOBA_BLOCK // PAGE_SIZE
    n_sel = MOBA_TOPK * per_block
    n_slots = MOBA_HEADS * n_sel
    seq = pl.program_id(0)
    scale = MOBA_HD ** -0.5

    def page_copies(s, hd, i):
        blk = sel_ref[(s * MOBA_HEADS + hd) * MOBA_TOPK + i // per_block]
        page = pt_ref[s * n_pages + blk * per_block + i % per_block]
        slot = (s % 2) * n_slots + hd * n_sel + i
        return (pltpu.make_async_copy(ck_hbm.at[layer, page, :, hd, :], k_buf.at[slot], sem.at[0, slot]),
                pltpu.make_async_copy(cv_hbm.at[layer, page, :, hd, :], v_buf.at[slot], sem.at[1, slot]))

    def start_fetch(s):
        for hd in range(MOBA_HEADS):
            for i in range(n_sel):
                for cp in page_copies(s, hd, i):
                    cp.start()

    @pl.when(seq == 0)
    def _():
        start_fetch(seq)

    @pl.when(seq + 1 < pl.num_programs(0))
    def _():
        start_fetch(seq + 1)

    for hd in range(MOBA_HEADS):
        for i in range(n_sel):
            for cp in page_copies(seq, hd, i):
                cp.wait()

    base = (seq % 2) * n_slots
    for hd in range(MOBA_HEADS):
        sl = slice(hd * MOBA_HD, (hd + 1) * MOBA_HD)
        q = q_ref[0, :, sl]
        s_new = jnp.sum(kn_ref[0, :, sl] * q, axis=1, keepdims=True) * scale
        scores = [jnp.sum(k_buf[base + hd * n_sel + i] * q, axis=1, keepdims=True) * scale for i in range(n_sel)]
        top = s_new
        for s in scores:
            top = jnp.maximum(top, jnp.max(s, axis=0, keepdims=True))
        p_new = jnp.exp(s_new - top)
        denom = p_new
        acc = p_new * vn_ref[0, :, sl]
        for i, s in enumerate(scores):
            p = jnp.exp(s - top)
            denom = denom + jnp.sum(p, axis=0, keepdims=True)
            acc = acc + jnp.sum(p * v_buf[base + hd * n_sel + i], axis=0, keepdims=True)
        o_ref[0, :, sl] = (acc / denom).astype(BF16)


def _sample_attend(q, k_new, v_new, cache_k, cache_v, layer, page_table_flat, sel_flat, n_pages):
    n = q.shape[0]
    n_slots = MOBA_HEADS * MOBA_TOPK * (MOBA_BLOCK // PAGE_SIZE)
    tok = pl.BlockSpec((1, 1, MOBA_W), lambda s, pt, sel: (s, 0, 0))
    hbm = pl.BlockSpec(memory_space=pl.ANY)
    as_tok = lambda a: a.reshape(n, 1, MOBA_W)
    out = pl.pallas_call(
        functools.partial(_sample_attend_body, layer=layer, n_pages=n_pages),
        grid_spec=pltpu.PrefetchScalarGridSpec(
            num_scalar_prefetch=2,
            grid=(n,),
            in_specs=[tok, tok, tok, hbm, hbm],
            out_specs=tok,
            scratch_shapes=[pltpu.VMEM((2 * n_slots, PAGE_SIZE, MOBA_HD), F32),
                            pltpu.VMEM((2 * n_slots, PAGE_SIZE, MOBA_HD), F32),
                            pltpu.SemaphoreType.DMA((2, 2 * n_slots))],
        ),
        out_shape=jax.ShapeDtypeStruct((n, 1, MOBA_W), BF16),
        compiler_params=pltpu.CompilerParams(dimension_semantics=("arbitrary",),
                                             vmem_limit_bytes=_vmem_limit(5 * n_slots * PAGE_SIZE * MOBA_HD * 4)),
        name="moba_sample_attention",
    )(page_table_flat, sel_flat, as_tok(q), as_tok(k_new), as_tok(v_new), cache_k, cache_v)
    return out.reshape(n, MOBA_W)


def _rope_tables(pos):
    half = MOBA_HD // 2
    inv = ROPE_THETA ** (-jnp.arange(half, dtype=F32) / half)
    ang = pos.astype(F32)[:, None] * inv[None, :]
    cos, sin = jnp.cos(ang), jnp.sin(ang)
    return jnp.concatenate([cos, cos], axis=1), jnp.concatenate([-sin, sin], axis=1)


def _layer_weights(w_in, w_a2, b_a, q_norm_g, k_norm_g, w_pa, w_pb, w_o):
    offs = np.concatenate([[0], np.cumsum(IN_SPLITS)])
    gq, gk, gv, gg, glr, mq, mk, mv, ga, gb = (w_in[:, offs[i]:offs[i + 1]] for i in range(len(IN_SPLITS)))
    glr = jnp.pad(glr, ((0, 0), (0, LANES - GLA_RANK)))
    a2 = jnp.pad(w_a2, ((0, LANES - GLA_RANK), (0, 0)))
    b16 = lambda a: a.astype(BF16)
    return dict(
        gq=b16(gq), gk=b16(gk), gv=b16(gv), gg=b16(gg), glr=b16(glr), a2=b16(a2), ba=b_a.reshape(1, GLA_DK),
        gk_t=b16(gk.T), glr_t=b16(glr.T), a2_t=b16(a2.T), ba_t=b_a.reshape(GLA_DK, 1),
        mq=b16(mq), mk=b16(mk), mv=b16(mv),
        qg=jnp.tile(q_norm_g, MOBA_HEADS).reshape(1, MOBA_W), kg=jnp.tile(k_norm_g, MOBA_HEADS).reshape(1, MOBA_W),
        ga=b16(ga), gb=b16(gb), pa=b16(w_pa), pb=b16(w_pb), o=b16(w_o))


def kernel(x_prompt, x_sample, cache_k, cache_v, state_gla, page_table, ffn1_g, ffn1_wg, ffn1_wu, ffn1_wd, mix_g, w_in, w_a2, b_a, gla_norm_g, q_norm_g, k_norm_g, w_pa, w_pb, w_o, ffn2_g, ffn2_wg, ffn2_wu, ffn2_wd):
    batch, seq, _ = x_prompt.shape
    n_dec, dec_seq, _ = x_sample.shape
    n_pages = page_table.shape[1]
    depth = w_in.shape[0]
    past_len = n_pages * PAGE_SIZE
    assert dec_seq == 1 and seq % MOBA_BLOCK == 0 and past_len % MOBA_BLOCK == 0
    assert past_len // MOBA_BLOCK >= MOBA_TOPK
    assert (seq // MOBA_BLOCK) % MOBA_GROUP == 0 and MOBA_GROUP % MOBA_KV_BLOCKS == 0

    yp = x_prompt.reshape(batch * seq, D_MODEL)
    ys = x_sample.reshape(n_dec, D_MODEL)
    rope_p = _rope_tables(jnp.arange(seq, dtype=jnp.int32))
    rope_s = _rope_tables(jnp.full((n_dec,), past_len, jnp.int32))
    pt_flat = page_table.reshape(-1)
    row = lambda a: a.reshape(1, -1)
    outs = [[] for _ in range(6)]
    for l in range(depth):
        w = _layer_weights(w_in[l], w_a2[l], b_a[l], q_norm_g[l], k_norm_g[l], w_pa[l], w_pb[l], w_o[l])
        ffn1 = (row(ffn1_g[l]), ffn1_wg[l].astype(BF16), ffn1_wu[l].astype(BF16), ffn1_wd[l].astype(BF16))
        ffn2 = (row(ffn2_g[l]), ffn2_wg[l].astype(BF16), ffn2_wu[l].astype(BF16), ffn2_wd[l].astype(BF16))
        norm_g = row(gla_norm_g[l])

        x1 = _ffn(yp, *ffn1)
        gq, gk, gv, gg, la, gk_t, la_t = _gla_proj(x1, row(mix_g[l]), w)
        o_gla, s_prompt = _gla_prompt(gq, gk, la, gk_t, la_t, gv, gg, norm_g, batch, seq)
        _, k, v, qe = _moba_proj(x1, row(mix_g[l]), w, *rope_p, n_seq=batch, select=True)
        o_moba, page_sums = _moba_attend(qe, k, v, batch, seq, cache_k, l, pt_flat)
        yp = _ffn(_merge(x1, row(mix_g[l]), o_gla, o_moba, w), *ffn2)
        outs[0].append(k.reshape(batch, seq, MOBA_HEADS, MOBA_HD))
        outs[1].append(v.reshape(batch, seq, MOBA_HEADS, MOBA_HD))
        outs[2].append(s_prompt)

        x1 = _ffn(ys, *ffn1)
        gq, gk, gv, gg, la, _, _ = _gla_proj(x1, row(mix_g[l]), w)
        q, k, v, _ = _moba_proj(x1, row(mix_g[l]), w, *rope_s, n_seq=1, select=False)
        o_gla, s_sample = _gla_sample(gq, gk, la, gv, gg, state_gla[l], norm_g)
        sel = _sample_select(q, page_sums.reshape(n_dec, n_pages * PAGE_SIZE // MOBA_BLOCK, MOBA_W))
        o_moba = _sample_attend(q, k, v, cache_k, cache_v, l, pt_flat, sel, n_pages)
        ys = _ffn(_merge(x1, row(mix_g[l]), o_gla, o_moba, w), *ffn2)
        outs[3].append(k.reshape(n_dec, 1, MOBA_HEADS, MOBA_HD))
        outs[4].append(v.reshape(n_dec, 1, MOBA_HEADS, MOBA_HD))
        outs[5].append(s_sample)

    return (yp.reshape(batch, seq, D_MODEL), ys.reshape(n_dec, 1, D_MODEL),
            jnp.stack(outs[0]), jnp.stack(outs[1]), jnp.stack(outs[2]),
            jnp.stack(outs[3]), jnp.stack(outs[4]), jnp.stack(outs[5]))
```

```python
import functools

import numpy as np
import jax
import jax.numpy as jnp
from jax import lax
from jax.experimental import pallas as pl
from jax.experimental.pallas import tpu as pltpu

F32 = jnp.float32
BF16 = jnp.bfloat16

D_MODEL = 1024
D_FF = 2816
GLA_HEADS = 4
GLA_HDK = 128
GLA_HDV = 256
GLA_DK = GLA_HEADS * GLA_HDK
GLA_DV = GLA_HEADS * GLA_HDV
GLA_RANK = 16
GLA_TAU = 16.0
MOBA_HEADS = 8
MOBA_HD = 128
MOBA_W = MOBA_HEADS * MOBA_HD
MOBA_BLOCK = 256
MOBA_TOPK = 3
ROPE_THETA = 10000.0
EPS = 1e-6
PAGE_SIZE = 128
IN_SPLITS = (GLA_DK, GLA_DK, GLA_DV, GLA_DV, GLA_RANK, MOBA_W, MOBA_W, MOBA_W, D_MODEL, D_MODEL)

LANES = 128
SUBLANES = 8
VMEM_BYTES = 64 * 1024 * 1024

FF_CHUNK = 256
FFN_ROWS = 512
PROJ_ROWS = 512
GLA_CHUNK = 128
MOBA_GROUP = 2
MOBA_KV_BLOCKS = 2
MOBA_VISITS = 8
MASK_BIG = 2.0 ** 100
NEG_INIT = -1.0e38
LOG2_E = 1.4426950408889634


def _vmem_limit(nbytes):
    return int(min(VMEM_BYTES - (4 << 20), max(nbytes, 16 << 20)))


def _resident(shape):
    return pl.BlockSpec(shape, lambda *_: (0,) * len(shape), pipeline_mode=pl.Buffered(1))


def _dot(a, b):
    return jnp.dot(a, b, preferred_element_type=F32)


def _dot_nt(a, b):
    return lax.dot_general(a, b, (((1,), (1,)), ((), ())), preferred_element_type=F32)


def _rms_norm(x, g):
    return x * lax.rsqrt(jnp.mean(x * x, axis=-1, keepdims=True) + EPS) * g


def _log_sigmoid(z):
    return jnp.minimum(z, 0.0) - jnp.log1p(jnp.exp(-jnp.abs(z)))


def _split3(x):
    hi = x.astype(BF16)
    r = x - hi.astype(F32)
    mid = r.astype(BF16)
    lo = (r - mid.astype(F32)).astype(BF16)
    return hi, mid, lo


def _ffn_body(x_ref, g_ref, wg_ref, wu_ref, wd_ref, o_ref):
    x = x_ref[...]
    h = _rms_norm(x, g_ref[...]).astype(BF16)
    acc = jnp.zeros_like(x)
    for c in range(D_FF // FF_CHUNK):
        sl = slice(c * FF_CHUNK, (c + 1) * FF_CHUNK)
        a = _dot(h, wg_ref[:, sl])
        u = _dot(h, wu_ref[:, sl])
        act = (a * jax.nn.sigmoid(a) * u).astype(BF16)
        acc = acc + _dot(act, wd_ref[sl, :])
    o_ref[...] = x + 0.5 * acc


def _ffn(x, g, wg, wu, wd):
    m = x.shape[0]
    tm = min(FFN_ROWS, m)
    row = pl.BlockSpec((tm, D_MODEL), lambda i: (i, 0))
    need = 3 * D_MODEL * D_FF * 2 + 4 * tm * D_MODEL * 4 + 6 * tm * D_MODEL * 4
    return pl.pallas_call(
        _ffn_body,
        grid=(m // tm,),
        in_specs=[row, _resident((1, D_MODEL)), _resident((D_MODEL, D_FF)), _resident((D_MODEL, D_FF)),
                  _resident((D_FF, D_MODEL))],
        out_specs=row,
        out_shape=jax.ShapeDtypeStruct((m, D_MODEL), F32),
        compiler_params=pltpu.CompilerParams(dimension_semantics=("parallel",),
                                             vmem_limit_bytes=_vmem_limit(need)),
        name="swiglu_half_step",
    )(x, g, wg, wu, wd)


def _gla_proj_body(x_ref, g_ref, wgq, wgk, wgv, wgg, wglr, wa2, ba, gq_o, gk_o, gv_o, gg_o, la_o):
    h = _rms_norm(x_ref[...], g_ref[...]).astype(BF16)
    gq_o[...] = _dot(h, wgq[...]) * (GLA_HDK ** -0.5)
    gk_o[...] = _dot(h, wgk[...])
    gv_o[...] = _dot(h, wgv[...]).astype(BF16)
    gg_o[...] = _dot(h, wgg[...]).astype(BF16)
    glr = _dot(h, wglr[...]).astype(BF16)
    la_o[...] = _log_sigmoid(_dot(glr, wa2[...]) + ba[...]) * (1.0 / GLA_TAU)


def _gla_proj(x, mix_g, w):
    m = x.shape[0]
    tm = min(PROJ_ROWS, m)

    def rows(width):
        return pl.BlockSpec((tm, width), lambda i: (i, 0))

    weights = [w["gq"], w["gk"], w["gv"], w["gg"], w["glr"], w["a2"], w["ba"]]
    out_shape = [
        jax.ShapeDtypeStruct((m, GLA_DK), F32), jax.ShapeDtypeStruct((m, GLA_DK), F32),
        jax.ShapeDtypeStruct((m, GLA_DV), BF16), jax.ShapeDtypeStruct((m, GLA_DV), BF16),
        jax.ShapeDtypeStruct((m, GLA_DK), F32),
    ]
    w_bytes = sum(int(a.size) * a.dtype.itemsize for a in weights)
    need = w_bytes + 2 * tm * (3 * GLA_DK * 4 + 2 * GLA_DV * 2) + 2 * tm * D_MODEL * 4 + 6 * tm * GLA_DV * 4
    return pl.pallas_call(
        _gla_proj_body,
        grid=(m // tm,),
        in_specs=[rows(D_MODEL), _resident((1, D_MODEL))] + [_resident(a.shape) for a in weights],
        out_specs=[rows(GLA_DK), rows(GLA_DK), rows(GLA_DV), rows(GLA_DV), rows(GLA_DK)],
        out_shape=out_shape,
        compiler_params=pltpu.CompilerParams(dimension_semantics=("parallel",),
                                             vmem_limit_bytes=_vmem_limit(need)),
        name="gla_input_projection",
    )(x, mix_g, *weights)


def _moba_proj_body(x_ref, g_ref, wmq, wmk, wmv, qg, kg, rc, rs, q_o, k_o, v_o, qe_o, ksum_ref, *,
                    select, tiles_per_seq):
    tile = pl.program_id(0) % tiles_per_seq
    if select:
        @pl.when(tile == 0)
        def _():
            ksum_ref[...] = jnp.zeros_like(ksum_ref)

    h = _rms_norm(x_ref[...], g_ref[...]).astype(BF16)
    mq = _dot(h, wmq[...])
    mk = _dot(h, wmk[...])
    v_o[...] = _dot(h, wmv[...])
    cos = rc[...]
    sin = rs[...]
    for hd in range(MOBA_HEADS):
        sl = slice(hd * MOBA_HD, (hd + 1) * MOBA_HD)
        qh = _rms_norm(mq[:, sl], qg[:, sl])
        q_o[:, sl] = qh * cos + pltpu.roll(qh, MOBA_HD // 2, 1) * sin
        kh = _rms_norm(mk[:, sl], kg[:, sl])
        k_o[:, sl] = kh * cos + pltpu.roll(kh, MOBA_HD // 2, 1) * sin
    if not select:
        qe_o[...] = jnp.zeros_like(qe_o)
        return

    tq = MOBA_BLOCK
    nb = ksum_ref.shape[0]
    eye =(lax.broadcasted_iota(jnp.int32, (tq, tq), 0) == lax.broadcasted_iota(jnp.int32, (tq, tq), 1))
    eye = jnp.where(eye, 1.0, 0.0).astype(BF16)
    for c in range(q_o.shape[0] // tq):
        rows = slice(c * tq, (c + 1) * tq)
        own_i = tile * (q_o.shape[0] // tq) + c
        own = own_i.astype(F32)
        ksum_ref[pl.ds(own_i, 1), :] = jnp.sum(k_o[rows, :], axis=0, keepdims=True)
        gates = []
        for hd in range(MOBA_HEADS):
            sl = slice(hd * MOBA_HD, (hd + 1) * MOBA_HD)
            q = q_o[rows, sl]
            means = ksum_ref[:, sl] * (1.0 / MOBA_BLOCK)
            q_hi = q.astype(BF16)
            q_lo = (q - q_hi.astype(F32)).astype(BF16)
            m_hi = means.astype(BF16)
            m_lo = (means - m_hi.astype(F32)).astype(BF16)
            gates.append(_dot_nt(m_hi, q_hi) + _dot_nt(m_lo, q_hi) + _dot_nt(m_hi, q_lo))
            qe_o[0, hd, rows, :MOBA_HD] = (q * (MOBA_HD ** -0.5 * LOG2_E)).astype(BF16)
        gate = jnp.concatenate(gates, axis=1)
        blk = lax.broadcasted_iota(jnp.int32, gate.shape, 0).astype(F32)
        gate = jnp.where(blk < own, gate, -jnp.inf)
        mask = jnp.where(blk == own, 0.0, -1.0)
        for _ in range(MOBA_TOPK):
            best = jnp.max(gate, axis=0, keepdims=True)
            first = jnp.min(jnp.where(gate == best, blk, float(nb)), axis=0, keepdims=True)
            first = jnp.where(best > -jnp.inf, first, -1.0)
            pick = blk == first
            mask = jnp.where(pick, 0.0, mask)
            gate = jnp.where(pick, -jnp.inf, gate)
        mask = jnp.concatenate([mask, jnp.full((LANES - nb, mask.shape[1]), -1.0, F32)], axis=0).astype(BF16)
        for hd in range(MOBA_HEADS):
            qe_o[0, hd, rows, MOBA_HD:] = _dot_nt(eye, mask[:, hd * tq:(hd + 1) * tq]).astype(BF16)


def _moba_proj(x, mix_g, w, rope_cos, rope_sin, n_seq, select):
    m = x.shape[0]
    seq = m // n_seq
    tm = min(PROJ_ROWS, seq if select else m)
    n_rope = rope_cos.shape[0] // tm
    per_seq = seq // tm
    nb = max(seq // MOBA_BLOCK, SUBLANES)
    rows = pl.BlockSpec((tm, MOBA_W), lambda i: (i, 0))
    rope_spec = pl.BlockSpec((tm, MOBA_HD), lambda i: (i % n_rope, 0))
    weights = [w["mq"], w["mk"], w["mv"], w["qg"], w["kg"]]
    w_bytes = sum(int(a.size) * a.dtype.itemsize for a in weights)
    out_bytes = tm * MOBA_W * 3 * 4 + tm * MOBA_HEADS * 2 * MOBA_HD * 2
    need = w_bytes + 2 * out_bytes + 2 * tm * D_MODEL * 4 + 8 * tm * MOBA_W * 4 + (4 << 20)
    return pl.pallas_call(
        functools.partial(_moba_proj_body, select=select, tiles_per_seq=per_seq),
        grid=(m // tm,),
        in_specs=[pl.BlockSpec((tm, D_MODEL), lambda i: (i, 0)), _resident((1, D_MODEL))]
                 + [_resident(a.shape) for a in weights] + [rope_spec, rope_spec],
        out_specs=[rows, rows, rows,
                   pl.BlockSpec((1, MOBA_HEADS, tm, 2 * MOBA_HD), lambda i: (i // per_seq, 0, i % per_seq, 0))],
        out_shape=[jax.ShapeDtypeStruct((m, MOBA_W), F32)] * 3
                  + [jax.ShapeDtypeStruct((n_seq if select else 1, MOBA_HEADS, seq if select else m, 2 * MOBA_HD), BF16)],
        scratch_shapes=[pltpu.VMEM((nb, MOBA_W), F32)],
        compiler_params=pltpu.CompilerParams(dimension_semantics=("arbitrary",),
                                             vmem_limit_bytes=_vmem_limit(need)),
        name="moba_input_projection",
    )(x, mix_g, *weights, rope_cos, rope_sin)


def _gla_tables():
    c = GLA_CHUNK
    t = np.arange(c)
    le = t[None, :] <= t[:, None]
    gt = t[None, :] > t[:, None]
    masks = []
    s = c // 2
    while s >= 1:
        same = (t // (2 * s))[:, None] == (t // (2 * s))[None, :]
        right = (t % (2 * s)) >= s
        masks.append(same & right[:, None] & ~right[None, :])
        s //= 2
    masks.append(np.eye(c, dtype=bool))
    tail = np.concatenate([gt.T, np.ones((c, c), dtype=bool)], axis=1)
    return le.astype(np.float32), np.stack(masks).astype(np.float32), tail.astype(np.float32)


def _level_reference(b, s):
    c, dk = b.shape
    if s >= SUBLANES:
        blocks = b.reshape(c // (2 * s), 2 * s, dk)
        return jnp.broadcast_to(blocks[:, s - 1:s, :], blocks.shape).reshape(c, dk)
    rows8 = b.reshape(c // SUBLANES, SUBLANES, dk)
    sub = lax.broadcasted_iota(jnp.int32, rows8.shape, 1)
    ref = jnp.broadcast_to(rows8[:, SUBLANES - s - 1:SUBLANES - s, :], rows8.shape)
    for first in range(SUBLANES - 4 * s, -1, -2 * s):
        ref = jnp.where(sub < first + 2 * s, jnp.broadcast_to(rows8[:, first + s - 1:first + s, :], rows8.shape), ref)
    return ref.reshape(c, dk)


def _gla_body(q_ref, k_ref, la_ref, v_ref, gg_ref, ng_ref, le_ref, masks_ref, tail_ref, o_ref, s_ref):
    c = GLA_CHUNK
    n_lvl = masks_ref.shape[0] - 1

    @pl.when(pl.program_id(1) == 0)
    def _():
        s_ref[...] = jnp.zeros_like(s_ref)

    le = le_ref[...]
    tail = tail_ref[...]
    for hd in range(GLA_HEADS):
        ks = slice(hd * GLA_HDK, (hd + 1) * GLA_HDK)
        vs = slice(hd * GLA_HDV, (hd + 1) * GLA_HDV)
        q = q_ref[:, ks]
        k = k_ref[:, ks]
        v = v_ref[:, vs]
        state = s_ref[0, hd]
        b = sum(_dot(le, p) for p in _split3(la_ref[:, ks]))
        et = jnp.exp(sum(_dot(p, tail) for p in _split3(la_ref[:, ks].T)))
        out = _dot((q * jnp.exp(b)).astype(BF16), state.astype(BF16))
        attn = masks_ref[n_lvl] * _dot_nt(q.astype(BF16), k.astype(BF16))
        for lv in range(n_lvl):
            e = jnp.exp(-jnp.abs(b - _level_reference(b, c >> (lv + 1))))
            attn = attn + masks_ref[lv] * _dot_nt((q * e).astype(BF16), (k * e).astype(BF16))
        out = out + _dot(attn.astype(BF16), v)
        k_dec = (k.T * et[:, :c]).astype(BF16)
        decay = et[:, c:]
        s_ref[0, hd] = jnp.concatenate([state[:, :c] * decay, state[:, c:] * decay], axis=1) + _dot(k_dec, v)
        gate = gg_ref[:, vs].astype(F32)
        o_ref[:, vs] = (_rms_norm(out, ng_ref[...]) * (gate * jax.nn.sigmoid(gate))).astype(BF16)


def _gla_prompt(gq, gk, la, gv, gg, norm_g, batch, seq):
    c = GLA_CHUNK
    n_chunks = seq // c
    le, masks, tail = (jnp.asarray(a, BF16 if i != 1 else F32) for i, a in enumerate(_gla_tables()))

    def rows(width):
        return pl.BlockSpec((c, width), lambda b, t: (b * n_chunks + t, 0))

    need = 2 * (3 * c * GLA_DK * 4 + 3 * c * GLA_DV * 2) + 4 * GLA_DK * GLA_HDV * 4 + (8 << 20)
    return pl.pallas_call(
        _gla_body,
        grid=(batch, n_chunks),
        in_specs=[rows(GLA_DK), rows(GLA_DK), rows(GLA_DK), rows(GLA_DV), rows(GLA_DV),
                  _resident((1, GLA_HDV)), _resident(le.shape), _resident(masks.shape), _resident(tail.shape)],
        out_specs=[rows(GLA_DV),
                   pl.BlockSpec((1, GLA_HEADS, GLA_HDK, GLA_HDV), lambda b, t: (b, 0, 0, 0))],
        out_shape=[jax.ShapeDtypeStruct((batch * seq, GLA_DV), BF16),
                   jax.ShapeDtypeStruct((batch, GLA_HEADS, GLA_HDK, GLA_HDV), F32)],
        compiler_params=pltpu.CompilerParams(dimension_semantics=("parallel", "arbitrary"),
                                             vmem_limit_bytes=_vmem_limit(need)),
        name="gla_prompt_chunks",
    )(gq, gk, la, gv, gg, norm_g, le, masks, tail)


def _moba_attend_body(pt_ref, qe_ref, k_ref, v_ref, *refs):
    pages, (o_ref, ksum_ref, acc_ref, m_ref) = refs[:-4], refs[-4:]
    j = pl.program_id(2)
    tb = MOBA_BLOCK
    tkv = k_ref.shape[0]
    n_groups = qe_ref.shape[2] // (tb * MOBA_GROUP)

    @pl.when(j == 0)
    def _():
        acc_ref[...] = jnp.zeros_like(acc_ref)
        m_ref[...] = jnp.full_like(m_ref, NEG_INIT)

    k = k_ref[...].astype(BF16)
    lane = lax.broadcasted_iota(jnp.int32, (tkv, MOBA_HD), 1)
    key_block = j * (tkv // tb) + lax.broadcasted_iota(jnp.int32, (tkv, MOBA_HD), 0) // tb
    k_ext = jnp.concatenate([k, jnp.where(lane == key_block, MASK_BIG, 0.0).astype(BF16)], axis=1)
    v_ext = jnp.concatenate([v_ref[...].astype(BF16), jnp.ones((tkv, MOBA_HD), BF16)], axis=1)

    def update(rows, s):
        m_old = m_ref[rows, :]
        m_new = jnp.maximum(m_old, jnp.max(s, axis=1, keepdims=True))
        p = jnp.exp2(s - jnp.concatenate([m_new] * (tkv // MOBA_HD), axis=1)).astype(BF16)
        alpha = jnp.exp2(m_old - m_new)
        acc_new = jnp.concatenate([alpha, alpha], axis=1) * acc_ref[rows, :] + _dot(p, v_ext)
        m_ref[rows, :] = m_new
        acc_ref[rows, :] = acc_new
        return acc_new

    rows_per_group = tb * MOBA_GROUP

    def group_rows(g):
        return pl.ds(pl.multiple_of(g * rows_per_group, rows_per_group), rows_per_group)

    def scores(g):
        return _dot_nt(qe_ref[0, 0, group_rows(g), :], k_ext)

    first = (j * tkv) // rows_per_group
    n_later = n_groups - 1 - first
    n_extra = n_later % MOBA_VISITS

    def page_sum(page_ref):
        return jnp.sum(jnp.sum(page_ref[...].reshape(4, PAGE_SIZE // 4, MOBA_HEADS, MOBA_HD), axis=1), axis=0)

    def first_groups(extra):
        q_pos = first * rows_per_group + lax.broadcasted_iota(jnp.int32, (rows_per_group, tkv), 0)
        k_pos = j * tkv + lax.broadcasted_iota(jnp.int32, (rows_per_group, tkv), 1)
        ss = [jnp.where(k_pos <= q_pos, scores(first), -MASK_BIG)] + [scores(first + u) for u in range(1, extra + 1)]
        for u, s in enumerate(ss):
            update(group_rows(first + u), s)
        acc = acc_ref[pl.ds(pl.multiple_of(j * tkv, tkv), tkv), :]
        o_ref[...] = (acc[:, :MOBA_HD] / acc[:, MOBA_HD:]).astype(BF16)
        per_block = MOBA_BLOCK // PAGE_SIZE
        for blk in range(len(pages) // per_block):
            tot = page_sum(pages[blk * per_block])
            for p in range(1, per_block):
                tot = tot + page_sum(pages[blk * per_block + p])
            ksum_ref[0, blk] = tot

    for extra in range(MOBA_VISITS):
        pl.when(n_extra == extra)(functools.partial(first_groups, extra))

    def body(i, carry):
        g0 = first + 1 + n_extra + MOBA_VISITS * i
        ss = [scores(g0 + u) for u in range(MOBA_VISITS)]
        for u, s in enumerate(ss):
            update(group_rows(g0 + u), s)
        return carry

    lax.fori_loop(0, n_later // MOBA_VISITS, body, 0)


def _moba_attend(qe, k, v, batch, seq, cache, layer, page_table_flat):
    tkv = MOBA_BLOCK * MOBA_KV_BLOCKS
    nb = seq // tkv
    n_steps = batch * MOBA_HEADS * nb
    per_block = MOBA_BLOCK // PAGE_SIZE
    pages_per_step = page_table_flat.shape[0] // n_steps
    assert pages_per_step * n_steps == page_table_flat.shape[0] and pages_per_step % per_block == 0
    kv = pl.BlockSpec((tkv, MOBA_HD), lambda b, h, j, pt: (b * nb + j, h))

    def step(b, h, j):
        return (b * MOBA_HEADS + h) * nb + j

    def page_spec(i):
        return pl.BlockSpec((None, None, PAGE_SIZE, MOBA_HEADS, MOBA_HD),
                            lambda b, h, j, pt: (layer, pt[step(b, h, j) * pages_per_step + i], 0, 0, 0))

    blocks_per_step = pages_per_step // per_block
    need = (2 * seq * 2 * MOBA_HD * 2 + 3 * seq * MOBA_HD * 4 + 2 * pages_per_step * PAGE_SIZE * MOBA_W * 4
            + (16 << 20))
    out, sums = pl.pallas_call(
        _moba_attend_body,
        grid_spec=pltpu.PrefetchScalarGridSpec(
            num_scalar_prefetch=1,
            grid=(batch, MOBA_HEADS, nb),
            in_specs=[pl.BlockSpec((1, 1, seq, 2 * MOBA_HD), lambda b, h, j, pt: (b, h, 0, 0)), kv, kv]
                     + [page_spec(i) for i in range(pages_per_step)],
            out_specs=[kv, pl.BlockSpec((1, blocks_per_step, MOBA_HEADS, MOBA_HD),
                                        lambda b, h, j, pt: (step(b, h, j), 0, 0, 0))],
            scratch_shapes=[pltpu.VMEM((seq, 2 * MOBA_HD), F32), pltpu.VMEM((seq, MOBA_HD), F32)],
        ),
        out_shape=[jax.ShapeDtypeStruct((batch * seq, MOBA_W), BF16),
                   jax.ShapeDtypeStruct((n_steps, blocks_per_step, MOBA_HEADS, MOBA_HD), F32)],
        compiler_params=pltpu.CompilerParams(dimension_semantics=("parallel", "parallel", "arbitrary"),
                                             vmem_limit_bytes=_vmem_limit(need)),
        name="moba_prompt_attention",
    )(page_table_flat, qe, k, v, *([cache] * pages_per_step))
    return out, sums


def _merge_body(x_ref, g_ref, oa_ref, ob_ref, wga, wgb, wpa, wpb, wo, o_ref):
    x = x_ref[...]
    h = _rms_norm(x, g_ref[...]).astype(BF16)
    mix = jax.nn.sigmoid(_dot(h, wga[...])) * _dot(oa_ref[...], wpa[...])
    mix = mix + jax.nn.sigmoid(_dot(h, wgb[...])) * _dot(ob_ref[...], wpb[...])
    o_ref[...] = x + _dot(mix.astype(BF16), wo[...])


def _merge(x, mix_g, o_gla, o_moba, w):
    m = x.shape[0]
    tm = min(FFN_ROWS, m)
    row32 = pl.BlockSpec((tm, D_MODEL), lambda i: (i, 0))
    sq = _resident((D_MODEL, D_MODEL))
    need = 5 * D_MODEL * D_MODEL * 2 + 4 * tm * D_MODEL * 4 + 4 * tm * D_MODEL * 2 + 6 * tm * D_MODEL * 4
    return pl.pallas_call(
        _merge_body,
        grid=(m // tm,),
        in_specs=[row32, _resident((1, D_MODEL)), row32, row32, sq, sq, sq, sq, sq],
        out_specs=row32,
        out_shape=jax.ShapeDtypeStruct((m, D_MODEL), F32),
        compiler_params=pltpu.CompilerParams(dimension_semantics=("parallel",),
                                             vmem_limit_bytes=_vmem_limit(need)),
        name="gated_merge_projection",
    )(x, mix_g, o_gla, o_moba, w["ga"], w["gb"], w["pa"], w["pb"], w["o"])


def _gla_sample_body(q_ref, k_ref, la_ref, v_ref, gg_ref, s_ref, ng_ref, o_ref, so_ref):
    def column(ref, hd):
        row = ref[0, :, hd * GLA_HDK:(hd + 1) * GLA_HDK]
        col = jnp.broadcast_to(row, (GLA_HDK, GLA_HDK)).T
        return jnp.concatenate([col] * (GLA_HDV // GLA_HDK), axis=1)

    for hd in range(GLA_HEADS):
        vs = slice(hd * GLA_HDV, (hd + 1) * GLA_HDV)
        v = v_ref[0, :, vs].astype(F32)
        s_new = jnp.exp(column(la_ref, hd)) * s_ref[0, hd] + column(k_ref, hd) * v
        so_ref[0, hd] = s_new
        out = jnp.sum(column(q_ref, hd) * s_new, axis=0, keepdims=True)
        gate = gg_ref[0, :, vs].astype(F32)
        o_ref[0, :, vs] = (_rms_norm(out, ng_ref[...]) * (gate * jax.nn.sigmoid(gate))).astype(BF16)


def _gla_sample(gq, gk, la, gv, gg, state, norm_g):
    n = gq.shape[0]
    assert GLA_HDV % GLA_HDK == 0
    key_row = pl.BlockSpec((1, 1, GLA_DK), lambda i: (i, 0, 0))
    val_row = pl.BlockSpec((1, 1, GLA_DV), lambda i: (i, 0, 0))
    st = pl.BlockSpec((1, GLA_HEADS, GLA_HDK, GLA_HDV), lambda i: (i, 0, 0, 0))
    as_row = lambda a: a.reshape(n, 1, a.shape[-1])
    o, s_new = pl.pallas_call(
        _gla_sample_body,
        grid=(n,),
        in_specs=[key_row, key_row, key_row, val_row, val_row, st, _resident((1, GLA_HDV))],
        out_specs=[val_row, st],
        out_shape=[jax.ShapeDtypeStruct((n, 1, GLA_DV), BF16),
                   jax.ShapeDtypeStruct((n, GLA_HEADS, GLA_HDK, GLA_HDV), F32)],
        compiler_params=pltpu.CompilerParams(dimension_semantics=("parallel",),
                                             vmem_limit_bytes=_vmem_limit(16 << 20)),
        name="gla_sample_step",
    )(as_row(gq), as_row(gk), as_row(la), as_row(gv), as_row(gg), state, norm_g)
    return o.reshape(n, GLA_DV), s_new


def _sample_select_body(q_ref, ksum_ref, sel_ref):
    nb = ksum_ref.shape[1]
    q = q_ref[0]
    lane = lax.broadcasted_iota(jnp.int32, (nb, LANES), 1)
    row = lax.broadcasted_iota(jnp.int32, (nb, LANES), 0).astype(F32)
    gate = jnp.full((nb, LANES), -jnp.inf, F32)
    for hd in range(MOBA_HEADS):
        sl = slice(hd * MOBA_HD, (hd + 1) * MOBA_HD)
        g = jnp.sum(ksum_ref[0, :, sl] * (1.0 / MOBA_BLOCK) * q[:, sl], axis=1, keepdims=True)
        gate = jnp.where(lane == hd, g, gate)
    out_row = lax.broadcasted_iota(jnp.int32, (SUBLANES, LANES), 0)
    out = jnp.zeros((SUBLANES, LANES), jnp.int32)
    for r in range(MOBA_TOPK):
        best = jnp.max(gate, axis=0, keepdims=True)
        first = jnp.min(jnp.where(gate == best, row, float(nb)), axis=0, keepdims=True)
        out = jnp.where(out_row == r, first.astype(jnp.int32), out)
        gate = jnp.where(row == first, -jnp.inf, gate)
    sel_ref[0] = out


def _sample_select(q, ksum):
    n, nb = ksum.shape[0], ksum.shape[1]
    sel = pl.pallas_call(
        _sample_select_body,
        grid=(n,),
        in_specs=[pl.BlockSpec((1, 1, MOBA_W), lambda i: (i, 0, 0)),
                  pl.BlockSpec((1, nb, MOBA_W), lambda i: (i, 0, 0))],
        out_specs=pl.BlockSpec((1, SUBLANES, LANES), lambda i: (i, 0, 0)),
        out_shape=jax.ShapeDtypeStruct((n, SUBLANES, LANES), jnp.int32),
        compiler_params=pltpu.CompilerParams(dimension_semantics=("parallel",)),
        name="moba_sample_select",
    )(q.reshape(n, 1, MOBA_W), ksum)
    return jnp.transpose(sel[:, :MOBA_TOPK, :MOBA_HEADS], (0, 2, 1)).reshape(-1)


def _sample_attend_body(pt_ref, sel_ref, q_ref, kn_ref, vn_ref, ck_hbm, cv_hbm, o_ref, k_buf, v_buf, sem,
                        *, layer, n_pages):
    per_block = MOBA_BLOCK // PAGE_SIZE
    n_sel = MOBA_TOPK * per_block
    n_slots = MOBA_HEADS * n_sel
    seq = pl.program_id(0)
    scale = MOBA_HD ** -0.5

    def page_copies(s, hd, i):
        blk = sel_ref[(s * MOBA_HEADS + hd) * MOBA_TOPK + i // per_block]
        page = pt_ref[s * n_pages + blk * per_block + i % per_block]
        slot = (s % 2) * n_slots + hd * n_sel + i
        return (pltpu.make_async_copy(ck_hbm.at[layer, page, :, hd, :], k_buf.at[slot], sem.at[0, slot]),
                pltpu.make_async_copy(cv_hbm.at[layer, page, :, hd, :], v_buf.at[slot], sem.at[1, slot]))

    def start_fetch(s):
        for hd in range(MOBA_HEADS):
            for i in range(n_sel):
                for cp in page_copies(s, hd, i):
                    cp.start()

    @pl.when(seq == 0)
    def _():
        start_fetch(seq)

    @pl.when(seq + 1 < pl.num_programs(0))
    def _():
        start_fetch(seq + 1)

    for hd in range(MOBA_HEADS):
        for i in range(n_sel):
            for cp in page_copies(seq, hd, i):
                cp.wait()

    base = (seq % 2) * n_slots
    for hd in range(MOBA_HEADS):
        sl = slice(hd * MOBA_HD, (hd + 1) * MOBA_HD)
        q = q_ref[0, :, sl]
        s_new = jnp.sum(kn_ref[0, :, sl] * q, axis=1, keepdims=True) * scale
        scores = [jnp.sum(k_buf[base + hd * n_sel + i] * q, axis=1, keepdims=True) * scale for i in range(n_sel)]
        top = s_new
        for s in scores:
            top = jnp.maximum(top, jnp.max(s, axis=0, keepdims=True))
        p_new = jnp.exp(s_new - top)
        denom = p_new
        acc = p_new * vn_ref[0, :, sl]
        for i, s in enumerate(scores):
            p = jnp.exp(s - top)
            denom = denom + jnp.sum(p, axis=0, keepdims=True)
            acc = acc + jnp.sum(p * v_buf[base + hd * n_sel + i], axis=0, keepdims=True)
        o_ref[0, :, sl] = (acc / denom).astype(BF16)


def _sample_attend(q, k_new, v_new, cache_k, cache_v, layer, page_table_flat, sel_flat, n_pages):
    n = q.shape[0]
    n_slots = MOBA_HEADS * MOBA_TOPK * (MOBA_BLOCK // PAGE_SIZE)
    tok = pl.BlockSpec((1, 1, MOBA_W), lambda s, pt, sel: (s, 0, 0))
    hbm = pl.BlockSpec(memory_space=pl.ANY)
    as_tok = lambda a: a.reshape(n, 1, MOBA_W)
    out = pl.pallas_call(
        functools.partial(_sample_attend_body, layer=layer, n_pages=n_pages),
        grid_spec=pltpu.PrefetchScalarGridSpec(
            num_scalar_prefetch=2,
            grid=(n,),
            in_specs=[tok, tok, tok, hbm, hbm],
            out_specs=tok,
            scratch_shapes=[pltpu.VMEM((2 * n_slots, PAGE_SIZE, MOBA_HD), F32),
                            pltpu.VMEM((2 * n_slots, PAGE_SIZE, MOBA_HD), F32),
                            pltpu.SemaphoreType.DMA((2, 2 * n_slots))],
        ),
        out_shape=jax.ShapeDtypeStruct((n, 1, MOBA_W), BF16),
        compiler_params=pltpu.CompilerParams(dimension_semantics=("arbitrary",),
                                             vmem_limit_bytes=_vmem_limit(5 * n_slots * PAGE_SIZE * MOBA_HD * 4)),
        name="moba_sample_attention",
    )(page_table_flat, sel_flat, as_tok(q), as_tok(k_new), as_tok(v_new), cache_k, cache_v)
    return out.reshape(n, MOBA_W)


def _rope_tables(pos):
    half = MOBA_HD // 2
    inv = ROPE_THETA ** (-jnp.arange(half, dtype=F32) / half)
    ang = pos.astype(F32)[:, None] * inv[None, :]
    cos, sin = jnp.cos(ang), jnp.sin(ang)
    return jnp.concatenate([cos, cos], axis=1), jnp.concatenate([-sin, sin], axis=1)


def _layer_weights(w_in, w_a2, b_a, q_norm_g, k_norm_g, w_pa, w_pb, w_o):
    offs = np.concatenate([[0], np.cumsum(IN_SPLITS)])
    gq, gk, gv, gg, glr, mq, mk, mv, ga, gb = (w_in[:, offs[i]:offs[i + 1]] for i in range(len(IN_SPLITS)))
    glr = jnp.pad(glr, ((0, 0), (0, LANES - GLA_RANK)))
    a2 = jnp.pad(w_a2, ((0, LANES - GLA_RANK), (0, 0)))
    b16 = lambda a: a.astype(BF16)
    return dict(
        gq=b16(gq), gk=b16(gk), gv=b16(gv), gg=b16(gg), glr=b16(glr), a2=b16(a2), ba=b_a.reshape(1, GLA_DK),
        mq=b16(mq), mk=b16(mk), mv=b16(mv),
        qg=jnp.tile(q_norm_g, MOBA_HEADS).reshape(1, MOBA_W), kg=jnp.tile(k_norm_g, MOBA_HEADS).reshape(1, MOBA_W),
        ga=b16(ga), gb=b16(gb), pa=b16(w_pa), pb=b16(w_pb), o=b16(w_o))


def kernel(x_prompt, x_sample, cache_k, cache_v, state_gla, page_table, ffn1_g, ffn1_wg, ffn1_wu, ffn1_wd, mix_g, w_in, w_a2, b_a, gla_norm_g, q_norm_g, k_norm_g, w_pa, w_pb, w_o, ffn2_g, ffn2_wg, ffn2_wu, ffn2_wd):
    batch, seq, _ = x_prompt.shape
    n_dec, dec_seq, _ = x_sample.shape
    n_pages = page_table.shape[1]
    depth = w_in.shape[0]
    past_len = n_pages * PAGE_SIZE
    assert dec_seq == 1 and seq % MOBA_BLOCK == 0 and past_len % MOBA_BLOCK == 0
    assert past_len // MOBA_BLOCK >= MOBA_TOPK
    assert (seq // MOBA_BLOCK) % MOBA_GROUP == 0 and MOBA_GROUP % MOBA_KV_BLOCKS == 0

    yp = x_prompt.reshape(batch * seq, D_MODEL)
    ys = x_sample.reshape(n_dec, D_MODEL)
    rope_p = _rope_tables(jnp.arange(seq, dtype=jnp.int32))
    rope_s = _rope_tables(jnp.full((n_dec,), past_len, jnp.int32))
    pt_flat = page_table.reshape(-1)
    row = lambda a: a.reshape(1, -1)
    outs = [[] for _ in range(6)]
    for l in range(depth):
        w = _layer_weights(w_in[l], w_a2[l], b_a[l], q_norm_g[l], k_norm_g[l], w_pa[l], w_pb[l], w_o[l])
        ffn1 = (row(ffn1_g[l]), ffn1_wg[l].astype(BF16), ffn1_wu[l].astype(BF16), ffn1_wd[l].astype(BF16))
        ffn2 = (row(ffn2_g[l]), ffn2_wg[l].astype(BF16), ffn2_wu[l].astype(BF16), ffn2_wd[l].astype(BF16))
        norm_g = row(gla_norm_g[l])

        x1 = _ffn(yp, *ffn1)
        gq, gk, gv, gg, la = _gla_proj(x1, row(mix_g[l]), w)
        o_gla, s_prompt = _gla_prompt(gq, gk, la, gv, gg, norm_g, batch, seq)
        _, k, v, qe = _moba_proj(x1, row(mix_g[l]), w, *rope_p, n_seq=batch, select=True)
        o_moba, page_sums = _moba_attend(qe, k, v, batch, seq, cache_k, l, pt_flat)
        yp = _ffn(_merge(x1, row(mix_g[l]), o_gla, o_moba, w), *ffn2)
        outs[0].append(k.reshape(batch, seq, MOBA_HEADS, MOBA_HD))
        outs[1].append(v.reshape(batch, seq, MOBA_HEADS, MOBA_HD))
        outs[2].append(s_prompt)

        x1 = _ffn(ys, *ffn1)
        gq, gk, gv, gg, la = _gla_proj(x1, row(mix_g[l]), w)
        q, k, v, _ = _moba_proj(x1, row(mix_g[l]), w, *rope_s, n_seq=1, select=False)
        o_gla, s_sample = _gla_sample(gq, gk, la, gv, gg, state_gla[l], norm_g)
        sel = _sample_select(q, page_sums.reshape(n_dec, n_pages * PAGE_SIZE // MOBA_BLOCK, MOBA_W))
        o_moba = _sample_attend(q, k, v, cache_k, cache_v, l, pt_flat, sel, n_pages)
        ys = _ffn(_merge(x1, row(mix_g[l]), o_gla, o_moba, w), *ffn2)
        outs[3].append(k.reshape(n_dec, 1, MOBA_HEADS, MOBA_HD))
        outs[4].append(v.reshape(n_dec, 1, MOBA_HEADS, MOBA_HD))
        outs[5].append(s_sample)

    return (yp.reshape(batch, seq, D_MODEL), ys.reshape(n_dec, 1, D_MODEL),
            jnp.stack(outs[0]), jnp.stack(outs[1]), jnp.stack(outs[2]),
            jnp.stack(outs[3]), jnp.stack(outs[4]), jnp.stack(outs[5]))
```

```python
import functools

import numpy as np
import jax
import jax.numpy as jnp
from jax import lax
from jax.experimental import pallas as pl
from jax.experimental.pallas import tpu as pltpu

F32 = jnp.float32
BF16 = jnp.bfloat16

D_MODEL = 1024
D_FF = 2816
GLA_HEADS = 4
GLA_HDK = 128
GLA_HDV = 256
GLA_DK = GLA_HEADS * GLA_HDK
GLA_DV = GLA_HEADS * GLA_HDV
GLA_RANK = 16
GLA_TAU = 16.0
MOBA_HEADS = 8
MOBA_HD = 128
MOBA_W = MOBA_HEADS * MOBA_HD
MOBA_BLOCK = 256
MOBA_TOPK = 3
ROPE_THETA = 10000.0
EPS = 1e-6
PAGE_SIZE = 128
IN_SPLITS = (GLA_DK, GLA_DK, GLA_DV, GLA_DV, GLA_RANK, MOBA_W, MOBA_W, MOBA_W, D_MODEL, D_MODEL)

LANES = 128
SUBLANES = 8
VMEM_BYTES = 64 * 1024 * 1024

FF_CHUNK = 256
FFN_ROWS = 512
PROJ_ROWS = 512
GLA_CHUNK = 128
MOBA_GROUP = 4
MOBA_KV_BLOCKS = 4
MOBA_VISITS = 3
MASK_BIG = 2.0 ** 100
NEG_INIT = -1.0e38
LOG2_E = 1.4426950408889634


def _vmem_limit(nbytes):
    return int(min(VMEM_BYTES - (4 << 20), max(nbytes, 16 << 20)))


def _resident(shape):
    return pl.BlockSpec(shape, lambda *_: (0,) * len(shape), pipeline_mode=pl.Buffered(1))


def _dot(a, b):
    return jnp.dot(a, b, preferred_element_type=F32)


def _dot_nt(a, b):
    return lax.dot_general(a, b, (((1,), (1,)), ((), ())), preferred_element_type=F32)


def _rms_norm(x, g):
    return x * lax.rsqrt(jnp.mean(x * x, axis=-1, keepdims=True) + EPS) * g


def _log_sigmoid(z):
    return jnp.minimum(z, 0.0) - jnp.log1p(jnp.exp(-jnp.abs(z)))


def _split3(x):
    hi = x.astype(BF16)
    r = x - hi.astype(F32)
    mid = r.astype(BF16)
    lo = (r - mid.astype(F32)).astype(BF16)
    return hi, mid, lo


def _ffn_body(x_ref, g_ref, wg_ref, wu_ref, wd_ref, o_ref):
    x = x_ref[...]
    h = _rms_norm(x, g_ref[...]).astype(BF16)
    acc = jnp.zeros_like(x)
    for c in range(D_FF // FF_CHUNK):
        sl = slice(c * FF_CHUNK, (c + 1) * FF_CHUNK)
        a = _dot(h, wg_ref[:, sl])
        u = _dot(h, wu_ref[:, sl])
        act = (a * jax.nn.sigmoid(a) * u).astype(BF16)
        acc = acc + _dot(act, wd_ref[sl, :])
    o_ref[...] = x + 0.5 * acc


def _ffn(x, g, wg, wu, wd):
    m = x.shape[0]
    tm = min(FFN_ROWS, m)
    row = pl.BlockSpec((tm, D_MODEL), lambda i: (i, 0))
    need = 3 * D_MODEL * D_FF * 2 + 4 * tm * D_MODEL * 4 + 6 * tm * D_MODEL * 4
    return pl.pallas_call(
        _ffn_body,
        grid=(m // tm,),
        in_specs=[row, _resident((1, D_MODEL)), _resident((D_MODEL, D_FF)), _resident((D_MODEL, D_FF)),
                  _resident((D_FF, D_MODEL))],
        out_specs=row,
        out_shape=jax.ShapeDtypeStruct((m, D_MODEL), F32),
        compiler_params=pltpu.CompilerParams(dimension_semantics=("parallel",),
                                             vmem_limit_bytes=_vmem_limit(need)),
        name="swiglu_half_step",
    )(x, g, wg, wu, wd)


def _gla_proj_body(x_ref, g_ref, wgq, wgk, wgv, wgg, wglr, wa2, ba, gq_o, gk_o, gv_o, gg_o, la_o):
    h = _rms_norm(x_ref[...], g_ref[...]).astype(BF16)
    gq_o[...] = _dot(h, wgq[...]) * (GLA_HDK ** -0.5)
    gk_o[...] = _dot(h, wgk[...])
    gv_o[...] = _dot(h, wgv[...]).astype(BF16)
    gg_o[...] = _dot(h, wgg[...]).astype(BF16)
    glr = _dot(h, wglr[...]).astype(BF16)
    la_o[...] = _log_sigmoid(_dot(glr, wa2[...]) + ba[...]) * (1.0 / GLA_TAU)


def _gla_proj(x, mix_g, w):
    m = x.shape[0]
    tm = min(PROJ_ROWS, m)

    def rows(width):
        return pl.BlockSpec((tm, width), lambda i: (i, 0))

    weights = [w["gq"], w["gk"], w["gv"], w["gg"], w["glr"], w["a2"], w["ba"]]
    out_shape = [
        jax.ShapeDtypeStruct((m, GLA_DK), F32), jax.ShapeDtypeStruct((m, GLA_DK), F32),
        jax.ShapeDtypeStruct((m, GLA_DV), BF16), jax.ShapeDtypeStruct((m, GLA_DV), BF16),
        jax.ShapeDtypeStruct((m, GLA_DK), F32),
    ]
    w_bytes = sum(int(a.size) * a.dtype.itemsize for a in weights)
    need = w_bytes + 2 * tm * (3 * GLA_DK * 4 + 2 * GLA_DV * 2) + 2 * tm * D_MODEL * 4 + 6 * tm * GLA_DV * 4
    return pl.pallas_call(
        _gla_proj_body,
        grid=(m // tm,),
        in_specs=[rows(D_MODEL), _resident((1, D_MODEL))] + [_resident(a.shape) for a in weights],
        out_specs=[rows(GLA_DK), rows(GLA_DK), rows(GLA_DV), rows(GLA_DV), rows(GLA_DK)],
        out_shape=out_shape,
        compiler_params=pltpu.CompilerParams(dimension_semantics=("parallel",),
                                             vmem_limit_bytes=_vmem_limit(need)),
        name="gla_input_projection",
    )(x, mix_g, *weights)


def _moba_proj_body(x_ref, g_ref, wmq, wmk, wmv, qg, kg, rc, rs, q_o, k_o, v_o, qe_o, ksum_ref, *,
                    select, tiles_per_seq):
    tile = pl.program_id(0) % tiles_per_seq
    if select:
        @pl.when(tile == 0)
        def _():
            ksum_ref[...] = jnp.zeros_like(ksum_ref)

    h = _rms_norm(x_ref[...], g_ref[...]).astype(BF16)
    mq = _dot(h, wmq[...])
    mk = _dot(h, wmk[...])
    v_o[...] = _dot(h, wmv[...])
    cos = rc[...]
    sin = rs[...]
    for hd in range(MOBA_HEADS):
        sl = slice(hd * MOBA_HD, (hd + 1) * MOBA_HD)
        qh = _rms_norm(mq[:, sl], qg[:, sl])
        q_o[:, sl] = qh * cos + pltpu.roll(qh, MOBA_HD // 2, 1) * sin
        kh = _rms_norm(mk[:, sl], kg[:, sl])
        k_o[:, sl] = kh * cos + pltpu.roll(kh, MOBA_HD // 2, 1) * sin
    if not select:
        qe_o[...] = jnp.zeros_like(qe_o)
        return

    tq = MOBA_BLOCK
    nb = ksum_ref.shape[0]
    eye =(lax.broadcasted_iota(jnp.int32, (tq, tq), 0) == lax.broadcasted_iota(jnp.int32, (tq, tq), 1))
    eye = jnp.where(eye, 1.0, 0.0).astype(BF16)
    for c in range(q_o.shape[0] // tq):
        rows = slice(c * tq, (c + 1) * tq)
        own_i = tile * (q_o.shape[0] // tq) + c
        own = own_i.astype(F32)
        ksum_ref[pl.ds(own_i, 1), :] = jnp.sum(k_o[rows, :], axis=0, keepdims=True)
        gates = []
        for hd in range(MOBA_HEADS):
            sl = slice(hd * MOBA_HD, (hd + 1) * MOBA_HD)
            q = q_o[rows, sl]
            means = ksum_ref[:, sl] * (1.0 / MOBA_BLOCK)
            q_hi = q.astype(BF16)
            q_lo = (q - q_hi.astype(F32)).astype(BF16)
            m_hi = means.astype(BF16)
            m_lo = (means - m_hi.astype(F32)).astype(BF16)
            gates.append(_dot_nt(m_hi, q_hi) + _dot_nt(m_lo, q_hi) + _dot_nt(m_hi, q_lo))
            qe_o[0, hd, rows, :MOBA_HD] = (q * (MOBA_HD ** -0.5 * LOG2_E)).astype(BF16)
        gate = jnp.concatenate(gates, axis=1)
        blk = lax.broadcasted_iota(jnp.int32, gate.shape, 0).astype(F32)
        gate = jnp.where(blk < own, gate, -jnp.inf)
        mask = jnp.where(blk == own, 0.0, -1.0)
        for _ in range(MOBA_TOPK):
            best = jnp.max(gate, axis=0, keepdims=True)
            first = jnp.min(jnp.where(gate == best, blk, float(nb)), axis=0, keepdims=True)
            first = jnp.where(best > -jnp.inf, first, -1.0)
            pick = blk == first
            mask = jnp.where(pick, 0.0, mask)
            gate = jnp.where(pick, -jnp.inf, gate)
        mask = jnp.concatenate([mask, jnp.full((LANES - nb, mask.shape[1]), -1.0, F32)], axis=0).astype(BF16)
        for hd in range(MOBA_HEADS):
            qe_o[0, hd, rows, MOBA_HD:] = _dot_nt(eye, mask[:, hd * tq:(hd + 1) * tq]).astype(BF16)


def _moba_proj(x, mix_g, w, rope_cos, rope_sin, n_seq, select):
    m = x.shape[0]
    seq = m // n_seq
    tm = min(PROJ_ROWS, seq if select else m)
    n_rope = rope_cos.shape[0] // tm
    per_seq = seq // tm
    nb = max(seq // MOBA_BLOCK, SUBLANES)
    rows = pl.BlockSpec((tm, MOBA_W), lambda i: (i, 0))
    rope_spec = pl.BlockSpec((tm, MOBA_HD), lambda i: (i % n_rope, 0))
    weights = [w["mq"], w["mk"], w["mv"], w["qg"], w["kg"]]
    w_bytes = sum(int(a.size) * a.dtype.itemsize for a in weights)
    out_bytes = tm * MOBA_W * 3 * 4 + tm * MOBA_HEADS * 2 * MOBA_HD * 2
    need = w_bytes + 2 * out_bytes + 2 * tm * D_MODEL * 4 + 8 * tm * MOBA_W * 4 + (4 << 20)
    return pl.pallas_call(
        functools.partial(_moba_proj_body, select=select, tiles_per_seq=per_seq),
        grid=(m // tm,),
        in_specs=[pl.BlockSpec((tm, D_MODEL), lambda i: (i, 0)), _resident((1, D_MODEL))]
                 + [_resident(a.shape) for a in weights] + [rope_spec, rope_spec],
        out_specs=[rows, rows, rows,
                   pl.BlockSpec((1, MOBA_HEADS, tm, 2 * MOBA_HD), lambda i: (i // per_seq, 0, i % per_seq, 0))],
        out_shape=[jax.ShapeDtypeStruct((m, MOBA_W), F32)] * 3
                  + [jax.ShapeDtypeStruct((n_seq if select else 1, MOBA_HEADS, seq if select else m, 2 * MOBA_HD), BF16)],
        scratch_shapes=[pltpu.VMEM((nb, MOBA_W), F32)],
        compiler_params=pltpu.CompilerParams(dimension_semantics=("arbitrary",),
                                             vmem_limit_bytes=_vmem_limit(need)),
        name="moba_input_projection",
    )(x, mix_g, *weights, rope_cos, rope_sin)


def _gla_tables():
    c = GLA_CHUNK
    t = np.arange(c)
    le = t[None, :] <= t[:, None]
    gt = t[None, :] > t[:, None]
    masks = []
    s = c // 2
    while s >= 1:
        same = (t // (2 * s))[:, None] == (t // (2 * s))[None, :]
        right = (t % (2 * s)) >= s
        masks.append(same & right[:, None] & ~right[None, :])
        s //= 2
    masks.append(np.eye(c, dtype=bool))
    tail = np.concatenate([gt.T, np.ones((c, c), dtype=bool)], axis=1)
    return le.astype(np.float32), np.stack(masks).astype(np.float32), tail.astype(np.float32)


def _level_reference(b, s):
    c, dk = b.shape
    if s >= SUBLANES:
        blocks = b.reshape(c // (2 * s), 2 * s, dk)
        return jnp.broadcast_to(blocks[:, s - 1:s, :], blocks.shape).reshape(c, dk)
    rows8 = b.reshape(c // SUBLANES, SUBLANES, dk)
    sub = lax.broadcasted_iota(jnp.int32, rows8.shape, 1)
    ref = jnp.broadcast_to(rows8[:, SUBLANES - s - 1:SUBLANES - s, :], rows8.shape)
    for first in range(SUBLANES - 4 * s, -1, -2 * s):
        ref = jnp.where(sub < first + 2 * s, jnp.broadcast_to(rows8[:, first + s - 1:first + s, :], rows8.shape), ref)
    return ref.reshape(c, dk)


def _gla_body(q_ref, k_ref, la_ref, v_ref, gg_ref, ng_ref, le_ref, masks_ref, tail_ref, o_ref, s_ref):
    c = GLA_CHUNK
    n_lvl = masks_ref.shape[0] - 1

    @pl.when(pl.program_id(1) == 0)
    def _():
        s_ref[...] = jnp.zeros_like(s_ref)

    le = le_ref[...]
    tail = tail_ref[...]
    for hd in range(GLA_HEADS):
        ks = slice(hd * GLA_HDK, (hd + 1) * GLA_HDK)
        vs = slice(hd * GLA_HDV, (hd + 1) * GLA_HDV)
        q = q_ref[:, ks]
        k = k_ref[:, ks]
        v = v_ref[:, vs]
        state = s_ref[0, hd]
        b = sum(_dot(le, p) for p in _split3(la_ref[:, ks]))
        et = jnp.exp(sum(_dot(p, tail) for p in _split3(la_ref[:, ks].T)))
        out = _dot((q * jnp.exp(b)).astype(BF16), state.astype(BF16))
        attn = masks_ref[n_lvl] * _dot_nt(q.astype(BF16), k.astype(BF16))
        for lv in range(n_lvl):
            e = jnp.exp(-jnp.abs(b - _level_reference(b, c >> (lv + 1))))
            attn = attn + masks_ref[lv] * _dot_nt((q * e).astype(BF16), (k * e).astype(BF16))
        out = out + _dot(attn.astype(BF16), v)
        k_dec = (k.T * et[:, :c]).astype(BF16)
        decay = et[:, c:]
        s_ref[0, hd] = jnp.concatenate([state[:, :c] * decay, state[:, c:] * decay], axis=1) + _dot(k_dec, v)
        gate = gg_ref[:, vs].astype(F32)
        o_ref[:, vs] = (_rms_norm(out, ng_ref[...]) * (gate * jax.nn.sigmoid(gate))).astype(BF16)


def _gla_prompt(gq, gk, la, gv, gg, norm_g, batch, seq):
    c = GLA_CHUNK
    n_chunks = seq // c
    le, masks, tail = (jnp.asarray(a, BF16 if i != 1 else F32) for i, a in enumerate(_gla_tables()))

    def rows(width):
        return pl.BlockSpec((c, width), lambda b, t: (b * n_chunks + t, 0))

    need = 2 * (3 * c * GLA_DK * 4 + 3 * c * GLA_DV * 2) + 4 * GLA_DK * GLA_HDV * 4 + (8 << 20)
    return pl.pallas_call(
        _gla_body,
        grid=(batch, n_chunks),
        in_specs=[rows(GLA_DK), rows(GLA_DK), rows(GLA_DK), rows(GLA_DV), rows(GLA_DV),
                  _resident((1, GLA_HDV)), _resident(le.shape), _resident(masks.shape), _resident(tail.shape)],
        out_specs=[rows(GLA_DV),
                   pl.BlockSpec((1, GLA_HEADS, GLA_HDK, GLA_HDV), lambda b, t: (b, 0, 0, 0))],
        out_shape=[jax.ShapeDtypeStruct((batch * seq, GLA_DV), BF16),
                   jax.ShapeDtypeStruct((batch, GLA_HEADS, GLA_HDK, GLA_HDV), F32)],
        compiler_params=pltpu.CompilerParams(dimension_semantics=("parallel", "arbitrary"),
                                             vmem_limit_bytes=_vmem_limit(need)),
        name="gla_prompt_chunks",
    )(gq, gk, la, gv, gg, norm_g, le, masks, tail)


def _moba_attend_body(pt_ref, qe_ref, k_ref, v_ref, *refs):
    pages, (o_ref, ksum_ref, acc_ref, m_ref) = refs[:-4], refs[-4:]
    j = pl.program_id(2)
    tb = MOBA_BLOCK
    tkv = k_ref.shape[0]
    n_groups = qe_ref.shape[2] // (tb * MOBA_GROUP)

    @pl.when(j == 0)
    def _():
        acc_ref[...] = jnp.zeros_like(acc_ref)
        m_ref[...] = jnp.full_like(m_ref, NEG_INIT)

    k = k_ref[...].astype(BF16)
    lane = lax.broadcasted_iota(jnp.int32, (tkv, MOBA_HD), 1)
    key_block = j * (tkv // tb) + lax.broadcasted_iota(jnp.int32, (tkv, MOBA_HD), 0) // tb
    k_ext = jnp.concatenate([k, jnp.where(lane == key_block, MASK_BIG, 0.0).astype(BF16)], axis=1)
    v_ext = jnp.concatenate([v_ref[...].astype(BF16), jnp.ones((tkv, MOBA_HD), BF16)], axis=1)

    def update(rows, s):
        m_old = m_ref[rows, :]
        m_new = jnp.maximum(m_old, jnp.max(s, axis=1, keepdims=True))
        p = jnp.exp2(s - jnp.concatenate([m_new] * (tkv // MOBA_HD), axis=1)).astype(BF16)
        alpha = jnp.exp2(m_old - m_new)
        acc_new = jnp.concatenate([alpha, alpha], axis=1) * acc_ref[rows, :] + _dot(p, v_ext)
        m_ref[rows, :] = m_new
        acc_ref[rows, :] = acc_new
        return acc_new

    rows_per_group = tb * MOBA_GROUP

    def group_rows(g):
        return pl.ds(pl.multiple_of(g * rows_per_group, rows_per_group), rows_per_group)

    def scores(g):
        return _dot_nt(qe_ref[0, 0, group_rows(g), :], k_ext)

    first = (j * tkv) // rows_per_group
    n_later = n_groups - 1 - first
    n_extra = n_later % MOBA_VISITS

    def page_sum(page_ref):
        return jnp.sum(jnp.sum(page_ref[...].reshape(4, PAGE_SIZE // 4, MOBA_HEADS, MOBA_HD), axis=1), axis=0)

    def first_groups(extra):
        s_own = scores(first)
        lower = (lax.broadcasted_iota(jnp.int32, (tb, tb), 1) <= lax.broadcasted_iota(jnp.int32, (tb, tb), 0))
        tiles = []
        for t in range(tkv // tb):
            r, c = slice(t * tb, (t + 1) * tb), slice(t * tb, (t + 1) * tb)
            diag = jnp.where(lower, s_own[r, c], -MASK_BIG)
            parts = ([s_own[r, :t * tb]] if t else []) + [diag] + ([s_own[r, (t + 1) * tb:]] if (t + 1) * tb < tkv else [])
            tiles.append(jnp.concatenate(parts, axis=1))
        ss = [jnp.concatenate(tiles, axis=0)] + [scores(first + u) for u in range(1, extra + 1)]
        for u, s in enumerate(ss):
            update(group_rows(first + u), s)
        acc = acc_ref[pl.ds(pl.multiple_of(j * tkv, tkv), tkv), :]
        o_ref[...] = (acc[:, :MOBA_HD] / acc[:, MOBA_HD:]).astype(BF16)
        per_block = MOBA_BLOCK // PAGE_SIZE
        for blk in range(len(pages) // per_block):
            tot = page_sum(pages[blk * per_block])
            for p in range(1, per_block):
                tot = tot + page_sum(pages[blk * per_block + p])
            ksum_ref[0, blk] = tot

    for extra in range(MOBA_VISITS):
        pl.when(n_extra == extra)(functools.partial(first_groups, extra))

    def body(i, carry):
        g0 = first + 1 + n_extra + MOBA_VISITS * i
        ss = [scores(g0 + u) for u in range(MOBA_VISITS)]
        for u, s in enumerate(ss):
            update(group_rows(g0 + u), s)
        return carry

    lax.fori_loop(0, n_later // MOBA_VISITS, body, 0)


def _moba_attend(qe, k, v, batch, seq, cache, layer, page_table_flat):
    tkv = MOBA_BLOCK * MOBA_KV_BLOCKS
    nb = seq // tkv
    n_steps = batch * MOBA_HEADS * nb
    per_block = MOBA_BLOCK // PAGE_SIZE
    pages_per_step = page_table_flat.shape[0] // n_steps
    assert pages_per_step * n_steps == page_table_flat.shape[0] and pages_per_step % per_block == 0
    kv = pl.BlockSpec((tkv, MOBA_HD), lambda b, h, j, pt: (b * nb + j, h))

    def step(b, h, j):
        return (b * MOBA_HEADS + h) * nb + j

    def page_spec(i):
        return pl.BlockSpec((None, None, PAGE_SIZE, MOBA_HEADS, MOBA_HD),
                            lambda b, h, j, pt: (layer, pt[step(b, h, j) * pages_per_step + i], 0, 0, 0))

    blocks_per_step = pages_per_step // per_block
    need = (2 * seq * 2 * MOBA_HD * 2 + 3 * seq * MOBA_HD * 4 + 2 * pages_per_step * PAGE_SIZE * MOBA_W * 4
            + (16 << 20))
    out, sums = pl.pallas_call(
        _moba_attend_body,
        grid_spec=pltpu.PrefetchScalarGridSpec(
            num_scalar_prefetch=1,
            grid=(batch, MOBA_HEADS, nb),
            in_specs=[pl.BlockSpec((1, 1, seq, 2 * MOBA_HD), lambda b, h, j, pt: (b, h, 0, 0)), kv, kv]
                     + [page_spec(i) for i in range(pages_per_step)],
            out_specs=[kv, pl.BlockSpec((1, blocks_per_step, MOBA_HEADS, MOBA_HD),
                                        lambda b, h, j, pt: (step(b, h, j), 0, 0, 0))],
            scratch_shapes=[pltpu.VMEM((seq, 2 * MOBA_HD), F32), pltpu.VMEM((seq, MOBA_HD), F32)],
        ),
        out_shape=[jax.ShapeDtypeStruct((batch * seq, MOBA_W), BF16),
                   jax.ShapeDtypeStruct((n_steps, blocks_per_step, MOBA_HEADS, MOBA_HD), F32)],
        compiler_params=pltpu.CompilerParams(dimension_semantics=("parallel", "parallel", "arbitrary"),
                                             vmem_limit_bytes=_vmem_limit(need)),
        name="moba_prompt_attention",
    )(page_table_flat, qe, k, v, *([cache] * pages_per_step))
    return out, sums


def _merge_body(x_ref, g_ref, oa_ref, ob_ref, wga, wgb, wpa, wpb, wo, o_ref):
    x = x_ref[...]
    h = _rms_norm(x, g_ref[...]).astype(BF16)
    mix = jax.nn.sigmoid(_dot(h, wga[...])) * _dot(oa_ref[...], wpa[...])
    mix = mix + jax.nn.sigmoid(_dot(h, wgb[...])) * _dot(ob_ref[...], wpb[...])
    o_ref[...] = x + _dot(mix.astype(BF16), wo[...])


def _merge(x, mix_g, o_gla, o_moba, w):
    m = x.shape[0]
    tm = min(FFN_ROWS, m)
    row32 = pl.BlockSpec((tm, D_MODEL), lambda i: (i, 0))
    sq = _resident((D_MODEL, D_MODEL))
    need = 5 * D_MODEL * D_MODEL * 2 + 4 * tm * D_MODEL * 4 + 4 * tm * D_MODEL * 2 + 6 * tm * D_MODEL * 4
    return pl.pallas_call(
        _merge_body,
        grid=(m // tm,),
        in_specs=[row32, _resident((1, D_MODEL)), row32, row32, sq, sq, sq, sq, sq],
        out_specs=row32,
        out_shape=jax.ShapeDtypeStruct((m, D_MODEL), F32),
        compiler_params=pltpu.CompilerParams(dimension_semantics=("parallel",),
                                             vmem_limit_bytes=_vmem_limit(need)),
        name="gated_merge_projection",
    )(x, mix_g, o_gla, o_moba, w["ga"], w["gb"], w["pa"], w["pb"], w["o"])


def _gla_sample_body(q_ref, k_ref, la_ref, v_ref, gg_ref, s_ref, ng_ref, o_ref, so_ref):
    def column(ref, hd):
        row = ref[0, :, hd * GLA_HDK:(hd + 1) * GLA_HDK]
        col = jnp.broadcast_to(row, (GLA_HDK, GLA_HDK)).T
        return jnp.concatenate([col] * (GLA_HDV // GLA_HDK), axis=1)

    for hd in range(GLA_HEADS):
        vs = slice(hd * GLA_HDV, (hd + 1) * GLA_HDV)
        v = v_ref[0, :, vs].astype(F32)
        s_new = jnp.exp(column(la_ref, hd)) * s_ref[0, hd] + column(k_ref, hd) * v
        so_ref[0, hd] = s_new
        out = jnp.sum(column(q_ref, hd) * s_new, axis=0, keepdims=True)
        gate = gg_ref[0, :, vs].astype(F32)
        o_ref[0, :, vs] = (_rms_norm(out, ng_ref[...]) * (gate * jax.nn.sigmoid(gate))).astype(BF16)


def _gla_sample(gq, gk, la, gv, gg, state, norm_g):
    n = gq.shape[0]
    assert GLA_HDV % GLA_HDK == 0
    key_row = pl.BlockSpec((1, 1, GLA_DK), lambda i: (i, 0, 0))
    val_row = pl.BlockSpec((1, 1, GLA_DV), lambda i: (i, 0, 0))
    st = pl.BlockSpec((1, GLA_HEADS, GLA_HDK, GLA_HDV), lambda i: (i, 0, 0, 0))
    as_row = lambda a: a.reshape(n, 1, a.shape[-1])
    o, s_new = pl.pallas_call(
        _gla_sample_body,
        grid=(n,),
        in_specs=[key_row, key_row, key_row, val_row, val_row, st, _resident((1, GLA_HDV))],
        out_specs=[val_row, st],
        out_shape=[jax.ShapeDtypeStruct((n, 1, GLA_DV), BF16),
                   jax.ShapeDtypeStruct((n, GLA_HEADS, GLA_HDK, GLA_HDV), F32)],
        compiler_params=pltpu.CompilerParams(dimension_semantics=("parallel",),
                                             vmem_limit_bytes=_vmem_limit(16 << 20)),
        name="gla_sample_step",
    )(as_row(gq), as_row(gk), as_row(la), as_row(gv), as_row(gg), state, norm_g)
    return o.reshape(n, GLA_DV), s_new


def _sample_select_body(q_ref, ksum_ref, sel_ref):
    nb = ksum_ref.shape[1]
    q = q_ref[0]
    lane = lax.broadcasted_iota(jnp.int32, (nb, LANES), 1)
    row = lax.broadcasted_iota(jnp.int32, (nb, LANES), 0).astype(F32)
    gate = jnp.full((nb, LANES), -jnp.inf, F32)
    for hd in range(MOBA_HEADS):
        sl = slice(hd * MOBA_HD, (hd + 1) * MOBA_HD)
        g = jnp.sum(ksum_ref[0, :, sl] * (1.0 / MOBA_BLOCK) * q[:, sl], axis=1, keepdims=True)
        gate = jnp.where(lane == hd, g, gate)
    out_row = lax.broadcasted_iota(jnp.int32, (SUBLANES, LANES), 0)
    out = jnp.zeros((SUBLANES, LANES), jnp.int32)
    for r in range(MOBA_TOPK):
        best = jnp.max(gate, axis=0, keepdims=True)
        first = jnp.min(jnp.where(gate == best, row, float(nb)), axis=0, keepdims=True)
        out = jnp.where(out_row == r, first.astype(jnp.int32), out)
        gate = jnp.where(row == first, -jnp.inf, gate)
    sel_ref[0] = out


def _sample_select(q, ksum):
    n, nb = ksum.shape[0], ksum.shape[1]
    sel = pl.pallas_call(
        _sample_select_body,
        grid=(n,),
        in_specs=[pl.BlockSpec((1, 1, MOBA_W), lambda i: (i, 0, 0)),
                  pl.BlockSpec((1, nb, MOBA_W), lambda i: (i, 0, 0))],
        out_specs=pl.BlockSpec((1, SUBLANES, LANES), lambda i: (i, 0, 0)),
        out_shape=jax.ShapeDtypeStruct((n, SUBLANES, LANES), jnp.int32),
        compiler_params=pltpu.CompilerParams(dimension_semantics=("parallel",)),
        name="moba_sample_select",
    )(q.reshape(n, 1, MOBA_W), ksum)
    return jnp.transpose(sel[:, :MOBA_TOPK, :MOBA_HEADS], (0, 2, 1)).reshape(-1)


def _sample_attend_body(pt_ref, sel_ref, q_ref, kn_ref, vn_ref, ck_hbm, cv_hbm, o_ref, k_buf, v_buf, sem,
                        *, layer, n_pages):
    per_block = MOBA_BLOCK // PAGE_SIZE
    n_sel = MOBA_TOPK * per_block
    n_slots = MOBA_HEADS * n_sel
    seq = pl.program_id(0)
    scale = MOBA_HD ** -0.5

    def page_copies(s, hd, i):
        blk = sel_ref[(s * MOBA_HEADS + hd) * MOBA_TOPK + i // per_block]
        page = pt_ref[s * n_pages + blk * per_block + i % per_block]
        slot = (s % 2) * n_slots + hd * n_sel + i
        return (pltpu.make_async_copy(ck_hbm.at[layer, page, :, hd, :], k_buf.at[slot], sem.at[0, slot]),
                pltpu.make_async_copy(cv_hbm.at[layer, page, :, hd, :], v_buf.at[slot], sem.at[1, slot]))

    def start_fetch(s):
        for hd in range(MOBA_HEADS):
            for i in range(n_sel):
                for cp in page_copies(s, hd, i):
                    cp.start()

    @pl.when(seq == 0)
    def _():
        start_fetch(seq)

    @pl.when(seq + 1 < pl.num_programs(0))
    def _():
        start_fetch(seq + 1)

    for hd in range(MOBA_HEADS):
        for i in range(n_sel):
            for cp in page_copies(seq, hd, i):
                cp.wait()

    base = (seq % 2) * n_slots
    for hd in range(MOBA_HEADS):
        sl = slice(hd * MOBA_HD, (hd + 1) * MOBA_HD)
        q = q_ref[0, :, sl]
        s_new = jnp.sum(kn_ref[0, :, sl] * q, axis=1, keepdims=True) * scale
        scores = [jnp.sum(k_buf[base + hd * n_sel + i] * q, axis=1, keepdims=True) * scale for i in range(n_sel)]
        top = s_new
        for s in scores:
            top = jnp.maximum(top, jnp.max(s, axis=0, keepdims=True))
        p_new = jnp.exp(s_new - top)
        denom = p_new
        acc = p_new * vn_ref[0, :, sl]
        for i, s in enumerate(scores):
            p = jnp.exp(s - top)
            denom = denom + jnp.sum(p, axis=0, keepdims=True)
            acc = acc + jnp.sum(p * v_buf[base + hd * n_sel + i], axis=0, keepdims=True)
        o_ref[0, :, sl] = (acc / denom).astype(BF16)


def _sample_attend(q, k_new, v_new, cache_k, cache_v, layer, page_table_flat, sel_flat, n_pages):
    n = q.shape[0]
    n_slots = MOBA_HEADS * MOBA_TOPK * (MOBA_BLOCK // PAGE_SIZE)
    tok = pl.BlockSpec((1, 1, MOBA_W), lambda s, pt, sel: (s, 0, 0))
    hbm = pl.BlockSpec(memory_space=pl.ANY)
    as_tok = lambda a: a.reshape(n, 1, MOBA_W)
    out = pl.pallas_call(
        functools.partial(_sample_attend_body, layer=layer, n_pages=n_pages),
        grid_spec=pltpu.PrefetchScalarGridSpec(
            num_scalar_prefetch=2,
            grid=(n,),
            in_specs=[tok, tok, tok, hbm, hbm],
            out_specs=tok,
            scratch_shapes=[pltpu.VMEM((2 * n_slots, PAGE_SIZE, MOBA_HD), F32),
                            pltpu.VMEM((2 * n_slots, PAGE_SIZE, MOBA_HD), F32),
                            pltpu.SemaphoreType.DMA((2, 2 * n_slots))],
        ),
        out_shape=jax.ShapeDtypeStruct((n, 1, MOBA_W), BF16),
        compiler_params=pltpu.CompilerParams(dimension_semantics=("arbitrary",),
                                             vmem_limit_bytes=_vmem_limit(5 * n_slots * PAGE_SIZE * MOBA_HD * 4)),
        name="moba_sample_attention",
    )(page_table_flat, sel_flat, as_tok(q), as_tok(k_new), as_tok(v_new), cache_k, cache_v)
    return out.reshape(n, MOBA_W)


def _rope_tables(pos):
    half = MOBA_HD // 2
    inv = ROPE_THETA ** (-jnp.arange(half, dtype=F32) / half)
    ang = pos.astype(F32)[:, None] * inv[None, :]
    cos, sin = jnp.cos(ang), jnp.sin(ang)
    return jnp.concatenate([cos, cos], axis=1), jnp.concatenate([-sin, sin], axis=1)


def _layer_weights(w_in, w_a2, b_a, q_norm_g, k_norm_g, w_pa, w_pb, w_o):
    offs = np.concatenate([[0], np.cumsum(IN_SPLITS)])
    gq, gk, gv, gg, glr, mq, mk, mv, ga, gb = (w_in[:, offs[i]:offs[i + 1]] for i in range(len(IN_SPLITS)))
    glr = jnp.pad(glr, ((0, 0), (0, LANES - GLA_RANK)))
    a2 = jnp.pad(w_a2, ((0, LANES - GLA_RANK), (0, 0)))
    b16 = lambda a: a.astype(BF16)
    return dict(
        gq=b16(gq), gk=b16(gk), gv=b16(gv), gg=b16(gg), glr=b16(glr), a2=b16(a2), ba=b_a.reshape(1, GLA_DK),
        mq=b16(mq), mk=b16(mk), mv=b16(mv),
        qg=jnp.tile(q_norm_g, MOBA_HEADS).reshape(1, MOBA_W), kg=jnp.tile(k_norm_g, MOBA_HEADS).reshape(1, MOBA_W),
        ga=b16(ga), gb=b16(gb), pa=b16(w_pa), pb=b16(w_pb), o=b16(w_o))


def kernel(x_prompt, x_sample, cache_k, cache_v, state_gla, page_table, ffn1_g, ffn1_wg, ffn1_wu, ffn1_wd, mix_g, w_in, w_a2, b_a, gla_norm_g, q_norm_g, k_norm_g, w_pa, w_pb, w_o, ffn2_g, ffn2_wg, ffn2_wu, ffn2_wd):
    batch, seq, _ = x_prompt.shape
    n_dec, dec_seq, _ = x_sample.shape
    n_pages = page_table.shape[1]
    depth = w_in.shape[0]
    past_len = n_pages * PAGE_SIZE
    assert dec_seq == 1 and seq % MOBA_BLOCK == 0 and past_len % MOBA_BLOCK == 0
    assert past_len // MOBA_BLOCK >= MOBA_TOPK
    assert (seq // MOBA_BLOCK) % MOBA_GROUP == 0 and MOBA_GROUP == MOBA_KV_BLOCKS

    yp = x_prompt.reshape(batch * seq, D_MODEL)
    ys = x_sample.reshape(n_dec, D_MODEL)
    rope_p = _rope_tables(jnp.arange(seq, dtype=jnp.int32))
    rope_s = _rope_tables(jnp.full((n_dec,), past_len, jnp.int32))
    pt_flat = page_table.reshape(-1)
    row = lambda a: a.reshape(1, -1)
    outs = [[] for _ in range(6)]
    for l in range(depth):
        w = _layer_weights(w_in[l], w_a2[l], b_a[l], q_norm_g[l], k_norm_g[l], w_pa[l], w_pb[l], w_o[l])
        ffn1 = (row(ffn1_g[l]), ffn1_wg[l].astype(BF16), ffn1_wu[l].astype(BF16), ffn1_wd[l].astype(BF16))
        ffn2 = (row(ffn2_g[l]), ffn2_wg[l].astype(BF16), ffn2_wu[l].astype(BF16), ffn2_wd[l].astype(BF16))
        norm_g = row(gla_norm_g[l])

        x1 = _ffn(yp, *ffn1)
        gq, gk, gv, gg, la = _gla_proj(x1, row(mix_g[l]), w)
        o_gla, s_prompt = _gla_prompt(gq, gk, la, gv, gg, norm_g, batch, seq)
        _, k, v, qe = _moba_proj(x1, row(mix_g[l]), w, *rope_p, n_seq=batch, select=True)
        o_moba, page_sums = _moba_attend(qe, k, v, batch, seq, cache_k, l, pt_flat)
        yp = _ffn(_merge(x1, row(mix_g[l]), o_gla, o_moba, w), *ffn2)
        outs[0].append(k.reshape(batch, seq, MOBA_HEADS, MOBA_HD))
        outs[1].append(v.reshape(batch, seq, MOBA_HEADS, MOBA_HD))
        outs[2].append(s_prompt)

        x1 = _ffn(ys, *ffn1)
        gq, gk, gv, gg, la = _gla_proj(x1, row(mix_g[l]), w)
        q, k, v, _ = _moba_proj(x1, row(mix_g[l]), w, *rope_s, n_seq=1, select=False)
        o_gla, s_sample = _gla_sample(gq, gk, la, gv, gg, state_gla[l], norm_g)
        sel = _sample_select(q, page_sums.reshape(n_dec, n_pages * PAGE_SIZE // MOBA_BLOCK, MOBA_W))
        o_moba = _sample_attend(q, k, v, cache_k, cache_v, l, pt_flat, sel, n_pages)
        ys = _ffn(_merge(x1, row(mix_g[l]), o_gla, o_moba, w), *ffn2)
        outs[3].append(k.reshape(n_dec, 1, MOBA_HEADS, MOBA_HD))
        outs[4].append(v.reshape(n_dec, 1, MOBA_HEADS, MOBA_HD))
        outs[5].append(s_sample)

    return (yp.reshape(batch, seq, D_MODEL), ys.reshape(n_dec, 1, D_MODEL),
            jnp.stack(outs[0]), jnp.stack(outs[1]), jnp.stack(outs[2]),
            jnp.stack(outs[3]), jnp.stack(outs[4]), jnp.stack(outs[5]))
```

```python
import functools

import numpy as np
import jax
import jax.numpy as jnp
from jax import lax
from jax.experimental import pallas as pl
from jax.experimental.pallas import tpu as pltpu

F32 = jnp.float32
BF16 = jnp.bfloat16

D_MODEL = 1024
D_FF = 2816
GLA_HEADS = 4
GLA_HDK = 128
GLA_HDV = 256
GLA_DK = GLA_HEADS * GLA_HDK
GLA_DV = GLA_HEADS * GLA_HDV
GLA_RANK = 16
GLA_TAU = 16.0
MOBA_HEADS = 8
MOBA_HD = 128
MOBA_W = MOBA_HEADS * MOBA_HD
MOBA_BLOCK = 256
MOBA_TOPK = 3
ROPE_THETA = 10000.0
EPS = 1e-6
PAGE_SIZE = 128
IN_SPLITS = (GLA_DK, GLA_DK, GLA_DV, GLA_DV, GLA_RANK, MOBA_W, MOBA_W, MOBA_W, D_MODEL, D_MODEL)

LANES = 128
SUBLANES = 8
VMEM_BYTES = 64 * 1024 * 1024

FF_CHUNK = 256
FFN_ROWS = 512
PROJ_ROWS = 512
GLA_CHUNK = 128
MOBA_GROUP = 4
MOBA_KV_BLOCKS = 4
MOBA_VISITS = 3
MASK_BIG = 2.0 ** 100
NEG_INIT = -1.0e38
LOG2_E = 1.4426950408889634


def _vmem_limit(nbytes):
    return int(min(VMEM_BYTES - (4 << 20), max(nbytes, 16 << 20)))


def _resident(shape):
    return pl.BlockSpec(shape, lambda *_: (0,) * len(shape), pipeline_mode=pl.Buffered(1))


def _dot(a, b):
    return jnp.dot(a, b, preferred_element_type=F32)


def _dot_nt(a, b):
    return lax.dot_general(a, b, (((1,), (1,)), ((), ())), preferred_element_type=F32)


def _rms_norm(x, g):
    return x * lax.rsqrt(jnp.mean(x * x, axis=-1, keepdims=True) + EPS) * g


def _log_sigmoid(z):
    return jnp.minimum(z, 0.0) - jnp.log1p(jnp.exp(-jnp.abs(z)))


def _split3(x):
    hi = x.astype(BF16)
    r = x - hi.astype(F32)
    mid = r.astype(BF16)
    lo = (r - mid.astype(F32)).astype(BF16)
    return hi, mid, lo


def _ffn_body(x_ref, g_ref, wg_ref, wu_ref, wd_ref, o_ref):
    x = x_ref[...]
    h = _rms_norm(x, g_ref[...]).astype(BF16)
    acc = jnp.zeros_like(x)
    for c in range(D_FF // FF_CHUNK):
        sl = slice(c * FF_CHUNK, (c + 1) * FF_CHUNK)
        a = _dot(h, wg_ref[:, sl])
        u = _dot(h, wu_ref[:, sl])
        act = (a * jax.nn.sigmoid(a) * u).astype(BF16)
        acc = acc + _dot(act, wd_ref[sl, :])
    o_ref[...] = x + 0.5 * acc


def _ffn(x, g, wg, wu, wd):
    m = x.shape[0]
    tm = min(FFN_ROWS, m)
    row = pl.BlockSpec((tm, D_MODEL), lambda i: (i, 0))
    need = 3 * D_MODEL * D_FF * 2 + 4 * tm * D_MODEL * 4 + 6 * tm * D_MODEL * 4
    return pl.pallas_call(
        _ffn_body,
        grid=(m // tm,),
        in_specs=[row, _resident((1, D_MODEL)), _resident((D_MODEL, D_FF)), _resident((D_MODEL, D_FF)),
                  _resident((D_FF, D_MODEL))],
        out_specs=row,
        out_shape=jax.ShapeDtypeStruct((m, D_MODEL), F32),
        compiler_params=pltpu.CompilerParams(dimension_semantics=("parallel",),
                                             vmem_limit_bytes=_vmem_limit(need)),
        name="swiglu_half_step",
    )(x, g, wg, wu, wd)


def _gla_proj_body(x_ref, g_ref, wgq, wgk, wgv, wgg, wglr, wa2, ba, gq_o, gk_o, gv_o, gg_o, la_o):
    h = _rms_norm(x_ref[...], g_ref[...]).astype(BF16)
    gq_o[...] = _dot(h, wgq[...]) * (GLA_HDK ** -0.5)
    gk_o[...] = _dot(h, wgk[...])
    gv_o[...] = _dot(h, wgv[...]).astype(BF16)
    gg_o[...] = _dot(h, wgg[...]).astype(BF16)
    glr = _dot(h, wglr[...]).astype(BF16)
    la_o[...] = _log_sigmoid(_dot(glr, wa2[...]) + ba[...]) * (1.0 / GLA_TAU)


def _gla_proj(x, mix_g, w):
    m = x.shape[0]
    tm = min(PROJ_ROWS, m)

    def rows(width):
        return pl.BlockSpec((tm, width), lambda i: (i, 0))

    weights = [w["gq"], w["gk"], w["gv"], w["gg"], w["glr"], w["a2"], w["ba"]]
    out_shape = [
        jax.ShapeDtypeStruct((m, GLA_DK), F32), jax.ShapeDtypeStruct((m, GLA_DK), F32),
        jax.ShapeDtypeStruct((m, GLA_DV), BF16), jax.ShapeDtypeStruct((m, GLA_DV), BF16),
        jax.ShapeDtypeStruct((m, GLA_DK), F32),
    ]
    w_bytes = sum(int(a.size) * a.dtype.itemsize for a in weights)
    need = w_bytes + 2 * tm * (3 * GLA_DK * 4 + 2 * GLA_DV * 2) + 2 * tm * D_MODEL * 4 + 6 * tm * GLA_DV * 4
    return pl.pallas_call(
        _gla_proj_body,
        grid=(m // tm,),
        in_specs=[rows(D_MODEL), _resident((1, D_MODEL))] + [_resident(a.shape) for a in weights],
        out_specs=[rows(GLA_DK), rows(GLA_DK), rows(GLA_DV), rows(GLA_DV), rows(GLA_DK)],
        out_shape=out_shape,
        compiler_params=pltpu.CompilerParams(dimension_semantics=("parallel",),
                                             vmem_limit_bytes=_vmem_limit(need)),
        name="gla_input_projection",
    )(x, mix_g, *weights)


def _moba_proj_body(x_ref, g_ref, wmq, wmk, wmv, qg, kg, rc, rs, q_o, k_o, v_o, qe_o, ksum_ref, *,
                    select, tiles_per_seq):
    tile = pl.program_id(0) % tiles_per_seq
    if select:
        @pl.when(tile == 0)
        def _():
            ksum_ref[...] = jnp.zeros_like(ksum_ref)

    h = _rms_norm(x_ref[...], g_ref[...]).astype(BF16)
    mq = _dot(h, wmq[...])
    mk = _dot(h, wmk[...])
    v_o[...] = _dot(h, wmv[...])
    cos = rc[...]
    sin = rs[...]
    for hd in range(MOBA_HEADS):
        sl = slice(hd * MOBA_HD, (hd + 1) * MOBA_HD)
        qh = _rms_norm(mq[:, sl], qg[:, sl])
        q_o[:, sl] = qh * cos + pltpu.roll(qh, MOBA_HD // 2, 1) * sin
        kh = _rms_norm(mk[:, sl], kg[:, sl])
        k_o[:, sl] = kh * cos + pltpu.roll(kh, MOBA_HD // 2, 1) * sin
    if not select:
        qe_o[...] = jnp.zeros_like(qe_o)
        return

    tq = MOBA_BLOCK
    nb = ksum_ref.shape[0]
    eye =(lax.broadcasted_iota(jnp.int32, (tq, tq), 0) == lax.broadcasted_iota(jnp.int32, (tq, tq), 1))
    eye = jnp.where(eye, 1.0, 0.0).astype(BF16)
    for c in range(q_o.shape[0] // tq):
        rows = slice(c * tq, (c + 1) * tq)
        own_i = tile * (q_o.shape[0] // tq) + c
        own = own_i.astype(F32)
        ksum_ref[pl.ds(own_i, 1), :] = jnp.sum(k_o[rows, :], axis=0, keepdims=True)
        gates = []
        for hd in range(MOBA_HEADS):
            sl = slice(hd * MOBA_HD, (hd + 1) * MOBA_HD)
            q = q_o[rows, sl]
            means = ksum_ref[:, sl] * (1.0 / MOBA_BLOCK)
            q_hi = q.astype(BF16)
            q_lo = (q - q_hi.astype(F32)).astype(BF16)
            m_hi = means.astype(BF16)
            m_lo = (means - m_hi.astype(F32)).astype(BF16)
            gates.append(_dot_nt(m_hi, q_hi) + _dot_nt(m_lo, q_hi) + _dot_nt(m_hi, q_lo))
            qe_o[0, hd, rows, :MOBA_HD] = (q * (MOBA_HD ** -0.5 * LOG2_E)).astype(BF16)
        gate = jnp.concatenate(gates, axis=1)
        blk = lax.broadcasted_iota(jnp.int32, gate.shape, 0).astype(F32)
        gate = jnp.where(blk < own, gate, -jnp.inf)
        mask = jnp.where(blk == own, 0.0, -1.0)
        for _ in range(MOBA_TOPK):
            best = jnp.max(gate, axis=0, keepdims=True)
            first = jnp.min(jnp.where(gate == best, blk, float(nb)), axis=0, keepdims=True)
            first = jnp.where(best > -jnp.inf, first, -1.0)
            pick = blk == first
            mask = jnp.where(pick, 0.0, mask)
            gate = jnp.where(pick, -jnp.inf, gate)
        mask = jnp.concatenate([mask, jnp.full((LANES - nb, mask.shape[1]), -1.0, F32)], axis=0).astype(BF16)
        for hd in range(MOBA_HEADS):
            qe_o[0, hd, rows, MOBA_HD:] = _dot_nt(eye, mask[:, hd * tq:(hd + 1) * tq]).astype(BF16)


def _moba_proj(x, mix_g, w, rope_cos, rope_sin, n_seq, select):
    m = x.shape[0]
    seq = m // n_seq
    tm = min(PROJ_ROWS, seq if select else m)
    n_rope = rope_cos.shape[0] // tm
    per_seq = seq // tm
    nb = max(seq // MOBA_BLOCK, SUBLANES)
    rows = pl.BlockSpec((tm, MOBA_W), lambda i: (i, 0))
    rope_spec = pl.BlockSpec((tm, MOBA_HD), lambda i: (i % n_rope, 0))
    weights = [w["mq"], w["mk"], w["mv"], w["qg"], w["kg"]]
    w_bytes = sum(int(a.size) * a.dtype.itemsize for a in weights)
    out_bytes = tm * MOBA_W * 3 * 4 + tm * MOBA_HEADS * 2 * MOBA_HD * 2
    need = w_bytes + 2 * out_bytes + 2 * tm * D_MODEL * 4 + 8 * tm * MOBA_W * 4 + (4 << 20)
    return pl.pallas_call(
        functools.partial(_moba_proj_body, select=select, tiles_per_seq=per_seq),
        grid=(m // tm,),
        in_specs=[pl.BlockSpec((tm, D_MODEL), lambda i: (i, 0)), _resident((1, D_MODEL))]
                 + [_resident(a.shape) for a in weights] + [rope_spec, rope_spec],
        out_specs=[rows, rows, rows,
                   pl.BlockSpec((1, MOBA_HEADS, tm, 2 * MOBA_HD), lambda i: (i // per_seq, 0, i % per_seq, 0))],
        out_shape=[jax.ShapeDtypeStruct((m, MOBA_W), F32)] * 3
                  + [jax.ShapeDtypeStruct((n_seq if select else 1, MOBA_HEADS, seq if select else m, 2 * MOBA_HD), BF16)],
        scratch_shapes=[pltpu.VMEM((nb, MOBA_W), F32)],
        compiler_params=pltpu.CompilerParams(dimension_semantics=("arbitrary",),
                                             vmem_limit_bytes=_vmem_limit(need)),
        name="moba_input_projection",
    )(x, mix_g, *weights, rope_cos, rope_sin)


def _gla_tables():
    c = GLA_CHUNK
    t = np.arange(c)
    le = t[None, :] <= t[:, None]
    gt = t[None, :] > t[:, None]
    masks = []
    s = c // 2
    while s >= 1:
        same = (t // (2 * s))[:, None] == (t // (2 * s))[None, :]
        right = (t % (2 * s)) >= s
        masks.append(same & right[:, None] & ~right[None, :])
        s //= 2
    masks.append(np.eye(c, dtype=bool))
    tail = np.concatenate([gt.T, np.ones((c, c), dtype=bool)], axis=1)
    return le.astype(np.float32), np.stack(masks).astype(np.float32), tail.astype(np.float32)


def _level_reference(b, s):
    c, dk = b.shape
    if s >= SUBLANES:
        blocks = b.reshape(c // (2 * s), 2 * s, dk)
        return jnp.broadcast_to(blocks[:, s - 1:s, :], blocks.shape).reshape(c, dk)
    rows8 = b.reshape(c // SUBLANES, SUBLANES, dk)
    sub = lax.broadcasted_iota(jnp.int32, rows8.shape, 1)
    ref = jnp.broadcast_to(rows8[:, SUBLANES - s - 1:SUBLANES - s, :], rows8.shape)
    for first in range(SUBLANES - 4 * s, -1, -2 * s):
        ref = jnp.where(sub < first + 2 * s, jnp.broadcast_to(rows8[:, first + s - 1:first + s, :], rows8.shape), ref)
    return ref.reshape(c, dk)


def _gla_body(q_ref, k_ref, la_ref, v_ref, gg_ref, ng_ref, le_ref, masks_ref, tail_ref, o_ref, s_ref):
    c = GLA_CHUNK
    n_lvl = masks_ref.shape[0] - 1

    @pl.when(pl.program_id(1) == 0)
    def _():
        s_ref[...] = jnp.zeros_like(s_ref)

    le = le_ref[...]
    tail = tail_ref[...]
    for hd in range(GLA_HEADS):
        ks = slice(hd * GLA_HDK, (hd + 1) * GLA_HDK)
        vs = slice(hd * GLA_HDV, (hd + 1) * GLA_HDV)
        q = q_ref[:, ks]
        k = k_ref[:, ks]
        v = v_ref[:, vs]
        state = s_ref[0, hd]
        b = sum(_dot(le, p) for p in _split3(la_ref[:, ks])) * LOG2_E
        et = jnp.exp(sum(_dot(p, tail) for p in _split3(la_ref[:, ks].T)))
        out = _dot((q * jnp.exp2(b)).astype(BF16), state.astype(BF16))
        attn = masks_ref[n_lvl] * _dot_nt(q.astype(BF16), k.astype(BF16))
        for lv in range(n_lvl):
            e = jnp.exp2(-jnp.abs(b - _level_reference(b, c >> (lv + 1))))
            attn = attn + masks_ref[lv] * _dot_nt((q * e).astype(BF16), (k * e).astype(BF16))
        out = out + _dot(attn.astype(BF16), v)
        k_dec = (k.T * et[:, :c]).astype(BF16)
        decay = et[:, c:]
        s_ref[0, hd] = jnp.concatenate([state[:, :c] * decay, state[:, c:] * decay], axis=1) + _dot(k_dec, v)
        gate = gg_ref[:, vs].astype(F32)
        o_ref[:, vs] = (_rms_norm(out, ng_ref[...]) * (gate * jax.nn.sigmoid(gate))).astype(BF16)


def _gla_prompt(gq, gk, la, gv, gg, norm_g, batch, seq):
    c = GLA_CHUNK
    n_chunks = seq // c
    le, masks, tail = (jnp.asarray(a, BF16 if i != 1 else F32) for i, a in enumerate(_gla_tables()))

    def rows(width):
        return pl.BlockSpec((c, width), lambda b, t: (b * n_chunks + t, 0))

    need = 2 * (3 * c * GLA_DK * 4 + 3 * c * GLA_DV * 2) + 4 * GLA_DK * GLA_HDV * 4 + (8 << 20)
    return pl.pallas_call(
        _gla_body,
        grid=(batch, n_chunks),
        in_specs=[rows(GLA_DK), rows(GLA_DK), rows(GLA_DK), rows(GLA_DV), rows(GLA_DV),
                  _resident((1, GLA_HDV)), _resident(le.shape), _resident(masks.shape), _resident(tail.shape)],
        out_specs=[rows(GLA_DV),
                   pl.BlockSpec((1, GLA_HEADS, GLA_HDK, GLA_HDV), lambda b, t: (b, 0, 0, 0))],
        out_shape=[jax.ShapeDtypeStruct((batch * seq, GLA_DV), BF16),
                   jax.ShapeDtypeStruct((batch, GLA_HEADS, GLA_HDK, GLA_HDV), F32)],
        compiler_params=pltpu.CompilerParams(dimension_semantics=("parallel", "arbitrary"),
                                             vmem_limit_bytes=_vmem_limit(need)),
        name="gla_prompt_chunks",
    )(gq, gk, la, gv, gg, norm_g, le, masks, tail)


def _moba_attend_body(pt_ref, qe_ref, k_ref, v_ref, *refs):
    pages, (o_ref, ksum_ref, acc_ref, m_ref) = refs[:-4], refs[-4:]
    j = pl.program_id(2)
    tb = MOBA_BLOCK
    tkv = k_ref.shape[0]
    n_groups = qe_ref.shape[2] // (tb * MOBA_GROUP)

    @pl.when(j == 0)
    def _():
        acc_ref[...] = jnp.zeros_like(acc_ref)
        m_ref[...] = jnp.full_like(m_ref, NEG_INIT)

    k = k_ref[...].astype(BF16)
    lane = lax.broadcasted_iota(jnp.int32, (tkv, MOBA_HD), 1)
    key_block = j * (tkv // tb) + lax.broadcasted_iota(jnp.int32, (tkv, MOBA_HD), 0) // tb
    k_ext = jnp.concatenate([k, jnp.where(lane == key_block, MASK_BIG, 0.0).astype(BF16)], axis=1)
    v_ext = jnp.concatenate([v_ref[...].astype(BF16), jnp.ones((tkv, MOBA_HD), BF16)], axis=1)

    def update(rows, s):
        m_old = m_ref[rows, :]
        m_new = jnp.maximum(m_old, jnp.max(s, axis=1, keepdims=True))
        p = jnp.exp2(s - jnp.concatenate([m_new] * (tkv // MOBA_HD), axis=1)).astype(BF16)
        alpha = jnp.exp2(m_old - m_new)
        acc_new = jnp.concatenate([alpha, alpha], axis=1) * acc_ref[rows, :] + _dot(p, v_ext)
        m_ref[rows, :] = m_new
        acc_ref[rows, :] = acc_new
        return acc_new

    rows_per_group = tb * MOBA_GROUP

    def group_rows(g):
        return pl.ds(pl.multiple_of(g * rows_per_group, rows_per_group), rows_per_group)

    def scores(g):
        return _dot_nt(qe_ref[0, 0, group_rows(g), :], k_ext)

    first = (j * tkv) // rows_per_group
    n_later = n_groups - 1 - first
    n_extra = n_later % MOBA_VISITS

    def page_sum(page_ref):
        return jnp.sum(jnp.sum(page_ref[...].reshape(4, PAGE_SIZE // 4, MOBA_HEADS, MOBA_HD), axis=1), axis=0)

    def first_groups(extra):
        lower = (lax.broadcasted_iota(jnp.int32, (tb, tb), 1) <= lax.broadcasted_iota(jnp.int32, (tb, tb), 0))
        n_parts = 1 if extra else 2
        part = rows_per_group // n_parts
        visits = []
        for hh in range(n_parts):
            rows = pl.ds(pl.multiple_of(first * rows_per_group + hh * part, part), part)
            s_own = _dot_nt(qe_ref[0, 0, rows, :], k_ext)
            tiles = []
            for t2 in range(part // tb):
                t = hh * (part // tb) + t2
                r, c = slice(t2 * tb, (t2 + 1) * tb), slice(t * tb, (t + 1) * tb)
                diag = jnp.where(lower, s_own[r, c], -MASK_BIG)
                parts = (([s_own[r, :t * tb]] if t else []) + [diag]
                         + ([s_own[r, (t + 1) * tb:]] if (t + 1) * tb < tkv else []))
                tiles.append(jnp.concatenate(parts, axis=1))
            visits.append((rows, jnp.concatenate(tiles, axis=0)))
        visits += [(group_rows(first + u), scores(first + u)) for u in range(1, extra + 1)]
        for rows, s in visits:
            update(rows, s)
        acc = acc_ref[pl.ds(pl.multiple_of(j * tkv, tkv), tkv), :]
        o_ref[...] = (acc[:, :MOBA_HD] / acc[:, MOBA_HD:]).astype(BF16)
        per_block = MOBA_BLOCK // PAGE_SIZE
        for blk in range(len(pages) // per_block):
            tot = page_sum(pages[blk * per_block])
            for p in range(1, per_block):
                tot = tot + page_sum(pages[blk * per_block + p])
            ksum_ref[0, blk] = tot

    for extra in range(MOBA_VISITS):
        pl.when(n_extra == extra)(functools.partial(first_groups, extra))

    def body(i, carry):
        g0 = first + 1 + n_extra + MOBA_VISITS * i
        ss = [scores(g0 + u) for u in range(MOBA_VISITS)]
        for u, s in enumerate(ss):
            update(group_rows(g0 + u), s)
        return carry

    lax.fori_loop(0, n_later // MOBA_VISITS, body, 0)


def _moba_attend(qe, k, v, batch, seq, cache, layer, page_table_flat):
    tkv = MOBA_BLOCK * MOBA_KV_BLOCKS
    nb = seq // tkv
    n_steps = batch * MOBA_HEADS * nb
    per_block = MOBA_BLOCK // PAGE_SIZE
    pages_per_step = page_table_flat.shape[0] // n_steps
    assert pages_per_step * n_steps == page_table_flat.shape[0] and pages_per_step % per_block == 0
    kv = pl.BlockSpec((tkv, MOBA_HD), lambda b, h, j, pt: (b * nb + j, h))

    def step(b, h, j):
        return (b * MOBA_HEADS + h) * nb + j

    def page_spec(i):
        return pl.BlockSpec((None, None, PAGE_SIZE, MOBA_HEADS, MOBA_HD),
                            lambda b, h, j, pt: (layer, pt[step(b, h, j) * pages_per_step + i], 0, 0, 0))

    blocks_per_step = pages_per_step // per_block
    need = (2 * seq * 2 * MOBA_HD * 2 + 3 * seq * MOBA_HD * 4 + 2 * pages_per_step * PAGE_SIZE * MOBA_W * 4
            + (16 << 20))
    out, sums = pl.pallas_call(
        _moba_attend_body,
        grid_spec=pltpu.PrefetchScalarGridSpec(
            num_scalar_prefetch=1,
            grid=(batch, MOBA_HEADS, nb),
            in_specs=[pl.BlockSpec((1, 1, seq, 2 * MOBA_HD), lambda b, h, j, pt: (b, h, 0, 0)), kv, kv]
                     + [page_spec(i) for i in range(pages_per_step)],
            out_specs=[kv, pl.BlockSpec((1, blocks_per_step, MOBA_HEADS, MOBA_HD),
                                        lambda b, h, j, pt: (step(b, h, j), 0, 0, 0))],
            scratch_shapes=[pltpu.VMEM((seq, 2 * MOBA_HD), F32), pltpu.VMEM((seq, MOBA_HD), F32)],
        ),
        out_shape=[jax.ShapeDtypeStruct((batch * seq, MOBA_W), BF16),
                   jax.ShapeDtypeStruct((n_steps, blocks_per_step, MOBA_HEADS, MOBA_HD), F32)],
        compiler_params=pltpu.CompilerParams(dimension_semantics=("parallel", "parallel", "arbitrary"),
                                             vmem_limit_bytes=_vmem_limit(need)),
        name="moba_prompt_attention",
    )(page_table_flat, qe, k, v, *([cache] * pages_per_step))
    return out, sums


def _merge_body(x_ref, g_ref, oa_ref, ob_ref, wga, wgb, wpa, wpb, wo, o_ref):
    x = x_ref[...]
    h = _rms_norm(x, g_ref[...]).astype(BF16)
    mix = jax.nn.sigmoid(_dot(h, wga[...])) * _dot(oa_ref[...], wpa[...])
    mix = mix + jax.nn.sigmoid(_dot(h, wgb[...])) * _dot(ob_ref[...], wpb[...])
    o_ref[...] = x + _dot(mix.astype(BF16), wo[...])


def _merge(x, mix_g, o_gla, o_moba, w):
    m = x.shape[0]
    tm = min(FFN_ROWS, m)
    row32 = pl.BlockSpec((tm, D_MODEL), lambda i: (i, 0))
    sq = _resident((D_MODEL, D_MODEL))
    need = 5 * D_MODEL * D_MODEL * 2 + 4 * tm * D_MODEL * 4 + 4 * tm * D_MODEL * 2 + 6 * tm * D_MODEL * 4
    return pl.pallas_call(
        _merge_body,
        grid=(m // tm,),
        in_specs=[row32, _resident((1, D_MODEL)), row32, row32, sq, sq, sq, sq, sq],
        out_specs=row32,
        out_shape=jax.ShapeDtypeStruct((m, D_MODEL), F32),
        compiler_params=pltpu.CompilerParams(dimension_semantics=("parallel",),
                                             vmem_limit_bytes=_vmem_limit(need)),
        name="gated_merge_projection",
    )(x, mix_g, o_gla, o_moba, w["ga"], w["gb"], w["pa"], w["pb"], w["o"])


def _gla_sample_body(q_ref, k_ref, la_ref, v_ref, gg_ref, s_ref, ng_ref, o_ref, so_ref):
    def column(ref, hd):
        row = ref[0, :, hd * GLA_HDK:(hd + 1) * GLA_HDK]
        col = jnp.broadcast_to(row, (GLA_HDK, GLA_HDK)).T
        return jnp.concatenate([col] * (GLA_HDV // GLA_HDK), axis=1)

    for hd in range(GLA_HEADS):
        vs = slice(hd * GLA_HDV, (hd + 1) * GLA_HDV)
        v = v_ref[0, :, vs].astype(F32)
        s_new = jnp.exp(column(la_ref, hd)) * s_ref[0, hd] + column(k_ref, hd) * v
        so_ref[0, hd] = s_new
        out = jnp.sum(column(q_ref, hd) * s_new, axis=0, keepdims=True)
        gate = gg_ref[0, :, vs].astype(F32)
        o_ref[0, :, vs] = (_rms_norm(out, ng_ref[...]) * (gate * jax.nn.sigmoid(gate))).astype(BF16)


def _gla_sample(gq, gk, la, gv, gg, state, norm_g):
    n = gq.shape[0]
    assert GLA_HDV % GLA_HDK == 0
    key_row = pl.BlockSpec((1, 1, GLA_DK), lambda i: (i, 0, 0))
    val_row = pl.BlockSpec((1, 1, GLA_DV), lambda i: (i, 0, 0))
    st = pl.BlockSpec((1, GLA_HEADS, GLA_HDK, GLA_HDV), lambda i: (i, 0, 0, 0))
    as_row = lambda a: a.reshape(n, 1, a.shape[-1])
    o, s_new = pl.pallas_call(
        _gla_sample_body,
        grid=(n,),
        in_specs=[key_row, key_row, key_row, val_row, val_row, st, _resident((1, GLA_HDV))],
        out_specs=[val_row, st],
        out_shape=[jax.ShapeDtypeStruct((n, 1, GLA_DV), BF16),
                   jax.ShapeDtypeStruct((n, GLA_HEADS, GLA_HDK, GLA_HDV), F32)],
        compiler_params=pltpu.CompilerParams(dimension_semantics=("parallel",),
                                             vmem_limit_bytes=_vmem_limit(16 << 20)),
        name="gla_sample_step",
    )(as_row(gq), as_row(gk), as_row(la), as_row(gv), as_row(gg), state, norm_g)
    return o.reshape(n, GLA_DV), s_new


def _sample_select_body(q_ref, ksum_ref, sel_ref):
    nb = ksum_ref.shape[1]
    q = q_ref[0]
    lane = lax.broadcasted_iota(jnp.int32, (nb, LANES), 1)
    row = lax.broadcasted_iota(jnp.int32, (nb, LANES), 0).astype(F32)
    gate = jnp.full((nb, LANES), -jnp.inf, F32)
    for hd in range(MOBA_HEADS):
        sl = slice(hd * MOBA_HD, (hd + 1) * MOBA_HD)
        g = jnp.sum(ksum_ref[0, :, sl] * (1.0 / MOBA_BLOCK) * q[:, sl], axis=1, keepdims=True)
        gate = jnp.where(lane == hd, g, gate)
    out_row = lax.broadcasted_iota(jnp.int32, (SUBLANES, LANES), 0)
    out = jnp.zeros((SUBLANES, LANES), jnp.int32)
    for r in range(MOBA_TOPK):
        best = jnp.max(gate, axis=0, keepdims=True)
        first = jnp.min(jnp.where(gate == best, row, float(nb)), axis=0, keepdims=True)
        out = jnp.where(out_row == r, first.astype(jnp.int32), out)
        gate = jnp.where(row == first, -jnp.inf, gate)
    sel_ref[0] = out


def _sample_select(q, ksum):
    n, nb = ksum.shape[0], ksum.shape[1]
    sel = pl.pallas_call(
        _sample_select_body,
        grid=(n,),
        in_specs=[pl.BlockSpec((1, 1, MOBA_W), lambda i: (i, 0, 0)),
                  pl.BlockSpec((1, nb, MOBA_W), lambda i: (i, 0, 0))],
        out_specs=pl.BlockSpec((1, SUBLANES, LANES), lambda i: (i, 0, 0)),
        out_shape=jax.ShapeDtypeStruct((n, SUBLANES, LANES), jnp.int32),
        compiler_params=pltpu.CompilerParams(dimension_semantics=("parallel",)),
        name="moba_sample_select",
    )(q.reshape(n, 1, MOBA_W), ksum)
    return jnp.transpose(sel[:, :MOBA_TOPK, :MOBA_HEADS], (0, 2, 1)).reshape(-1)


def _sample_attend_body(pt_ref, sel_ref, q_ref, kn_ref, vn_ref, ck_hbm, cv_hbm, o_ref, k_buf, v_buf, sem,
                        *, layer, n_pages):
    per_block = MOBA_BLOCK // PAGE_SIZE
    n_sel = MOBA_TOPK * per_block
    n_slots = MOBA_HEADS * n_sel
    seq = pl.program_id(0)
    scale = MOBA_HD ** -0.5

    def page_copies(s, hd, i):
        blk = sel_ref[(s * MOBA_HEADS + hd) * MOBA_TOPK + i // per_block]
        page = pt_ref[s * n_pages + blk * per_block + i % per_block]
        slot = (s % 2) * n_slots + hd * n_sel + i
        return (pltpu.make_async_copy(ck_hbm.at[layer, page, :, hd, :], k_buf.at[slot], sem.at[0, slot]),
                pltpu.make_async_copy(cv_hbm.at[layer, page, :, hd, :], v_buf.at[slot], sem.at[1, slot]))

    def start_fetch(s):
        for hd in range(MOBA_HEADS):
            for i in range(n_sel):
                for cp in page_copies(s, hd, i):
                    cp.start()

    @pl.when(seq == 0)
    def _():
        start_fetch(seq)

    @pl.when(seq + 1 < pl.num_programs(0))
    def _():
        start_fetch(seq + 1)

    for hd in range(MOBA_HEADS):
        for i in range(n_sel):
            for cp in page_copies(seq, hd, i):
                cp.wait()

    base = (seq % 2) * n_slots
    for hd in range(MOBA_HEADS):
        sl = slice(hd * MOBA_HD, (hd + 1) * MOBA_HD)
        q = q_ref[0, :, sl]
        s_new = jnp.sum(kn_ref[0, :, sl] * q, axis=1, keepdims=True) * scale
        scores = [jnp.sum(k_buf[base + hd * n_sel + i] * q, axis=1, keepdims=True) * scale for i in range(n_sel)]
        top = s_new
        for s in scores:
            top = jnp.maximum(top, jnp.max(s, axis=0, keepdims=True))
        p_new = jnp.exp(s_new - top)
        denom = p_new
        acc = p_new * vn_ref[0, :, sl]
        for i, s in enumerate(scores):
            p = jnp.exp(s - top)
            denom = denom + jnp.sum(p, axis=0, keepdims=True)
            acc = acc + jnp.sum(p * v_buf[base + hd * n_sel + i], axis=0, keepdims=True)
        o_ref[0, :, sl] = (acc / denom).astype(BF16)


def _sample_attend(q, k_new, v_new, cache_k, cache_v, layer, page_table_flat, sel_flat, n_pages):
    n = q.shape[0]
    n_slots = MOBA_HEADS * MOBA_TOPK * (MOBA_BLOCK // PAGE_SIZE)
    tok = pl.BlockSpec((1, 1, MOBA_W), lambda s, pt, sel: (s, 0, 0))
    hbm = pl.BlockSpec(memory_space=pl.ANY)
    as_tok = lambda a: a.reshape(n, 1, MOBA_W)
    out = pl.pallas_call(
        functools.partial(_sample_attend_body, layer=layer, n_pages=n_pages),
        grid_spec=pltpu.PrefetchScalarGridSpec(
            num_scalar_prefetch=2,
            grid=(n,),
            in_specs=[tok, tok, tok, hbm, hbm],
            out_specs=tok,
            scratch_shapes=[pltpu.VMEM((2 * n_slots, PAGE_SIZE, MOBA_HD), F32),
                            pltpu.VMEM((2 * n_slots, PAGE_SIZE, MOBA_HD), F32),
                            pltpu.SemaphoreType.DMA((2, 2 * n_slots))],
        ),
        out_shape=jax.ShapeDtypeStruct((n, 1, MOBA_W), BF16),
        compiler_params=pltpu.CompilerParams(dimension_semantics=("arbitrary",),
                                             vmem_limit_bytes=_vmem_limit(5 * n_slots * PAGE_SIZE * MOBA_HD * 4)),
        name="moba_sample_attention",
    )(page_table_flat, sel_flat, as_tok(q), as_tok(k_new), as_tok(v_new), cache_k, cache_v)
    return out.reshape(n, MOBA_W)


def _rope_tables(pos):
    half = MOBA_HD // 2
    inv = ROPE_THETA ** (-jnp.arange(half, dtype=F32) / half)
    ang = pos.astype(F32)[:, None] * inv[None, :]
    cos, sin = jnp.cos(ang), jnp.sin(ang)
    return jnp.concatenate([cos, cos], axis=1), jnp.concatenate([-sin, sin], axis=1)


def _layer_weights(w_in, w_a2, b_a, q_norm_g, k_norm_g, w_pa, w_pb, w_o):
    offs = np.concatenate([[0], np.cumsum(IN_SPLITS)])
    gq, gk, gv, gg, glr, mq, mk, mv, ga, gb = (w_in[:, offs[i]:offs[i + 1]] for i in range(len(IN_SPLITS)))
    glr = jnp.pad(glr, ((0, 0), (0, LANES - GLA_RANK)))
    a2 = jnp.pad(w_a2, ((0, LANES - GLA_RANK), (0, 0)))
    b16 = lambda a: a.astype(BF16)
    return dict(
        gq=b16(gq), gk=b16(gk), gv=b16(gv), gg=b16(gg), glr=b16(glr), a2=b16(a2), ba=b_a.reshape(1, GLA_DK),
        mq=b16(mq), mk=b16(mk), mv=b16(mv),
        qg=jnp.tile(q_norm_g, MOBA_HEADS).reshape(1, MOBA_W), kg=jnp.tile(k_norm_g, MOBA_HEADS).reshape(1, MOBA_W),
        ga=b16(ga), gb=b16(gb), pa=b16(w_pa), pb=b16(w_pb), o=b16(w_o))


def kernel(x_prompt, x_sample, cache_k, cache_v, state_gla, page_table, ffn1_g, ffn1_wg, ffn1_wu, ffn1_wd, mix_g, w_in, w_a2, b_a, gla_norm_g, q_norm_g, k_norm_g, w_pa, w_pb, w_o, ffn2_g, ffn2_wg, ffn2_wu, ffn2_wd):
    batch, seq, _ = x_prompt.shape
    n_dec, dec_seq, _ = x_sample.shape
    n_pages = page_table.shape[1]
    depth = w_in.shape[0]
    past_len = n_pages * PAGE_SIZE
    assert dec_seq == 1 and seq % MOBA_BLOCK == 0 and past_len % MOBA_BLOCK == 0
    assert past_len // MOBA_BLOCK >= MOBA_TOPK
    assert (seq // MOBA_BLOCK) % MOBA_GROUP == 0 and MOBA_GROUP == MOBA_KV_BLOCKS

    yp = x_prompt.reshape(batch * seq, D_MODEL)
    ys = x_sample.reshape(n_dec, D_MODEL)
    rope_p = _rope_tables(jnp.arange(seq, dtype=jnp.int32))
    rope_s = _rope_tables(jnp.full((n_dec,), past_len, jnp.int32))
    pt_flat = page_table.reshape(-1)
    row = lambda a: a.reshape(1, -1)
    outs = [[] for _ in range(6)]
    for l in range(depth):
        w = _layer_weights(w_in[l], w_a2[l], b_a[l], q_norm_g[l], k_norm_g[l], w_pa[l], w_pb[l], w_o[l])
        ffn1 = (row(ffn1_g[l]), ffn1_wg[l].astype(BF16), ffn1_wu[l].astype(BF16), ffn1_wd[l].astype(BF16))
        ffn2 = (row(ffn2_g[l]), ffn2_wg[l].astype(BF16), ffn2_wu[l].astype(BF16), ffn2_wd[l].astype(BF16))
        norm_g = row(gla_norm_g[l])

        x1 = _ffn(yp, *ffn1)
        gq, gk, gv, gg, la = _gla_proj(x1, row(mix_g[l]), w)
        o_gla, s_prompt = _gla_prompt(gq, gk, la, gv, gg, norm_g, batch, seq)
        _, k, v, qe = _moba_proj(x1, row(mix_g[l]), w, *rope_p, n_seq=batch, select=True)
        o_moba, page_sums = _moba_attend(qe, k, v, batch, seq, cache_k, l, pt_flat)
        yp = _ffn(_merge(x1, row(mix_g[l]), o_gla, o_moba, w), *ffn2)
        outs[0].append(k.reshape(batch, seq, MOBA_HEADS, MOBA_HD))
        outs[1].append(v.reshape(batch, seq, MOBA_HEADS, MOBA_HD))
        outs[2].append(s_prompt)

        x1 = _ffn(ys, *ffn1)
        gq, gk, gv, gg, la = _gla_proj(x1, row(mix_g[l]), w)
        q, k, v, _ = _moba_proj(x1, row(mix_g[l]), w, *rope_s, n_seq=1, select=False)
        o_gla, s_sample = _gla_sample(gq, gk, la, gv, gg, state_gla[l], norm_g)
        sel = _sample_select(q, page_sums.reshape(n_dec, n_pages * PAGE_SIZE // MOBA_BLOCK, MOBA_W))
        o_moba = _sample_attend(q, k, v, cache_k, cache_v, l, pt_flat, sel, n_pages)
        ys = _ffn(_merge(x1, row(mix_g[l]), o_gla, o_moba, w), *ffn2)
        outs[3].append(k.reshape(n_dec, 1, MOBA_HEADS, MOBA_HD))
        outs[4].append(v.reshape(n_dec, 1, MOBA_HEADS, MOBA_HD))
        outs[5].append(s_sample)

    return (yp.reshape(batch, seq, D_MODEL), ys.reshape(n_dec, 1, D_MODEL),
            jnp.stack(outs[0]), jnp.stack(outs[1]), jnp.stack(outs[2]),
            jnp.stack(outs[3]), jnp.stack(outs[4]), jnp.stack(outs[5]))
```

```python
import functools

import numpy as np
import jax
import jax.numpy as jnp
from jax import lax
from jax.experimental import pallas as pl
from jax.experimental.pallas import tpu as pltpu

F32 = jnp.float32
BF16 = jnp.bfloat16

D_MODEL = 1024
D_FF = 2816
GLA_HEADS = 4
GLA_HDK = 128
GLA_HDV = 256
GLA_DK = GLA_HEADS * GLA_HDK
GLA_DV = GLA_HEADS * GLA_HDV
GLA_RANK = 16
GLA_TAU = 16.0
MOBA_HEADS = 8
MOBA_HD = 128
MOBA_W = MOBA_HEADS * MOBA_HD
MOBA_BLOCK = 256
MOBA_TOPK = 3
ROPE_THETA = 10000.0
EPS = 1e-6
PAGE_SIZE = 128
IN_SPLITS = (GLA_DK, GLA_DK, GLA_DV, GLA_DV, GLA_RANK, MOBA_W, MOBA_W, MOBA_W, D_MODEL, D_MODEL)

LANES = 128
SUBLANES = 8
VMEM_BYTES = 64 * 1024 * 1024

FF_CHUNK = 256
FFN_ROWS = 512
PROJ_ROWS = 512
GLA_CHUNK = 128
GLA_CHUNKS_PER_STEP = 4
MOBA_GROUP = 4
MOBA_KV_BLOCKS = 4
MOBA_VISITS = 3
MASK_BIG = 2.0 ** 100
NEG_INIT = -1.0e38
LOG2_E = 1.4426950408889634


def _vmem_limit(nbytes):
    return int(min(VMEM_BYTES - (4 << 20), max(nbytes, 16 << 20)))


def _resident(shape):
    return pl.BlockSpec(shape, lambda *_: (0,) * len(shape), pipeline_mode=pl.Buffered(1))


def _dot(a, b):
    return jnp.dot(a, b, preferred_element_type=F32)


def _dot_nt(a, b):
    return lax.dot_general(a, b, (((1,), (1,)), ((), ())), preferred_element_type=F32)


def _rms_norm(x, g):
    return x * lax.rsqrt(jnp.mean(x * x, axis=-1, keepdims=True) + EPS) * g


def _log_sigmoid(z):
    return jnp.minimum(z, 0.0) - jnp.log1p(jnp.exp(-jnp.abs(z)))


def _split3(x):
    hi = x.astype(BF16)
    r = x - hi.astype(F32)
    mid = r.astype(BF16)
    lo = (r - mid.astype(F32)).astype(BF16)
    return hi, mid, lo


def _ffn_body(x_ref, g_ref, wg_ref, wu_ref, wd_ref, o_ref):
    x = x_ref[...]
    h = _rms_norm(x, g_ref[...]).astype(BF16)
    acc = jnp.zeros_like(x)
    for c in range(D_FF // FF_CHUNK):
        sl = slice(c * FF_CHUNK, (c + 1) * FF_CHUNK)
        a = _dot(h, wg_ref[:, sl])
        u = _dot(h, wu_ref[:, sl])
        act = (a * jax.nn.sigmoid(a) * u).astype(BF16)
        acc = acc + _dot(act, wd_ref[sl, :])
    o_ref[...] = x + 0.5 * acc


def _ffn(x, g, wg, wu, wd):
    m = x.shape[0]
    tm = min(FFN_ROWS, m)
    row = pl.BlockSpec((tm, D_MODEL), lambda i: (i, 0))
    need = 3 * D_MODEL * D_FF * 2 + 4 * tm * D_MODEL * 4 + 6 * tm * D_MODEL * 4
    return pl.pallas_call(
        _ffn_body,
        grid=(m // tm,),
        in_specs=[row, _resident((1, D_MODEL)), _resident((D_MODEL, D_FF)), _resident((D_MODEL, D_FF)),
                  _resident((D_FF, D_MODEL))],
        out_specs=row,
        out_shape=jax.ShapeDtypeStruct((m, D_MODEL), F32),
        compiler_params=pltpu.CompilerParams(dimension_semantics=("parallel",),
                                             vmem_limit_bytes=_vmem_limit(need)),
        name="swiglu_half_step",
    )(x, g, wg, wu, wd)


def _gla_proj_body(x_ref, g_ref, wgq, wgk, wgv, wgg, wglr, wa2, ba, gq_o, gk_o, gv_o, gg_o, la_o):
    h = _rms_norm(x_ref[...], g_ref[...]).astype(BF16)
    gq_o[...] = _dot(h, wgq[...]) * (GLA_HDK ** -0.5)
    gk_o[...] = _dot(h, wgk[...])
    gv_o[...] = _dot(h, wgv[...]).astype(BF16)
    gg_o[...] = _dot(h, wgg[...]).astype(BF16)
    glr = _dot(h, wglr[...]).astype(BF16)
    la_o[...] = _log_sigmoid(_dot(glr, wa2[...]) + ba[...]) * (1.0 / GLA_TAU)


def _gla_proj(x, mix_g, w):
    m = x.shape[0]
    tm = min(PROJ_ROWS, m)

    def rows(width):
        return pl.BlockSpec((tm, width), lambda i: (i, 0))

    weights = [w["gq"], w["gk"], w["gv"], w["gg"], w["glr"], w["a2"], w["ba"]]
    out_shape = [
        jax.ShapeDtypeStruct((m, GLA_DK), F32), jax.ShapeDtypeStruct((m, GLA_DK), F32),
        jax.ShapeDtypeStruct((m, GLA_DV), BF16), jax.ShapeDtypeStruct((m, GLA_DV), BF16),
        jax.ShapeDtypeStruct((m, GLA_DK), F32),
    ]
    w_bytes = sum(int(a.size) * a.dtype.itemsize for a in weights)
    need = w_bytes + 2 * tm * (3 * GLA_DK * 4 + 2 * GLA_DV * 2) + 2 * tm * D_MODEL * 4 + 6 * tm * GLA_DV * 4
    return pl.pallas_call(
        _gla_proj_body,
        grid=(m // tm,),
        in_specs=[rows(D_MODEL), _resident((1, D_MODEL))] + [_resident(a.shape) for a in weights],
        out_specs=[rows(GLA_DK), rows(GLA_DK), rows(GLA_DV), rows(GLA_DV), rows(GLA_DK)],
        out_shape=out_shape,
        compiler_params=pltpu.CompilerParams(dimension_semantics=("parallel",),
                                             vmem_limit_bytes=_vmem_limit(need)),
        name="gla_input_projection",
    )(x, mix_g, *weights)


def _moba_proj_body(x_ref, g_ref, wmq, wmk, wmv, qg, kg, rc, rs, q_o, k_o, v_o, qe_o, ksum_ref, *,
                    select, tiles_per_seq):
    tile = pl.program_id(0) % tiles_per_seq
    if select:
        @pl.when(tile == 0)
        def _():
            ksum_ref[...] = jnp.zeros_like(ksum_ref)

    h = _rms_norm(x_ref[...], g_ref[...]).astype(BF16)
    mq = _dot(h, wmq[...])
    mk = _dot(h, wmk[...])
    v_o[...] = _dot(h, wmv[...])
    cos = rc[...]
    sin = rs[...]
    for hd in range(MOBA_HEADS):
        sl = slice(hd * MOBA_HD, (hd + 1) * MOBA_HD)
        qh = _rms_norm(mq[:, sl], qg[:, sl])
        q_o[:, sl] = qh * cos + pltpu.roll(qh, MOBA_HD // 2, 1) * sin
        kh = _rms_norm(mk[:, sl], kg[:, sl])
        k_o[:, sl] = kh * cos + pltpu.roll(kh, MOBA_HD // 2, 1) * sin
    if not select:
        qe_o[...] = jnp.zeros_like(qe_o)
        return

    tq = MOBA_BLOCK
    nb = ksum_ref.shape[0]
    eye =(lax.broadcasted_iota(jnp.int32, (tq, tq), 0) == lax.broadcasted_iota(jnp.int32, (tq, tq), 1))
    eye = jnp.where(eye, 1.0, 0.0).astype(BF16)
    for c in range(q_o.shape[0] // tq):
        rows = slice(c * tq, (c + 1) * tq)
        own_i = tile * (q_o.shape[0] // tq) + c
        own = own_i.astype(F32)
        ksum_ref[pl.ds(own_i, 1), :] = jnp.sum(k_o[rows, :], axis=0, keepdims=True)
        gates = []
        for hd in range(MOBA_HEADS):
            sl = slice(hd * MOBA_HD, (hd + 1) * MOBA_HD)
            q = q_o[rows, sl]
            means = ksum_ref[:, sl] * (1.0 / MOBA_BLOCK)
            q_hi = q.astype(BF16)
            q_lo = (q - q_hi.astype(F32)).astype(BF16)
            m_hi = means.astype(BF16)
            m_lo = (means - m_hi.astype(F32)).astype(BF16)
            gates.append(_dot_nt(m_hi, q_hi) + _dot_nt(m_lo, q_hi) + _dot_nt(m_hi, q_lo))
            qe_o[0, hd, rows, :MOBA_HD] = (q * (MOBA_HD ** -0.5 * LOG2_E)).astype(BF16)
        gate = jnp.concatenate(gates, axis=1)
        blk = lax.broadcasted_iota(jnp.int32, gate.shape, 0).astype(F32)
        gate = jnp.where(blk < own, gate, -jnp.inf)
        mask = jnp.where(blk == own, 0.0, -1.0)
        for _ in range(MOBA_TOPK):
            best = jnp.max(gate, axis=0, keepdims=True)
            first = jnp.min(jnp.where(gate == best, blk, float(nb)), axis=0, keepdims=True)
            first = jnp.where(best > -jnp.inf, first, -1.0)
            pick = blk == first
            mask = jnp.where(pick, 0.0, mask)
            gate = jnp.where(pick, -jnp.inf, gate)
        mask = jnp.concatenate([mask, jnp.full((LANES - nb, mask.shape[1]), -1.0, F32)], axis=0).astype(BF16)
        for hd in range(MOBA_HEADS):
            qe_o[0, hd, rows, MOBA_HD:] = _dot_nt(eye, mask[:, hd * tq:(hd + 1) * tq]).astype(BF16)


def _moba_proj(x, mix_g, w, rope_cos, rope_sin, n_seq, select):
    m = x.shape[0]
    seq = m // n_seq
    tm = min(PROJ_ROWS, seq if select else m)
    n_rope = rope_cos.shape[0] // tm
    per_seq = seq // tm
    nb = max(seq // MOBA_BLOCK, SUBLANES)
    rows = pl.BlockSpec((tm, MOBA_W), lambda i: (i, 0))
    rope_spec = pl.BlockSpec((tm, MOBA_HD), lambda i: (i % n_rope, 0))
    weights = [w["mq"], w["mk"], w["mv"], w["qg"], w["kg"]]
    w_bytes = sum(int(a.size) * a.dtype.itemsize for a in weights)
    out_bytes = tm * MOBA_W * 3 * 4 + tm * MOBA_HEADS * 2 * MOBA_HD * 2
    need = w_bytes + 2 * out_bytes + 2 * tm * D_MODEL * 4 + 8 * tm * MOBA_W * 4 + (4 << 20)
    return pl.pallas_call(
        functools.partial(_moba_proj_body, select=select, tiles_per_seq=per_seq),
        grid=(m // tm,),
        in_specs=[pl.BlockSpec((tm, D_MODEL), lambda i: (i, 0)), _resident((1, D_MODEL))]
                 + [_resident(a.shape) for a in weights] + [rope_spec, rope_spec],
        out_specs=[rows, rows, rows,
                   pl.BlockSpec((1, MOBA_HEADS, tm, 2 * MOBA_HD), lambda i: (i // per_seq, 0, i % per_seq, 0))],
        out_shape=[jax.ShapeDtypeStruct((m, MOBA_W), F32)] * 3
                  + [jax.ShapeDtypeStruct((n_seq if select else 1, MOBA_HEADS, seq if select else m, 2 * MOBA_HD), BF16)],
        scratch_shapes=[pltpu.VMEM((nb, MOBA_W), F32)],
        compiler_params=pltpu.CompilerParams(dimension_semantics=("arbitrary",),
                                             vmem_limit_bytes=_vmem_limit(need)),
        name="moba_input_projection",
    )(x, mix_g, *weights, rope_cos, rope_sin)


def _gla_tables():
    c = GLA_CHUNK
    t = np.arange(c)
    le = t[None, :] <= t[:, None]
    gt = t[None, :] > t[:, None]
    masks = []
    s = c // 2
    while s >= 1:
        same = (t // (2 * s))[:, None] == (t // (2 * s))[None, :]
        right = (t % (2 * s)) >= s
        masks.append(same & right[:, None] & ~right[None, :])
        s //= 2
    masks.append(np.eye(c, dtype=bool))
    tail = np.concatenate([gt.T, np.ones((c, c), dtype=bool)], axis=1)
    return le.astype(np.float32), np.stack(masks).astype(np.float32), tail.astype(np.float32)


def _level_reference(b, s):
    c, dk = b.shape
    if s >= SUBLANES:
        blocks = b.reshape(c // (2 * s), 2 * s, dk)
        return jnp.broadcast_to(blocks[:, s - 1:s, :], blocks.shape).reshape(c, dk)
    rows8 = b.reshape(c // SUBLANES, SUBLANES, dk)
    sub = lax.broadcasted_iota(jnp.int32, rows8.shape, 1)
    ref = jnp.broadcast_to(rows8[:, SUBLANES - s - 1:SUBLANES - s, :], rows8.shape)
    for first in range(SUBLANES - 4 * s, -1, -2 * s):
        ref = jnp.where(sub < first + 2 * s, jnp.broadcast_to(rows8[:, first + s - 1:first + s, :], rows8.shape), ref)
    return ref.reshape(c, dk)


def _gla_body(q_ref, k_ref, la_ref, v_ref, gg_ref, ng_ref, le_ref, masks_ref, tail_ref, o_ref, s_ref):
    c = GLA_CHUNK
    n_lvl = masks_ref.shape[0] - 1

    @pl.when(pl.program_id(1) == 0)
    def _():
        s_ref[...] = jnp.zeros_like(s_ref)

    le = le_ref[...]
    tail = tail_ref[...]
    for chunk, hd in [(cc, hh) for cc in range(q_ref.shape[0] // c) for hh in range(GLA_HEADS)]:
        rows = slice(chunk * c, (chunk + 1) * c)
        ks = slice(hd * GLA_HDK, (hd + 1) * GLA_HDK)
        vs = slice(hd * GLA_HDV, (hd + 1) * GLA_HDV)
        q = q_ref[rows, ks]
        k = k_ref[rows, ks]
        v = v_ref[rows, vs]
        state = s_ref[0, hd]
        b = sum(_dot(le, p) for p in _split3(la_ref[rows, ks])) * LOG2_E
        et = jnp.exp(sum(_dot(p, tail) for p in _split3(la_ref[rows, ks].T)))
        out = _dot((q * jnp.exp2(b)).astype(BF16), state.astype(BF16))
        attn = masks_ref[n_lvl] * _dot_nt(q.astype(BF16), k.astype(BF16))
        for lv in range(n_lvl):
            e = jnp.exp2(-jnp.abs(b - _level_reference(b, c >> (lv + 1))))
            attn = attn + masks_ref[lv] * _dot_nt((q * e).astype(BF16), (k * e).astype(BF16))
        out = out + _dot(attn.astype(BF16), v)
        k_dec = (k.T * et[:, :c]).astype(BF16)
        decay = et[:, c:]
        s_ref[0, hd] = jnp.concatenate([state[:, :c] * decay, state[:, c:] * decay], axis=1) + _dot(k_dec, v)
        gate = gg_ref[rows, vs].astype(F32)
        o_ref[rows, vs] = (_rms_norm(out, ng_ref[...]) * (gate * jax.nn.sigmoid(gate))).astype(BF16)


def _gla_prompt(gq, gk, la, gv, gg, norm_g, batch, seq):
    c = GLA_CHUNK * GLA_CHUNKS_PER_STEP
    n_chunks = seq // c
    le, masks, tail = (jnp.asarray(a, BF16 if i != 1 else F32) for i, a in enumerate(_gla_tables()))

    def rows(width):
        return pl.BlockSpec((c, width), lambda b, t: (b * n_chunks + t, 0))

    need = 2 * (3 * c * GLA_DK * 4 + 3 * c * GLA_DV * 2) + 4 * GLA_DK * GLA_HDV * 4 + (8 << 20)
    return pl.pallas_call(
        _gla_body,
        grid=(batch, n_chunks),
        in_specs=[rows(GLA_DK), rows(GLA_DK), rows(GLA_DK), rows(GLA_DV), rows(GLA_DV),
                  _resident((1, GLA_HDV)), _resident(le.shape), _resident(masks.shape), _resident(tail.shape)],
        out_specs=[rows(GLA_DV),
                   pl.BlockSpec((1, GLA_HEADS, GLA_HDK, GLA_HDV), lambda b, t: (b, 0, 0, 0))],
        out_shape=[jax.ShapeDtypeStruct((batch * seq, GLA_DV), BF16),
                   jax.ShapeDtypeStruct((batch, GLA_HEADS, GLA_HDK, GLA_HDV), F32)],
        compiler_params=pltpu.CompilerParams(dimension_semantics=("parallel", "arbitrary"),
                                             vmem_limit_bytes=_vmem_limit(need)),
        name="gla_prompt_chunks",
    )(gq, gk, la, gv, gg, norm_g, le, masks, tail)


def _moba_attend_body(pt_ref, qe_ref, k_ref, v_ref, *refs):
    pages, (o_ref, ksum_ref, acc_ref, m_ref) = refs[:-4], refs[-4:]
    j = pl.program_id(2)
    tb = MOBA_BLOCK
    tkv = k_ref.shape[0]
    n_groups = qe_ref.shape[2] // (tb * MOBA_GROUP)

    @pl.when(j == 0)
    def _():
        acc_ref[...] = jnp.zeros_like(acc_ref)
        m_ref[...] = jnp.full_like(m_ref, NEG_INIT)

    k = k_ref[...].astype(BF16)
    lane = lax.broadcasted_iota(jnp.int32, (tkv, MOBA_HD), 1)
    key_block = j * (tkv // tb) + lax.broadcasted_iota(jnp.int32, (tkv, MOBA_HD), 0) // tb
    k_ext = jnp.concatenate([k, jnp.where(lane == key_block, MASK_BIG, 0.0).astype(BF16)], axis=1)
    v_ext = jnp.concatenate([v_ref[...].astype(BF16), jnp.ones((tkv, MOBA_HD), BF16)], axis=1)

    def update(rows, s):
        m_old = m_ref[rows, :]
        m_new = jnp.maximum(m_old, jnp.max(s, axis=1, keepdims=True))
        p = jnp.exp2(s - jnp.concatenate([m_new] * (tkv // MOBA_HD), axis=1)).astype(BF16)
        alpha = jnp.exp2(m_old - m_new)
        acc_new = jnp.concatenate([alpha, alpha], axis=1) * acc_ref[rows, :] + _dot(p, v_ext)
        m_ref[rows, :] = m_new
        acc_ref[rows, :] = acc_new
        return acc_new

    rows_per_group = tb * MOBA_GROUP

    def group_rows(g):
        return pl.ds(pl.multiple_of(g * rows_per_group, rows_per_group), rows_per_group)

    def scores(g):
        return _dot_nt(qe_ref[0, 0, group_rows(g), :], k_ext)

    first = (j * tkv) // rows_per_group
    n_later = n_groups - 1 - first
    n_extra = n_later % MOBA_VISITS

    def page_sum(page_ref):
        return jnp.sum(jnp.sum(page_ref[...].reshape(4, PAGE_SIZE // 4, MOBA_HEADS, MOBA_HD), axis=1), axis=0)

    def first_groups(extra):
        lower = (lax.broadcasted_iota(jnp.int32, (tb, tb), 1) <= lax.broadcasted_iota(jnp.int32, (tb, tb), 0))
        n_parts = 1 if extra else 2
        part = rows_per_group // n_parts
        visits = []
        for hh in range(n_parts):
            rows = pl.ds(pl.multiple_of(first * rows_per_group + hh * part, part), part)
            s_own = _dot_nt(qe_ref[0, 0, rows, :], k_ext)
            tiles = []
            for t2 in range(part // tb):
                t = hh * (part // tb) + t2
                r, c = slice(t2 * tb, (t2 + 1) * tb), slice(t * tb, (t + 1) * tb)
                diag = jnp.where(lower, s_own[r, c], -MASK_BIG)
                parts = (([s_own[r, :t * tb]] if t else []) + [diag]
                         + ([s_own[r, (t + 1) * tb:]] if (t + 1) * tb < tkv else []))
                tiles.append(jnp.concatenate(parts, axis=1))
            visits.append((rows, jnp.concatenate(tiles, axis=0)))
        visits += [(group_rows(first + u), scores(first + u)) for u in range(1, extra + 1)]
        for rows, s in visits:
            update(rows, s)
        acc = acc_ref[pl.ds(pl.multiple_of(j * tkv, tkv), tkv), :]
        o_ref[...] = (acc[:, :MOBA_HD] / acc[:, MOBA_HD:]).astype(BF16)
        per_block = MOBA_BLOCK // PAGE_SIZE
        for blk in range(len(pages) // per_block):
            tot = page_sum(pages[blk * per_block])
            for p in range(1, per_block):
                tot = tot + page_sum(pages[blk * per_block + p])
            ksum_ref[0, blk] = tot

    for extra in range(MOBA_VISITS):
        pl.when(n_extra == extra)(functools.partial(first_groups, extra))

    def body(i, carry):
        g0 = first + 1 + n_extra + MOBA_VISITS * i
        ss = [scores(g0 + u) for u in range(MOBA_VISITS)]
        for u, s in enumerate(ss):
            update(group_rows(g0 + u), s)
        return carry

    lax.fori_loop(0, n_later // MOBA_VISITS, body, 0)


def _moba_attend(qe, k, v, batch, seq, cache, layer, page_table_flat):
    tkv = MOBA_BLOCK * MOBA_KV_BLOCKS
    nb = seq // tkv
    n_steps = batch * MOBA_HEADS * nb
    per_block = MOBA_BLOCK // PAGE_SIZE
    pages_per_step = page_table_flat.shape[0] // n_steps
    assert pages_per_step * n_steps == page_table_flat.shape[0] and pages_per_step % per_block == 0
    kv = pl.BlockSpec((tkv, MOBA_HD), lambda b, h, j, pt: (b * nb + j, h))

    def step(b, h, j):
        return (b * MOBA_HEADS + h) * nb + j

    def page_spec(i):
        return pl.BlockSpec((None, None, PAGE_SIZE, MOBA_HEADS, MOBA_HD),
                            lambda b, h, j, pt: (layer, pt[step(b, h, j) * pages_per_step + i], 0, 0, 0))

    blocks_per_step = pages_per_step // per_block
    need = (2 * seq * 2 * MOBA_HD * 2 + 3 * seq * MOBA_HD * 4 + 2 * pages_per_step * PAGE_SIZE * MOBA_W * 4
            + (16 << 20))
    out, sums = pl.pallas_call(
        _moba_attend_body,
        grid_spec=pltpu.PrefetchScalarGridSpec(
            num_scalar_prefetch=1,
            grid=(batch, MOBA_HEADS, nb),
            in_specs=[pl.BlockSpec((1, 1, seq, 2 * MOBA_HD), lambda b, h, j, pt: (b, h, 0, 0)), kv, kv]
                     + [page_spec(i) for i in range(pages_per_step)],
            out_specs=[kv, pl.BlockSpec((1, blocks_per_step, MOBA_HEADS, MOBA_HD),
                                        lambda b, h, j, pt: (step(b, h, j), 0, 0, 0))],
            scratch_shapes=[pltpu.VMEM((seq, 2 * MOBA_HD), F32), pltpu.VMEM((seq, MOBA_HD), F32)],
        ),
        out_shape=[jax.ShapeDtypeStruct((batch * seq, MOBA_W), BF16),
                   jax.ShapeDtypeStruct((n_steps, blocks_per_step, MOBA_HEADS, MOBA_HD), F32)],
        compiler_params=pltpu.CompilerParams(dimension_semantics=("parallel", "parallel", "arbitrary"),
                                             vmem_limit_bytes=_vmem_limit(need)),
        name="moba_prompt_attention",
    )(page_table_flat, qe, k, v, *([cache] * pages_per_step))
    return out, sums


def _merge_body(x_ref, g_ref, oa_ref, ob_ref, wga, wgb, wpa, wpb, wo, o_ref):
    x = x_ref[...]
    h = _rms_norm(x, g_ref[...]).astype(BF16)
    mix = jax.nn.sigmoid(_dot(h, wga[...])) * _dot(oa_ref[...], wpa[...])
    mix = mix + jax.nn.sigmoid(_dot(h, wgb[...])) * _dot(ob_ref[...], wpb[...])
    o_ref[...] = x + _dot(mix.astype(BF16), wo[...])


def _merge(x, mix_g, o_gla, o_moba, w):
    m = x.shape[0]
    tm = min(FFN_ROWS, m)
    row32 = pl.BlockSpec((tm, D_MODEL), lambda i: (i, 0))
    sq = _resident((D_MODEL, D_MODEL))
    need = 5 * D_MODEL * D_MODEL * 2 + 4 * tm * D_MODEL * 4 + 4 * tm * D_MODEL * 2 + 6 * tm * D_MODEL * 4
    return pl.pallas_call(
        _merge_body,
        grid=(m // tm,),
        in_specs=[row32, _resident((1, D_MODEL)), row32, row32, sq, sq, sq, sq, sq],
        out_specs=row32,
        out_shape=jax.ShapeDtypeStruct((m, D_MODEL), F32),
        compiler_params=pltpu.CompilerParams(dimension_semantics=("parallel",),
                                             vmem_limit_bytes=_vmem_limit(need)),
        name="gated_merge_projection",
    )(x, mix_g, o_gla, o_moba, w["ga"], w["gb"], w["pa"], w["pb"], w["o"])


def _gla_sample_body(q_ref, k_ref, la_ref, v_ref, gg_ref, s_ref, ng_ref, o_ref, so_ref):
    def column(ref, hd):
        row = ref[0, :, hd * GLA_HDK:(hd + 1) * GLA_HDK]
        col = jnp.broadcast_to(row, (GLA_HDK, GLA_HDK)).T
        return jnp.concatenate([col] * (GLA_HDV // GLA_HDK), axis=1)

    for hd in range(GLA_HEADS):
        vs = slice(hd * GLA_HDV, (hd + 1) * GLA_HDV)
        v = v_ref[0, :, vs].astype(F32)
        s_new = jnp.exp(column(la_ref, hd)) * s_ref[0, hd] + column(k_ref, hd) * v
        so_ref[0, hd] = s_new
        out = jnp.sum(column(q_ref, hd) * s_new, axis=0, keepdims=True)
        gate = gg_ref[0, :, vs].astype(F32)
        o_ref[0, :, vs] = (_rms_norm(out, ng_ref[...]) * (gate * jax.nn.sigmoid(gate))).astype(BF16)


def _gla_sample(gq, gk, la, gv, gg, state, norm_g):
    n = gq.shape[0]
    assert GLA_HDV % GLA_HDK == 0
    key_row = pl.BlockSpec((1, 1, GLA_DK), lambda i: (i, 0, 0))
    val_row = pl.BlockSpec((1, 1, GLA_DV), lambda i: (i, 0, 0))
    st = pl.BlockSpec((1, GLA_HEADS, GLA_HDK, GLA_HDV), lambda i: (i, 0, 0, 0))
    as_row = lambda a: a.reshape(n, 1, a.shape[-1])
    o, s_new = pl.pallas_call(
        _gla_sample_body,
        grid=(n,),
        in_specs=[key_row, key_row, key_row, val_row, val_row, st, _resident((1, GLA_HDV))],
        out_specs=[val_row, st],
        out_shape=[jax.ShapeDtypeStruct((n, 1, GLA_DV), BF16),
                   jax.ShapeDtypeStruct((n, GLA_HEADS, GLA_HDK, GLA_HDV), F32)],
        compiler_params=pltpu.CompilerParams(dimension_semantics=("parallel",),
                                             vmem_limit_bytes=_vmem_limit(16 << 20)),
        name="gla_sample_step",
    )(as_row(gq), as_row(gk), as_row(la), as_row(gv), as_row(gg), state, norm_g)
    return o.reshape(n, GLA_DV), s_new


def _sample_select_body(q_ref, ksum_ref, sel_ref):
    nb = ksum_ref.shape[1]
    q = q_ref[0]
    lane = lax.broadcasted_iota(jnp.int32, (nb, LANES), 1)
    row = lax.broadcasted_iota(jnp.int32, (nb, LANES), 0).astype(F32)
    gate = jnp.full((nb, LANES), -jnp.inf, F32)
    for hd in range(MOBA_HEADS):
        sl = slice(hd * MOBA_HD, (hd + 1) * MOBA_HD)
        g = jnp.sum(ksum_ref[0, :, sl] * (1.0 / MOBA_BLOCK) * q[:, sl], axis=1, keepdims=True)
        gate = jnp.where(lane == hd, g, gate)
    out_row = lax.broadcasted_iota(jnp.int32, (SUBLANES, LANES), 0)
    out = jnp.zeros((SUBLANES, LANES), jnp.int32)
    for r in range(MOBA_TOPK):
        best = jnp.max(gate, axis=0, keepdims=True)
        first = jnp.min(jnp.where(gate == best, row, float(nb)), axis=0, keepdims=True)
        out = jnp.where(out_row == r, first.astype(jnp.int32), out)
        gate = jnp.where(row == first, -jnp.inf, gate)
    sel_ref[0] = out


def _sample_select(q, ksum):
    n, nb = ksum.shape[0], ksum.shape[1]
    sel = pl.pallas_call(
        _sample_select_body,
        grid=(n,),
        in_specs=[pl.BlockSpec((1, 1, MOBA_W), lambda i: (i, 0, 0)),
                  pl.BlockSpec((1, nb, MOBA_W), lambda i: (i, 0, 0))],
        out_specs=pl.BlockSpec((1, SUBLANES, LANES), lambda i: (i, 0, 0)),
        out_shape=jax.ShapeDtypeStruct((n, SUBLANES, LANES), jnp.int32),
        compiler_params=pltpu.CompilerParams(dimension_semantics=("parallel",)),
        name="moba_sample_select",
    )(q.reshape(n, 1, MOBA_W), ksum)
    return jnp.transpose(sel[:, :MOBA_TOPK, :MOBA_HEADS], (0, 2, 1)).reshape(-1)


def _sample_attend_body(pt_ref, sel_ref, q_ref, kn_ref, vn_ref, ck_hbm, cv_hbm, o_ref, k_buf, v_buf, sem,
                        *, layer, n_pages):
    per_block = MOBA_BLOCK // PAGE_SIZE
    n_sel = MOBA_TOPK * per_block
    n_slots = MOBA_HEADS * n_sel
    seq = pl.program_id(0)
    scale = MOBA_HD ** -0.5

    def page_copies(s, hd, i):
        blk = sel_ref[(s * MOBA_HEADS + hd) * MOBA_TOPK + i // per_block]
        page = pt_ref[s * n_pages + blk * per_block + i % per_block]
        slot = (s % 2) * n_slots + hd * n_sel + i
        return (pltpu.make_async_copy(ck_hbm.at[layer, page, :, hd, :], k_buf.at[slot], sem.at[0, slot]),
                pltpu.make_async_copy(cv_hbm.at[layer, page, :, hd, :], v_buf.at[slot], sem.at[1, slot]))

    def start_fetch(s):
        for hd in range(MOBA_HEADS):
            for i in range(n_sel):
                for cp in page_copies(s, hd, i):
                    cp.start()

    @pl.when(seq == 0)
    def _():
        start_fetch(seq)

    @pl.when(seq + 1 < pl.num_programs(0))
    def _():
        start_fetch(seq + 1)

    for hd in range(MOBA_HEADS):
        for i in range(n_sel):
            for cp in page_copies(seq, hd, i):
                cp.wait()

    base = (seq % 2) * n_slots
    for hd in range(MOBA_HEADS):
        sl = slice(hd * MOBA_HD, (hd + 1) * MOBA_HD)
        q = q_ref[0, :, sl]
        s_new = jnp.sum(kn_ref[0, :, sl] * q, axis=1, keepdims=True) * scale
        scores = [jnp.sum(k_buf[base + hd * n_sel + i] * q, axis=1, keepdims=True) * scale for i in range(n_sel)]
        top = s_new
        for s in scores:
            top = jnp.maximum(top, jnp.max(s, axis=0, keepdims=True))
        p_new = jnp.exp(s_new - top)
        denom = p_new
        acc = p_new * vn_ref[0, :, sl]
        for i, s in enumerate(scores):
            p = jnp.exp(s - top)
            denom = denom + jnp.sum(p, axis=0, keepdims=True)
            acc = acc + jnp.sum(p * v_buf[base + hd * n_sel + i], axis=0, keepdims=True)
        o_ref[0, :, sl] = (acc / denom).astype(BF16)


def _sample_attend(q, k_new, v_new, cache_k, cache_v, layer, page_table_flat, sel_flat, n_pages):
    n = q.shape[0]
    n_slots = MOBA_HEADS * MOBA_TOPK * (MOBA_BLOCK // PAGE_SIZE)
    tok = pl.BlockSpec((1, 1, MOBA_W), lambda s, pt, sel: (s, 0, 0))
    hbm = pl.BlockSpec(memory_space=pl.ANY)
    as_tok = lambda a: a.reshape(n, 1, MOBA_W)
    out = pl.pallas_call(
        functools.partial(_sample_attend_body, layer=layer, n_pages=n_pages),
        grid_spec=pltpu.PrefetchScalarGridSpec(
            num_scalar_prefetch=2,
            grid=(n,),
            in_specs=[tok, tok, tok, hbm, hbm],
            out_specs=tok,
            scratch_shapes=[pltpu.VMEM((2 * n_slots, PAGE_SIZE, MOBA_HD), F32),
                            pltpu.VMEM((2 * n_slots, PAGE_SIZE, MOBA_HD), F32),
                            pltpu.SemaphoreType.DMA((2, 2 * n_slots))],
        ),
        out_shape=jax.ShapeDtypeStruct((n, 1, MOBA_W), BF16),
        compiler_params=pltpu.CompilerParams(dimension_semantics=("arbitrary",),
                                             vmem_limit_bytes=_vmem_limit(5 * n_slots * PAGE_SIZE * MOBA_HD * 4)),
        name="moba_sample_attention",
    )(page_table_flat, sel_flat, as_tok(q), as_tok(k_new), as_tok(v_new), cache_k, cache_v)
    return out.reshape(n, MOBA_W)


def _rope_tables(pos):
    half = MOBA_HD // 2
    inv = ROPE_THETA ** (-jnp.arange(half, dtype=F32) / half)
    ang = pos.astype(F32)[:, None] * inv[None, :]
    cos, sin = jnp.cos(ang), jnp.sin(ang)
    return jnp.concatenate([cos, cos], axis=1), jnp.concatenate([-sin, sin], axis=1)


def _layer_weights(w_in, w_a2, b_a, q_norm_g, k_norm_g, w_pa, w_pb, w_o):
    offs = np.concatenate([[0], np.cumsum(IN_SPLITS)])
    gq, gk, gv, gg, glr, mq, mk, mv, ga, gb = (w_in[:, offs[i]:offs[i + 1]] for i in range(len(IN_SPLITS)))
    glr = jnp.pad(glr, ((0, 0), (0, LANES - GLA_RANK)))
    a2 = jnp.pad(w_a2, ((0, LANES - GLA_RANK), (0, 0)))
    b16 = lambda a: a.astype(BF16)
    return dict(
        gq=b16(gq), gk=b16(gk), gv=b16(gv), gg=b16(gg), glr=b16(glr), a2=b16(a2), ba=b_a.reshape(1, GLA_DK),
        mq=b16(mq), mk=b16(mk), mv=b16(mv),
        qg=jnp.tile(q_norm_g, MOBA_HEADS).reshape(1, MOBA_W), kg=jnp.tile(k_norm_g, MOBA_HEADS).reshape(1, MOBA_W),
        ga=b16(ga), gb=b16(gb), pa=b16(w_pa), pb=b16(w_pb), o=b16(w_o))


def kernel(x_prompt, x_sample, cache_k, cache_v, state_gla, page_table, ffn1_g, ffn1_wg, ffn1_wu, ffn1_wd, mix_g, w_in, w_a2, b_a, gla_norm_g, q_norm_g, k_norm_g, w_pa, w_pb, w_o, ffn2_g, ffn2_wg, ffn2_wu, ffn2_wd):
    batch, seq, _ = x_prompt.shape
    n_dec, dec_seq, _ = x_sample.shape
    n_pages = page_table.shape[1]
    depth = w_in.shape[0]
    past_len = n_pages * PAGE_SIZE
    assert dec_seq == 1 and seq % MOBA_BLOCK == 0 and past_len % MOBA_BLOCK == 0
    assert past_len // MOBA_BLOCK >= MOBA_TOPK
    assert (seq // MOBA_BLOCK) % MOBA_GROUP == 0 and MOBA_GROUP == MOBA_KV_BLOCKS

    yp = x_prompt.reshape(batch * seq, D_MODEL)
    ys = x_sample.reshape(n_dec, D_MODEL)
    rope_p = _rope_tables(jnp.arange(seq, dtype=jnp.int32))
    rope_s = _rope_tables(jnp.full((n_dec,), past_len, jnp.int32))
    pt_flat = page_table.reshape(-1)
    row = lambda a: a.reshape(1, -1)
    outs = [[] for _ in range(6)]
    for l in range(depth):
        w = _layer_weights(w_in[l], w_a2[l], b_a[l], q_norm_g[l], k_norm_g[l], w_pa[l], w_pb[l], w_o[l])
        ffn1 = (row(ffn1_g[l]), ffn1_wg[l].astype(BF16), ffn1_wu[l].astype(BF16), ffn1_wd[l].astype(BF16))
        ffn2 = (row(ffn2_g[l]), ffn2_wg[l].astype(BF16), ffn2_wu[l].astype(BF16), ffn2_wd[l].astype(BF16))
        norm_g = row(gla_norm_g[l])

        x1 = _ffn(yp, *ffn1)
        gq, gk, gv, gg, la = _gla_proj(x1, row(mix_g[l]), w)
        o_gla, s_prompt = _gla_prompt(gq, gk, la, gv, gg, norm_g, batch, seq)
        _, k, v, qe = _moba_proj(x1, row(mix_g[l]), w, *rope_p, n_seq=batch, select=True)
        o_moba, page_sums = _moba_attend(qe, k, v, batch, seq, cache_k, l, pt_flat)
        yp = _ffn(_merge(x1, row(mix_g[l]), o_gla, o_moba, w), *ffn2)
        outs[0].append(k.reshape(batch, seq, MOBA_HEADS, MOBA_HD))
        outs[1].append(v.reshape(batch, seq, MOBA_HEADS, MOBA_HD))
        outs[2].append(s_prompt)

        x1 = _ffn(ys, *ffn1)
        gq, gk, gv, gg, la = _gla_proj(x1, row(mix_g[l]), w)
        q, k, v, _ = _moba_proj(x1, row(mix_g[l]), w, *rope_s, n_seq=1, select=False)
        o_gla, s_sample = _gla_sample(gq, gk, la, gv, gg, state_gla[l], norm_g)
        sel = _sample_select(q, page_sums.reshape(n_dec, n_pages * PAGE_SIZE // MOBA_BLOCK, MOBA_W))
        o_moba = _sample_attend(q, k, v, cache_k, cache_v, l, pt_flat, sel, n_pages)
        ys = _ffn(_merge(x1, row(mix_g[l]), o_gla, o_moba, w), *ffn2)
        outs[3].append(k.reshape(n_dec, 1, MOBA_HEADS, MOBA_HD))
        outs[4].append(v.reshape(n_dec, 1, MOBA_HEADS, MOBA_HD))
        outs[5].append(s_sample)

    return (yp.reshape(batch, seq, D_MODEL), ys.reshape(n_dec, 1, D_MODEL),
            jnp.stack(outs[0]), jnp.stack(outs[1]), jnp.stack(outs[2]),
            jnp.stack(outs[3]), jnp.stack(outs[4]), jnp.stack(outs[5]))
```

```python
import functools

import numpy as np
import jax
import jax.numpy as jnp
from jax import lax
from jax.experimental import pallas as pl
from jax.experimental.pallas import tpu as pltpu

F32 = jnp.float32
BF16 = jnp.bfloat16

D_MODEL = 1024
D_FF = 2816
GLA_HEADS = 4
GLA_HDK = 128
GLA_HDV = 256
GLA_DK = GLA_HEADS * GLA_HDK
GLA_DV = GLA_HEADS * GLA_HDV
GLA_RANK = 16
GLA_TAU = 16.0
MOBA_HEADS = 8
MOBA_HD = 128
MOBA_W = MOBA_HEADS * MOBA_HD
MOBA_BLOCK = 256
MOBA_TOPK = 3
ROPE_THETA = 10000.0
EPS = 1e-6
PAGE_SIZE = 128
IN_SPLITS = (GLA_DK, GLA_DK, GLA_DV, GLA_DV, GLA_RANK, MOBA_W, MOBA_W, MOBA_W, D_MODEL, D_MODEL)

LANES = 128
SUBLANES = 8
VMEM_BYTES = 64 * 1024 * 1024

FF_CHUNK = 256
FFN_ROWS = 512
PROJ_ROWS = 512
GLA_CHUNK = 128
GLA_CHUNKS_PER_STEP = 4
MOBA_GROUP = 4
MOBA_KV_BLOCKS = 4
MOBA_VISITS = 3
MASK_BIG = 2.0 ** 100
NEG_INIT = -1.0e38
LOG2_E = 1.4426950408889634


def _vmem_limit(nbytes):
    return int(min(VMEM_BYTES - (4 << 20), max(nbytes, 16 << 20)))


def _resident(shape):
    return pl.BlockSpec(shape, lambda *_: (0,) * len(shape), pipeline_mode=pl.Buffered(1))


def _dot(a, b):
    return jnp.dot(a, b, preferred_element_type=F32)


def _dot_nt(a, b):
    return lax.dot_general(a, b, (((1,), (1,)), ((), ())), preferred_element_type=F32)


def _rms_norm(x, g):
    return x * lax.rsqrt(jnp.mean(x * x, axis=-1, keepdims=True) + EPS) * g


def _log_sigmoid(z):
    return jnp.minimum(z, 0.0) - jnp.log1p(jnp.exp(-jnp.abs(z)))


def _split3(x):
    hi = x.astype(BF16)
    r = x - hi.astype(F32)
    mid = r.astype(BF16)
    lo = (r - mid.astype(F32)).astype(BF16)
    return hi, mid, lo


def _ffn_body(x_ref, g_ref, wg_ref, wu_ref, wd_ref, o_ref):
    x = x_ref[...]
    h = _rms_norm(x, g_ref[...]).astype(BF16)
    acc = jnp.zeros_like(x)
    for c in range(D_FF // FF_CHUNK):
        sl = slice(c * FF_CHUNK, (c + 1) * FF_CHUNK)
        a = _dot(h, wg_ref[:, sl])
        u = _dot(h, wu_ref[:, sl])
        act = (a * jax.nn.sigmoid(a) * u).astype(BF16)
        acc = acc + _dot(act, wd_ref[sl, :])
    o_ref[...] = x + 0.5 * acc


def _ffn(x, g, wg, wu, wd):
    m = x.shape[0]
    tm = min(FFN_ROWS, m)
    row = pl.BlockSpec((tm, D_MODEL), lambda i: (i, 0))
    need = 3 * D_MODEL * D_FF * 2 + 4 * tm * D_MODEL * 4 + 6 * tm * D_MODEL * 4
    return pl.pallas_call(
        _ffn_body,
        grid=(m // tm,),
        in_specs=[row, _resident((1, D_MODEL)), _resident((D_MODEL, D_FF)), _resident((D_MODEL, D_FF)),
                  _resident((D_FF, D_MODEL))],
        out_specs=row,
        out_shape=jax.ShapeDtypeStruct((m, D_MODEL), F32),
        compiler_params=pltpu.CompilerParams(dimension_semantics=("parallel",),
                                             vmem_limit_bytes=_vmem_limit(need)),
        name="swiglu_half_step",
    )(x, g, wg, wu, wd)


def _gla_proj_body(x_ref, g_ref, wgq, wgk, wgv, wgg, wglr, wa2, ba, gq_o, gk_o, gv_o, gg_o, la_o):
    h = _rms_norm(x_ref[...], g_ref[...]).astype(BF16)
    gq_o[...] = _dot(h, wgq[...]) * (GLA_HDK ** -0.5)
    gk_o[...] = _dot(h, wgk[...])
    gv_o[...] = _dot(h, wgv[...]).astype(BF16)
    gg_o[...] = _dot(h, wgg[...]).astype(BF16)
    glr = _dot(h, wglr[...]).astype(BF16)
    la_o[...] = _log_sigmoid(_dot(glr, wa2[...]) + ba[...]) * (1.0 / GLA_TAU)


def _gla_proj(x, mix_g, w):
    m = x.shape[0]
    tm = min(PROJ_ROWS, m)

    def rows(width):
        return pl.BlockSpec((tm, width), lambda i: (i, 0))

    weights = [w["gq"], w["gk"], w["gv"], w["gg"], w["glr"], w["a2"], w["ba"]]
    out_shape = [
        jax.ShapeDtypeStruct((m, GLA_DK), F32), jax.ShapeDtypeStruct((m, GLA_DK), F32),
        jax.ShapeDtypeStruct((m, GLA_DV), BF16), jax.ShapeDtypeStruct((m, GLA_DV), BF16),
        jax.ShapeDtypeStruct((m, GLA_DK), F32),
    ]
    w_bytes = sum(int(a.size) * a.dtype.itemsize for a in weights)
    need = w_bytes + 2 * tm * (3 * GLA_DK * 4 + 2 * GLA_DV * 2) + 2 * tm * D_MODEL * 4 + 6 * tm * GLA_DV * 4
    return pl.pallas_call(
        _gla_proj_body,
        grid=(m // tm,),
        in_specs=[rows(D_MODEL), _resident((1, D_MODEL))] + [_resident(a.shape) for a in weights],
        out_specs=[rows(GLA_DK), rows(GLA_DK), rows(GLA_DV), rows(GLA_DV), rows(GLA_DK)],
        out_shape=out_shape,
        compiler_params=pltpu.CompilerParams(dimension_semantics=("parallel",),
                                             vmem_limit_bytes=_vmem_limit(need)),
        name="gla_input_projection",
    )(x, mix_g, *weights)


def _moba_proj_body(x_ref, g_ref, wmq, wmk, wmv, qg, kg, rc, rs, q_o, k_o, v_o, qe_o, ksum_ref, *,
                    select, tiles_per_seq):
    tile = pl.program_id(0) % tiles_per_seq
    if select:
        @pl.when(tile == 0)
        def _():
            ksum_ref[...] = jnp.zeros_like(ksum_ref)

    h = _rms_norm(x_ref[...], g_ref[...]).astype(BF16)
    mq = _dot(h, wmq[...])
    mk = _dot(h, wmk[...])
    v_o[...] = _dot(h, wmv[...])
    cos = rc[...]
    sin = rs[...]
    for hd in range(MOBA_HEADS):
        sl = slice(hd * MOBA_HD, (hd + 1) * MOBA_HD)
        qh = _rms_norm(mq[:, sl], qg[:, sl])
        q_o[:, sl] = qh * cos + pltpu.roll(qh, MOBA_HD // 2, 1) * sin
        kh = _rms_norm(mk[:, sl], kg[:, sl])
        k_o[:, sl] = kh * cos + pltpu.roll(kh, MOBA_HD // 2, 1) * sin
    if not select:
        qe_o[...] = jnp.zeros_like(qe_o)
        return

    tq = MOBA_BLOCK
    nb = ksum_ref.shape[0]
    eye = (lax.broadcasted_iota(jnp.int32, (tq, tq), 0) == lax.broadcasted_iota(jnp.int32, (tq, tq), 1))
    eye = jnp.where(eye, 1.0, 0.0).astype(BF16)
    for c in range(q_o.shape[0] // tq):
        rows = slice(c * tq, (c + 1) * tq)
        own_i = tile * (q_o.shape[0] // tq) + c
        own = own_i.astype(F32)
        ksum_ref[pl.ds(own_i, 1), :] = jnp.sum(k_o[rows, :], axis=0, keepdims=True)
        gates = []
        for hd in range(MOBA_HEADS):
            sl = slice(hd * MOBA_HD, (hd + 1) * MOBA_HD)
            q = q_o[rows, sl]
            means = ksum_ref[:, sl] * (1.0 / MOBA_BLOCK)
            q_hi = q.astype(BF16)
            q_lo = (q - q_hi.astype(F32)).astype(BF16)
            m_hi = means.astype(BF16)
            m_lo = (means - m_hi.astype(F32)).astype(BF16)
            gates.append(_dot_nt(m_hi, q_hi) + _dot_nt(m_lo, q_hi) + _dot_nt(m_hi, q_lo))
            qe_o[0, hd, rows, :MOBA_HD] = (q * (MOBA_HD ** -0.5 * LOG2_E)).astype(BF16)
        gate = jnp.concatenate(gates, axis=1)
        blk = lax.broadcasted_iota(jnp.int32, gate.shape, 0).astype(F32)
        gate = jnp.where(blk < own, gate, -jnp.inf)
        mask = jnp.where(blk == own, 0.0, -1.0)
        for _ in range(MOBA_TOPK):
            best = jnp.max(gate, axis=0, keepdims=True)
            first = jnp.min(jnp.where(gate == best, blk, float(nb)), axis=0, keepdims=True)
            first = jnp.where(best > -jnp.inf, first, -1.0)
            pick = blk == first
            mask = jnp.where(pick, 0.0, mask)
            gate = jnp.where(pick, -jnp.inf, gate)
        mask = jnp.concatenate([mask, jnp.full((LANES - nb, mask.shape[1]), -1.0, F32)], axis=0).astype(BF16)
        for hd in range(MOBA_HEADS):
            qe_o[0, hd, rows, MOBA_HD:] = _dot_nt(eye, mask[:, hd * tq:(hd + 1) * tq]).astype(BF16)


def _moba_proj(x, mix_g, w, rope_cos, rope_sin, n_seq, select):
    m = x.shape[0]
    seq = m // n_seq
    tm = min(PROJ_ROWS, seq if select else m)
    n_rope = rope_cos.shape[0] // tm
    per_seq = seq // tm
    nb = max(seq // MOBA_BLOCK, SUBLANES)
    rows = pl.BlockSpec((tm, MOBA_W), lambda i: (i, 0))
    rope_spec = pl.BlockSpec((tm, MOBA_HD), lambda i: (i % n_rope, 0))
    weights = [w["mq"], w["mk"], w["mv"], w["qg"], w["kg"]]
    w_bytes = sum(int(a.size) * a.dtype.itemsize for a in weights)
    out_bytes = tm * MOBA_W * 3 * 4 + tm * MOBA_HEADS * 2 * MOBA_HD * 2
    need = w_bytes + 2 * out_bytes + 2 * tm * D_MODEL * 4 + 8 * tm * MOBA_W * 4 + (4 << 20)
    return pl.pallas_call(
        functools.partial(_moba_proj_body, select=select, tiles_per_seq=per_seq),
        grid=(m // tm,),
        in_specs=[pl.BlockSpec((tm, D_MODEL), lambda i: (i, 0)), _resident((1, D_MODEL))]
                 + [_resident(a.shape) for a in weights] + [rope_spec, rope_spec],
        out_specs=[rows, rows, rows,
                   pl.BlockSpec((1, MOBA_HEADS, tm, 2 * MOBA_HD), lambda i: (i // per_seq, 0, i % per_seq, 0))],
        out_shape=[jax.ShapeDtypeStruct((m, MOBA_W), F32)] * 3
                  + [jax.ShapeDtypeStruct((n_seq if select else 1, MOBA_HEADS, seq if select else m, 2 * MOBA_HD), BF16)],
        scratch_shapes=[pltpu.VMEM((nb, MOBA_W), F32)],
        compiler_params=pltpu.CompilerParams(dimension_semantics=("arbitrary",),
                                             vmem_limit_bytes=_vmem_limit(need)),
        name="moba_input_projection",
    )(x, mix_g, *weights, rope_cos, rope_sin)


def _gla_tables():
    c = GLA_CHUNK
    t = np.arange(c)
    le = t[None, :] <= t[:, None]
    gt = t[None, :] > t[:, None]
    masks = []
    s = c // 2
    while s >= 1:
        same = (t // (2 * s))[:, None] == (t // (2 * s))[None, :]
        right = (t % (2 * s)) >= s
        masks.append(same & right[:, None] & ~right[None, :])
        s //= 2
    masks.append(np.eye(c, dtype=bool))
    tail = np.concatenate([gt.T, np.ones((c, c), dtype=bool)], axis=1)
    return le.astype(np.float32), np.stack(masks).astype(np.float32), tail.astype(np.float32)


def _level_reference(b, s):
    c, dk = b.shape
    if s >= SUBLANES:
        blocks = b.reshape(c // (2 * s), 2 * s, dk)
        return jnp.broadcast_to(blocks[:, s - 1:s, :], blocks.shape).reshape(c, dk)
    rows8 = b.reshape(c // SUBLANES, SUBLANES, dk)
    sub = lax.broadcasted_iota(jnp.int32, rows8.shape, 1)
    ref = jnp.broadcast_to(rows8[:, SUBLANES - s - 1:SUBLANES - s, :], rows8.shape)
    for first in range(SUBLANES - 4 * s, -1, -2 * s):
        ref = jnp.where(sub < first + 2 * s, jnp.broadcast_to(rows8[:, first + s - 1:first + s, :], rows8.shape), ref)
    return ref.reshape(c, dk)


def _gla_body(q_ref, k_ref, la_ref, v_ref, gg_ref, ng_ref, le_ref, masks_ref, tail_ref, o_ref, s_ref):
    c = GLA_CHUNK
    n_lvl = masks_ref.shape[0] - 1

    @pl.when(pl.program_id(1) == 0)
    def _():
        s_ref[...] = jnp.zeros_like(s_ref)

    le = le_ref[...]
    tail = tail_ref[...]
    for chunk, hd in [(cc, hh) for cc in range(q_ref.shape[0] // c) for hh in range(GLA_HEADS)]:
        rows = slice(chunk * c, (chunk + 1) * c)
        ks = slice(hd * GLA_HDK, (hd + 1) * GLA_HDK)
        vs = slice(hd * GLA_HDV, (hd + 1) * GLA_HDV)
        q = q_ref[rows, ks]
        k = k_ref[rows, ks]
        v = v_ref[rows, vs]
        state = s_ref[0, hd]
        b = sum(_dot(le, p) for p in _split3(la_ref[rows, ks])) * LOG2_E
        et = jnp.exp(sum(_dot(p, tail) for p in _split3(la_ref[rows, ks].T)))
        out = _dot((q * jnp.exp2(b)).astype(BF16), state.astype(BF16))
        attn = masks_ref[n_lvl] * _dot_nt(q.astype(BF16), k.astype(BF16))
        for lv in range(n_lvl):
            e = jnp.exp2(-jnp.abs(b - _level_reference(b, c >> (lv + 1))))
            attn = attn + masks_ref[lv] * _dot_nt((q * e).astype(BF16), (k * e).astype(BF16))
        out = out + _dot(attn.astype(BF16), v)
        k_dec = (k.T * et[:, :c]).astype(BF16)
        decay = et[:, c:]
        s_ref[0, hd] = jnp.concatenate([state[:, :c] * decay, state[:, c:] * decay], axis=1) + _dot(k_dec, v)
        gate = gg_ref[rows, vs].astype(F32)
        o_ref[rows, vs] = (_rms_norm(out, ng_ref[...]) * (gate * jax.nn.sigmoid(gate))).astype(BF16)


def _gla_prompt(gq, gk, la, gv, gg, norm_g, batch, seq):
    c = GLA_CHUNK * GLA_CHUNKS_PER_STEP
    n_chunks = seq // c
    le, masks, tail = (jnp.asarray(a, BF16 if i != 1 else F32) for i, a in enumerate(_gla_tables()))

    def rows(width):
        return pl.BlockSpec((c, width), lambda b, t: (b * n_chunks + t, 0))

    need = 2 * (3 * c * GLA_DK * 4 + 3 * c * GLA_DV * 2) + 4 * GLA_DK * GLA_HDV * 4 + (8 << 20)
    return pl.pallas_call(
        _gla_body,
        grid=(batch, n_chunks),
        in_specs=[rows(GLA_DK), rows(GLA_DK), rows(GLA_DK), rows(GLA_DV), rows(GLA_DV),
                  _resident((1, GLA_HDV)), _resident(le.shape), _resident(masks.shape), _resident(tail.shape)],
        out_specs=[rows(GLA_DV),
                   pl.BlockSpec((1, GLA_HEADS, GLA_HDK, GLA_HDV), lambda b, t: (b, 0, 0, 0))],
        out_shape=[jax.ShapeDtypeStruct((batch * seq, GLA_DV), BF16),
                   jax.ShapeDtypeStruct((batch, GLA_HEADS, GLA_HDK, GLA_HDV), F32)],
        compiler_params=pltpu.CompilerParams(dimension_semantics=("parallel", "arbitrary"),
                                             vmem_limit_bytes=_vmem_limit(need)),
        name="gla_prompt_chunks",
    )(gq, gk, la, gv, gg, norm_g, le, masks, tail)


def _moba_attend_body(pt_ref, qe_ref, k_ref, v_ref, *refs):
    pages, (o_ref, ksum_ref, acc_ref, m_ref) = refs[:-4], refs[-4:]
    j = pl.program_id(2)
    tb = MOBA_BLOCK
    tkv = k_ref.shape[0]
    n_groups = qe_ref.shape[2] // (tb * MOBA_GROUP)

    @pl.when(j == 0)
    def _():
        acc_ref[...] = jnp.zeros_like(acc_ref)
        m_ref[...] = jnp.full_like(m_ref, NEG_INIT)

    k = k_ref[...].astype(BF16)
    lane = lax.broadcasted_iota(jnp.int32, (tkv, MOBA_HD), 1)
    key_block = j * (tkv // tb) + lax.broadcasted_iota(jnp.int32, (tkv, MOBA_HD), 0) // tb
    k_ext = jnp.concatenate([k, jnp.where(lane == key_block, MASK_BIG, 0.0).astype(BF16)], axis=1)
    v_ext = jnp.concatenate([v_ref[...].astype(BF16), jnp.ones((tkv, MOBA_HD), BF16)], axis=1)

    def update(rows, s):
        m_old = m_ref[rows, :]
        m_new = jnp.maximum(m_old, jnp.max(s, axis=1, keepdims=True))
        p = jnp.exp2(s - jnp.concatenate([m_new] * (tkv // MOBA_HD), axis=1)).astype(BF16)
        alpha = jnp.exp2(m_old - m_new)
        acc_new = jnp.concatenate([alpha, alpha], axis=1) * acc_ref[rows, :] + _dot(p, v_ext)
        m_ref[rows, :] = m_new
        acc_ref[rows, :] = acc_new
        return acc_new

    rows_per_group = tb * MOBA_GROUP

    def group_rows(g):
        return pl.ds(pl.multiple_of(g * rows_per_group, rows_per_group), rows_per_group)

    def scores(g):
        return _dot_nt(qe_ref[0, 0, group_rows(g), :], k_ext)

    first = (j * tkv) // rows_per_group
    n_later = n_groups - 1 - first
    n_extra = n_later % MOBA_VISITS

    def page_sum(page_ref):
        return jnp.sum(jnp.sum(page_ref[...].reshape(4, PAGE_SIZE // 4, MOBA_HEADS, MOBA_HD), axis=1), axis=0)

    def first_groups(extra):
        lower = (lax.broadcasted_iota(jnp.int32, (tb, tb), 1) <= lax.broadcasted_iota(jnp.int32, (tb, tb), 0))
        n_parts = 1 if extra else 2
        part = rows_per_group // n_parts
        visits = []
        for hh in range(n_parts):
            rows = pl.ds(pl.multiple_of(first * rows_per_group + hh * part, part), part)
            s_own = _dot_nt(qe_ref[0, 0, rows, :], k_ext)
            tiles = []
            for t2 in range(part // tb):
                t = hh * (part // tb) + t2
                r, c = slice(t2 * tb, (t2 + 1) * tb), slice(t * tb, (t + 1) * tb)
                diag = jnp.where(lower, s_own[r, c], -MASK_BIG)
                parts = (([s_own[r, :t * tb]] if t else []) + [diag]
                         + ([s_own[r, (t + 1) * tb:]] if (t + 1) * tb < tkv else []))
                tiles.append(jnp.concatenate(parts, axis=1))
            visits.append((rows, jnp.concatenate(tiles, axis=0)))
        visits += [(group_rows(first + u), scores(first + u)) for u in range(1, extra + 1)]
        for rows, s in visits:
            update(rows, s)
        acc = acc_ref[pl.ds(pl.multiple_of(j * tkv, tkv), tkv), :]
        o_ref[...] = (acc[:, :MOBA_HD] / acc[:, MOBA_HD:]).astype(BF16)
        per_block = MOBA_BLOCK // PAGE_SIZE
        for blk in range(len(pages) // per_block):
            tot = page_sum(pages[blk * per_block])
            for p in range(1, per_block):
                tot = tot + page_sum(pages[blk * per_block + p])
            ksum_ref[0, blk] = tot

    for extra in range(MOBA_VISITS):
        pl.when(n_extra == extra)(functools.partial(first_groups, extra))

    def body(i, carry):
        g0 = first + 1 + n_extra + MOBA_VISITS * i
        ss = [scores(g0 + u) for u in range(MOBA_VISITS)]
        for u, s in enumerate(ss):
            update(group_rows(g0 + u), s)
        return carry

    lax.fori_loop(0, n_later // MOBA_VISITS, body, 0)


def _moba_attend(qe, k, v, batch, seq, cache, layer, page_table_flat):
    tkv = MOBA_BLOCK * MOBA_KV_BLOCKS
    nb = seq // tkv
    n_steps = batch * MOBA_HEADS * nb
    per_block = MOBA_BLOCK // PAGE_SIZE
    pages_per_step = page_table_flat.shape[0] // n_steps
    assert pages_per_step * n_steps == page_table_flat.shape[0] and pages_per_step % per_block == 0
    kv = pl.BlockSpec((tkv, MOBA_HD), lambda b, h, j, pt: (b * nb + j, h))

    def step(b, h, j):
        return (b * MOBA_HEADS + h) * nb + j

    def page_spec(i):
        return pl.BlockSpec((None, None, PAGE_SIZE, MOBA_HEADS, MOBA_HD),
                            lambda b, h, j, pt: (layer, pt[step(b, h, j) * pages_per_step + i], 0, 0, 0))

    blocks_per_step = pages_per_step // per_block
    need = (2 * seq * 2 * MOBA_HD * 2 + 3 * seq * MOBA_HD * 4 + 2 * pages_per_step * PAGE_SIZE * MOBA_W * 4
            + (16 << 20))
    out, sums = pl.pallas_call(
        _moba_attend_body,
        grid_spec=pltpu.PrefetchScalarGridSpec(
            num_scalar_prefetch=1,
            grid=(batch, MOBA_HEADS, nb),
            in_specs=[pl.BlockSpec((1, 1, seq, 2 * MOBA_HD), lambda b, h, j, pt: (b, h, 0, 0)), kv, kv]
                     + [page_spec(i) for i in range(pages_per_step)],
            out_specs=[kv, pl.BlockSpec((1, blocks_per_step, MOBA_HEADS, MOBA_HD),
                                        lambda b, h, j, pt: (step(b, h, j), 0, 0, 0))],
            scratch_shapes=[pltpu.VMEM((seq, 2 * MOBA_HD), F32), pltpu.VMEM((seq, MOBA_HD), F32)],
        ),
        out_shape=[jax.ShapeDtypeStruct((batch * seq, MOBA_W), BF16),
                   jax.ShapeDtypeStruct((n_steps, blocks_per_step, MOBA_HEADS, MOBA_HD), F32)],
        compiler_params=pltpu.CompilerParams(dimension_semantics=("parallel", "parallel", "arbitrary"),
                                             vmem_limit_bytes=_vmem_limit(need)),
        name="moba_prompt_attention",
    )(page_table_flat, qe, k, v, *([cache] * pages_per_step))
    return out, sums


def _merge_body(x_ref, g_ref, oa_ref, ob_ref, wga, wgb, wpa, wpb, wo, o_ref):
    x = x_ref[...]
    h = _rms_norm(x, g_ref[...]).astype(BF16)
    mix = jax.nn.sigmoid(_dot(h, wga[...])) * _dot(oa_ref[...], wpa[...])
    mix = mix + jax.nn.sigmoid(_dot(h, wgb[...])) * _dot(ob_ref[...], wpb[...])
    o_ref[...] = x + _dot(mix.astype(BF16), wo[...])


def _merge(x, mix_g, o_gla, o_moba, w):
    m = x.shape[0]
    tm = min(FFN_ROWS, m)
    row32 = pl.BlockSpec((tm, D_MODEL), lambda i: (i, 0))
    sq = _resident((D_MODEL, D_MODEL))
    need = 5 * D_MODEL * D_MODEL * 2 + 4 * tm * D_MODEL * 4 + 4 * tm * D_MODEL * 2 + 6 * tm * D_MODEL * 4
    return pl.pallas_call(
        _merge_body,
        grid=(m // tm,),
        in_specs=[row32, _resident((1, D_MODEL)), row32, row32, sq, sq, sq, sq, sq],
        out_specs=row32,
        out_shape=jax.ShapeDtypeStruct((m, D_MODEL), F32),
        compiler_params=pltpu.CompilerParams(dimension_semantics=("parallel",),
                                             vmem_limit_bytes=_vmem_limit(need)),
        name="gated_merge_projection",
    )(x, mix_g, o_gla, o_moba, w["ga"], w["gb"], w["pa"], w["pb"], w["o"])


def _gla_sample_body(q_ref, k_ref, la_ref, v_ref, gg_ref, s_ref, ng_ref, o_ref, so_ref):
    def column(ref, hd):
        row = ref[0, :, hd * GLA_HDK:(hd + 1) * GLA_HDK]
        col = jnp.broadcast_to(row, (GLA_HDK, GLA_HDK)).T
        return jnp.concatenate([col] * (GLA_HDV // GLA_HDK), axis=1)

    for hd in range(GLA_HEADS):
        vs = slice(hd * GLA_HDV, (hd + 1) * GLA_HDV)
        v = v_ref[0, :, vs].astype(F32)
        s_new = jnp.exp(column(la_ref, hd)) * s_ref[0, hd] + column(k_ref, hd) * v
        so_ref[0, hd] = s_new
        out = jnp.sum(column(q_ref, hd) * s_new, axis=0, keepdims=True)
        gate = gg_ref[0, :, vs].astype(F32)
        o_ref[0, :, vs] = (_rms_norm(out, ng_ref[...]) * (gate * jax.nn.sigmoid(gate))).astype(BF16)


def _gla_sample(gq, gk, la, gv, gg, state, norm_g):
    n = gq.shape[0]
    assert GLA_HDV % GLA_HDK == 0
    key_row = pl.BlockSpec((1, 1, GLA_DK), lambda i: (i, 0, 0))
    val_row = pl.BlockSpec((1, 1, GLA_DV), lambda i: (i, 0, 0))
    st = pl.BlockSpec((1, GLA_HEADS, GLA_HDK, GLA_HDV), lambda i: (i, 0, 0, 0))
    as_row = lambda a: a.reshape(n, 1, a.shape[-1])
    o, s_new = pl.pallas_call(
        _gla_sample_body,
        grid=(n,),
        in_specs=[key_row, key_row, key_row, val_row, val_row, st, _resident((1, GLA_HDV))],
        out_specs=[val_row, st],
        out_shape=[jax.ShapeDtypeStruct((n, 1, GLA_DV), BF16),
                   jax.ShapeDtypeStruct((n, GLA_HEADS, GLA_HDK, GLA_HDV), F32)],
        compiler_params=pltpu.CompilerParams(dimension_semantics=("parallel",),
                                             vmem_limit_bytes=_vmem_limit(16 << 20)),
        name="gla_sample_step",
    )(as_row(gq), as_row(gk), as_row(la), as_row(gv), as_row(gg), state, norm_g)
    return o.reshape(n, GLA_DV), s_new


def _sample_select_body(q_ref, ksum_ref, sel_ref):
    nb = ksum_ref.shape[1]
    lane = lax.broadcasted_iota(jnp.int32, (nb, LANES), 1)
    row = lax.broadcasted_iota(jnp.int32, (nb, LANES), 0).astype(F32)
    out_row = lax.broadcasted_iota(jnp.int32, (SUBLANES, LANES), 0)
    for s in range(q_ref.shape[0]):
        q = q_ref[s]
        gate = jnp.full((nb, LANES), -jnp.inf, F32)
        for hd in range(MOBA_HEADS):
            sl = slice(hd * MOBA_HD, (hd + 1) * MOBA_HD)
            g = jnp.sum(ksum_ref[s, :, sl] * (1.0 / MOBA_BLOCK) * q[:, sl], axis=1, keepdims=True)
            gate = jnp.where(lane == hd, g, gate)
        out = jnp.zeros((SUBLANES, LANES), jnp.int32)
        for r in range(MOBA_TOPK):
            best = jnp.max(gate, axis=0, keepdims=True)
            first = jnp.min(jnp.where(gate == best, row, float(nb)), axis=0, keepdims=True)
            out = jnp.where(out_row == r, first.astype(jnp.int32), out)
            gate = jnp.where(row == first, -jnp.inf, gate)
        sel_ref[s] = out


def _sample_select(q, ksum):
    n, nb = ksum.shape[0], ksum.shape[1]
    per_step = SUBLANES if n % SUBLANES == 0 else 1
    sel = pl.pallas_call(
        _sample_select_body,
        grid=(n // per_step,),
        in_specs=[pl.BlockSpec((per_step, 1, MOBA_W), lambda i: (i, 0, 0)),
                  pl.BlockSpec((per_step, nb, MOBA_W), lambda i: (i, 0, 0))],
        out_specs=pl.BlockSpec((per_step, SUBLANES, LANES), lambda i: (i, 0, 0)),
        out_shape=jax.ShapeDtypeStruct((n, SUBLANES, LANES), jnp.int32),
        compiler_params=pltpu.CompilerParams(dimension_semantics=("parallel",)),
        name="moba_sample_select",
    )(q.reshape(n, 1, MOBA_W), ksum)
    return jnp.transpose(sel[:, :MOBA_TOPK, :MOBA_HEADS], (0, 2, 1)).reshape(-1)


def _sample_attend_body(pt_ref, sel_ref, q_ref, kn_ref, vn_ref, ck_hbm, cv_hbm, o_ref, k_buf, v_buf, sem,
                        *, layer, n_pages):
    per_block = MOBA_BLOCK // PAGE_SIZE
    n_sel = MOBA_TOPK * per_block
    n_slots = MOBA_HEADS * n_sel
    seq = pl.program_id(0)
    scale = MOBA_HD ** -0.5

    def page_copies(s, hd, i):
        blk = sel_ref[(s * MOBA_HEADS + hd) * MOBA_TOPK + i // per_block]
        page = pt_ref[s * n_pages + blk * per_block + i % per_block]
        slot = (s % 2) * n_slots + hd * n_sel + i
        return (pltpu.make_async_copy(ck_hbm.at[layer, page, :, hd, :], k_buf.at[slot], sem.at[0, slot]),
                pltpu.make_async_copy(cv_hbm.at[layer, page, :, hd, :], v_buf.at[slot], sem.at[1, slot]))

    def start_fetch(s):
        for hd in range(MOBA_HEADS):
            for i in range(n_sel):
                for cp in page_copies(s, hd, i):
                    cp.start()

    @pl.when(seq == 0)
    def _():
        start_fetch(seq)

    @pl.when(seq + 1 < pl.num_programs(0))
    def _():
        start_fetch(seq + 1)

    for hd in range(MOBA_HEADS):
        for i in range(n_sel):
            for cp in page_copies(seq, hd, i):
                cp.wait()

    base = (seq % 2) * n_slots
    for hd in range(MOBA_HEADS):
        sl = slice(hd * MOBA_HD, (hd + 1) * MOBA_HD)
        q = q_ref[0, :, sl]
        s_new = jnp.sum(kn_ref[0, :, sl] * q, axis=1, keepdims=True) * scale
        scores = [jnp.sum(k_buf[base + hd * n_sel + i] * q, axis=1, keepdims=True) * scale for i in range(n_sel)]
        top = s_new
        for s in scores:
            top = jnp.maximum(top, jnp.max(s, axis=0, keepdims=True))
        p_new = jnp.exp(s_new - top)
        denom = p_new
        acc = p_new * vn_ref[0, :, sl]
        for i, s in enumerate(scores):
            p = jnp.exp(s - top)
            denom = denom + jnp.sum(p, axis=0, keepdims=True)
            acc = acc + jnp.sum(p * v_buf[base + hd * n_sel + i], axis=0, keepdims=True)
        o_ref[0, :, sl] = (acc / denom).astype(BF16)


def _sample_attend(q, k_new, v_new, cache_k, cache_v, layer, page_table_flat, sel_flat, n_pages):
    n = q.shape[0]
    n_slots = MOBA_HEADS * MOBA_TOPK * (MOBA_BLOCK // PAGE_SIZE)
    tok = pl.BlockSpec((1, 1, MOBA_W), lambda s, pt, sel: (s, 0, 0))
    hbm = pl.BlockSpec(memory_space=pl.ANY)
    as_tok = lambda a: a.reshape(n, 1, MOBA_W)
    out = pl.pallas_call(
        functools.partial(_sample_attend_body, layer=layer, n_pages=n_pages),
        grid_spec=pltpu.PrefetchScalarGridSpec(
            num_scalar_prefetch=2,
            grid=(n,),
            in_specs=[tok, tok, tok, hbm, hbm],
            out_specs=tok,
            scratch_shapes=[pltpu.VMEM((2 * n_slots, PAGE_SIZE, MOBA_HD), F32),
                            pltpu.VMEM((2 * n_slots, PAGE_SIZE, MOBA_HD), F32),
                            pltpu.SemaphoreType.DMA((2, 2 * n_slots))],
        ),
        out_shape=jax.ShapeDtypeStruct((n, 1, MOBA_W), BF16),
        compiler_params=pltpu.CompilerParams(dimension_semantics=("arbitrary",),
                                             vmem_limit_bytes=_vmem_limit(5 * n_slots * PAGE_SIZE * MOBA_HD * 4)),
        name="moba_sample_attention",
    )(page_table_flat, sel_flat, as_tok(q), as_tok(k_new), as_tok(v_new), cache_k, cache_v)
    return out.reshape(n, MOBA_W)


def _rope_tables(pos):
    half = MOBA_HD // 2
    inv = ROPE_THETA ** (-jnp.arange(half, dtype=F32) / half)
    ang = pos.astype(F32)[:, None] * inv[None, :]
    cos, sin = jnp.cos(ang), jnp.sin(ang)
    return jnp.concatenate([cos, cos], axis=1), jnp.concatenate([-sin, sin], axis=1)


def _layer_weights(w_in, w_a2, b_a, q_norm_g, k_norm_g, w_pa, w_pb, w_o):
    offs = np.concatenate([[0], np.cumsum(IN_SPLITS)])
    gq, gk, gv, gg, glr, mq, mk, mv, ga, gb = (w_in[:, offs[i]:offs[i + 1]] for i in range(len(IN_SPLITS)))
    glr = jnp.pad(glr, ((0, 0), (0, LANES - GLA_RANK)))
    a2 = jnp.pad(w_a2, ((0, LANES - GLA_RANK), (0, 0)))
    b16 = lambda a: a.astype(BF16)
    return dict(
        gq=b16(gq), gk=b16(gk), gv=b16(gv), gg=b16(gg), glr=b16(glr), a2=b16(a2), ba=b_a.reshape(1, GLA_DK),
        mq=b16(mq), mk=b16(mk), mv=b16(mv),
        qg=jnp.tile(q_norm_g, MOBA_HEADS).reshape(1, MOBA_W), kg=jnp.tile(k_norm_g, MOBA_HEADS).reshape(1, MOBA_W),
        ga=b16(ga), gb=b16(gb), pa=b16(w_pa), pb=b16(w_pb), o=b16(w_o))


def kernel(x_prompt, x_sample, cache_k, cache_v, state_gla, page_table, ffn1_g, ffn1_wg, ffn1_wu, ffn1_wd, mix_g, w_in, w_a2, b_a, gla_norm_g, q_norm_g, k_norm_g, w_pa, w_pb, w_o, ffn2_g, ffn2_wg, ffn2_wu, ffn2_wd):
    batch, seq, _ = x_prompt.shape
    n_dec, dec_seq, _ = x_sample.shape
    n_pages = page_table.shape[1]
    depth = w_in.shape[0]
    past_len = n_pages * PAGE_SIZE
    assert dec_seq == 1 and seq % MOBA_BLOCK == 0 and past_len % MOBA_BLOCK == 0
    assert past_len // MOBA_BLOCK >= MOBA_TOPK
    assert (seq // MOBA_BLOCK) % MOBA_GROUP == 0 and MOBA_GROUP == MOBA_KV_BLOCKS

    yp = x_prompt.reshape(batch * seq, D_MODEL)
    ys = x_sample.reshape(n_dec, D_MODEL)
    rope_p = _rope_tables(jnp.arange(seq, dtype=jnp.int32))
    rope_s = _rope_tables(jnp.full((n_dec,), past_len, jnp.int32))
    pt_flat = page_table.reshape(-1)
    row = lambda a: a.reshape(1, -1)
    outs = [[] for _ in range(6)]
    for l in range(depth):
        w = _layer_weights(w_in[l], w_a2[l], b_a[l], q_norm_g[l], k_norm_g[l], w_pa[l], w_pb[l], w_o[l])
        ffn1 = (row(ffn1_g[l]), ffn1_wg[l].astype(BF16), ffn1_wu[l].astype(BF16), ffn1_wd[l].astype(BF16))
        ffn2 = (row(ffn2_g[l]), ffn2_wg[l].astype(BF16), ffn2_wu[l].astype(BF16), ffn2_wd[l].astype(BF16))
        norm_g = row(gla_norm_g[l])

        x1 = _ffn(yp, *ffn1)
        gq, gk, gv, gg, la = _gla_proj(x1, row(mix_g[l]), w)
        o_gla, s_prompt = _gla_prompt(gq, gk, la, gv, gg, norm_g, batch, seq)
        _, k, v, qe = _moba_proj(x1, row(mix_g[l]), w, *rope_p, n_seq=batch, select=True)
        o_moba, page_sums = _moba_attend(qe, k, v, batch, seq, cache_k, l, pt_flat)
        yp = _ffn(_merge(x1, row(mix_g[l]), o_gla, o_moba, w), *ffn2)
        outs[0].append(k.reshape(batch, seq, MOBA_HEADS, MOBA_HD))
        outs[1].append(v.reshape(batch, seq, MOBA_HEADS, MOBA_HD))
        outs[2].append(s_prompt)

        x1 = _ffn(ys, *ffn1)
        gq, gk, gv, gg, la = _gla_proj(x1, row(mix_g[l]), w)
        q, k, v, _ = _moba_proj(x1, row(mix_g[l]), w, *rope_s, n_seq=1, select=False)
        o_gla, s_sample = _gla_sample(gq, gk, la, gv, gg, state_gla[l], norm_g)
        sel = _sample_select(q, page_sums.reshape(n_dec, n_pages * PAGE_SIZE // MOBA_BLOCK, MOBA_W))
        o_moba = _sample_attend(q, k, v, cache_k, cache_v, l, pt_flat, sel, n_pages)
        ys = _ffn(_merge(x1, row(mix_g[l]), o_gla, o_moba, w), *ffn2)
        outs[3].append(k.reshape(n_dec, 1, MOBA_HEADS, MOBA_HD))
        outs[4].append(v.reshape(n_dec, 1, MOBA_HEADS, MOBA_HD))
        outs[5].append(s_sample)

    return (yp.reshape(batch, seq, D_MODEL), ys.reshape(n_dec, 1, D_MODEL),
            jnp.stack(outs[0]), jnp.stack(outs[1]), jnp.stack(outs[2]),
            jnp.stack(outs[3]), jnp.stack(outs[4]), jnp.stack(outs[5]))
```

```python
import functools

import numpy as np
import jax
import jax.numpy as jnp
from jax import lax
from jax.experimental import pallas as pl
from jax.experimental.pallas import tpu as pltpu

F32 = jnp.float32
BF16 = jnp.bfloat16

D_MODEL = 1024
D_FF = 2816
GLA_HEADS = 4
GLA_HDK = 128
GLA_HDV = 256
GLA_DK = GLA_HEADS * GLA_HDK
GLA_DV = GLA_HEADS * GLA_HDV
GLA_RANK = 16
GLA_TAU = 16.0
MOBA_HEADS = 8
MOBA_HD = 128
MOBA_W = MOBA_HEADS * MOBA_HD
MOBA_BLOCK = 256
MOBA_TOPK = 3
ROPE_THETA = 10000.0
EPS = 1e-6
PAGE_SIZE = 128
IN_SPLITS = (GLA_DK, GLA_DK, GLA_DV, GLA_DV, GLA_RANK, MOBA_W, MOBA_W, MOBA_W, D_MODEL, D_MODEL)

LANES = 128
SUBLANES = 8
VMEM_BYTES = 64 * 1024 * 1024

FF_CHUNK = 256
FFN_ROWS = 512
PROJ_ROWS = 512
GLA_CHUNK = 128
GLA_CHUNKS_PER_STEP = 4
MOBA_GROUP = 4
MOBA_KV_BLOCKS = 4
MOBA_VISITS = 3
MASK_BIG = 2.0 ** 100
NEG_INIT = -1.0e38
LOG2_E = 1.4426950408889634


def _vmem_limit(nbytes):
    return int(min(VMEM_BYTES - (4 << 20), max(nbytes, 16 << 20)))


def _resident(shape):
    return pl.BlockSpec(shape, lambda *_: (0,) * len(shape), pipeline_mode=pl.Buffered(1))


def _dot(a, b):
    return jnp.dot(a, b, preferred_element_type=F32)


def _dot_nt(a, b):
    return lax.dot_general(a, b, (((1,), (1,)), ((), ())), preferred_element_type=F32)


def _rms_norm(x, g):
    return x * lax.rsqrt(jnp.mean(x * x, axis=-1, keepdims=True) + EPS) * g


def _log_sigmoid(z):
    return jnp.minimum(z, 0.0) - jnp.log1p(jnp.exp(-jnp.abs(z)))


def _split3(x):
    hi = x.astype(BF16)
    r = x - hi.astype(F32)
    mid = r.astype(BF16)
    lo = (r - mid.astype(F32)).astype(BF16)
    return hi, mid, lo


def _ffn_body(x_ref, g_ref, wg_ref, wu_ref, wd_ref, o_ref):
    x = x_ref[...]
    h = _rms_norm(x, g_ref[...]).astype(BF16)
    acc = jnp.zeros_like(x)
    for c in range(D_FF // FF_CHUNK):
        sl = slice(c * FF_CHUNK, (c + 1) * FF_CHUNK)
        a = _dot(h, wg_ref[:, sl])
        u = _dot(h, wu_ref[:, sl])
        act = (a * jax.nn.sigmoid(a) * u).astype(BF16)
        acc = acc + _dot(act, wd_ref[sl, :])
    o_ref[...] = x + 0.5 * acc


def _ffn(x, g, wg, wu, wd):
    m = x.shape[0]
    tm = min(FFN_ROWS, m)
    row = pl.BlockSpec((tm, D_MODEL), lambda i: (i, 0))
    need = 3 * D_MODEL * D_FF * 2 + 4 * tm * D_MODEL * 4 + 6 * tm * D_MODEL * 4
    return pl.pallas_call(
        _ffn_body,
        grid=(m // tm,),
        in_specs=[row, _resident((1, D_MODEL)), _resident((D_MODEL, D_FF)), _resident((D_MODEL, D_FF)),
                  _resident((D_FF, D_MODEL))],
        out_specs=row,
        out_shape=jax.ShapeDtypeStruct((m, D_MODEL), F32),
        compiler_params=pltpu.CompilerParams(dimension_semantics=("parallel",),
                                             vmem_limit_bytes=_vmem_limit(need)),
        name="swiglu_half_step",
    )(x, g, wg, wu, wd)


def _gla_proj_body(x_ref, g_ref, wgq, wgk, wgv, wgg, wglr, wa2, ba, gq_o, gk_o, gv_o, gg_o, la_o):
    h = _rms_norm(x_ref[...], g_ref[...]).astype(BF16)
    gq_o[...] = _dot(h, wgq[...]) * (GLA_HDK ** -0.5)
    gk_o[...] = _dot(h, wgk[...])
    gv_o[...] = _dot(h, wgv[...]).astype(BF16)
    gg_o[...] = _dot(h, wgg[...]).astype(BF16)
    glr = _dot(h, wglr[...]).astype(BF16)
    la_o[...] = _log_sigmoid(_dot(glr, wa2[...]) + ba[...]) * (1.0 / GLA_TAU)


def _gla_proj(x, mix_g, w):
    m = x.shape[0]
    tm = min(PROJ_ROWS, m)

    def rows(width):
        return pl.BlockSpec((tm, width), lambda i: (i, 0))

    weights = [w["gq"], w["gk"], w["gv"], w["gg"], w["glr"], w["a2"], w["ba"]]
    out_shape = [
        jax.ShapeDtypeStruct((m, GLA_DK), F32), jax.ShapeDtypeStruct((m, GLA_DK), F32),
        jax.ShapeDtypeStruct((m, GLA_DV), BF16), jax.ShapeDtypeStruct((m, GLA_DV), BF16),
        jax.ShapeDtypeStruct((m, GLA_DK), F32),
    ]
    w_bytes = sum(int(a.size) * a.dtype.itemsize for a in weights)
    need = w_bytes + 2 * tm * (3 * GLA_DK * 4 + 2 * GLA_DV * 2) + 2 * tm * D_MODEL * 4 + 6 * tm * GLA_DV * 4
    return pl.pallas_call(
        _gla_proj_body,
        grid=(m // tm,),
        in_specs=[rows(D_MODEL), _resident((1, D_MODEL))] + [_resident(a.shape) for a in weights],
        out_specs=[rows(GLA_DK), rows(GLA_DK), rows(GLA_DV), rows(GLA_DV), rows(GLA_DK)],
        out_shape=out_shape,
        compiler_params=pltpu.CompilerParams(dimension_semantics=("parallel",),
                                             vmem_limit_bytes=_vmem_limit(need)),
        name="gla_input_projection",
    )(x, mix_g, *weights)


def _moba_proj_body(x_ref, g_ref, wmq, wmk, wmv, qg, kg, rc, rs, q_o, k_o, v_o, qe_o, ksum_ref, *,
                    select, tiles_per_seq):
    tile = pl.program_id(0) % tiles_per_seq
    if select:
        @pl.when(tile == 0)
        def _():
            ksum_ref[...] = jnp.zeros_like(ksum_ref)

    h = _rms_norm(x_ref[...], g_ref[...]).astype(BF16)
    mq = _dot(h, wmq[...])
    mk = _dot(h, wmk[...])
    v_o[...] = _dot(h, wmv[...])
    cos = rc[...]
    sin = rs[...]
    for hd in range(MOBA_HEADS):
        sl = slice(hd * MOBA_HD, (hd + 1) * MOBA_HD)
        qh = _rms_norm(mq[:, sl], qg[:, sl])
        q_o[:, sl] = qh * cos + pltpu.roll(qh, MOBA_HD // 2, 1) * sin
        kh = _rms_norm(mk[:, sl], kg[:, sl])
        k_o[:, sl] = kh * cos + pltpu.roll(kh, MOBA_HD // 2, 1) * sin
    if not select:
        qe_o[...] = jnp.zeros_like(qe_o)
        return

    tq = MOBA_BLOCK
    nb = ksum_ref.shape[0]
    eye = (lax.broadcasted_iota(jnp.int32, (tq, tq), 0) == lax.broadcasted_iota(jnp.int32, (tq, tq), 1))
    eye = jnp.where(eye, 1.0, 0.0).astype(BF16)
    for c in range(q_o.shape[0] // tq):
        rows = slice(c * tq, (c + 1) * tq)
        own_i = tile * (q_o.shape[0] // tq) + c
        own = own_i.astype(F32)
        ksum_ref[pl.ds(own_i, 1), :] = jnp.sum(k_o[rows, :], axis=0, keepdims=True)
        gates = []
        for hd in range(MOBA_HEADS):
            sl = slice(hd * MOBA_HD, (hd + 1) * MOBA_HD)
            q = q_o[rows, sl]
            means = ksum_ref[:, sl] * (1.0 / MOBA_BLOCK)
            q_hi = q.astype(BF16)
            q_lo = (q - q_hi.astype(F32)).astype(BF16)
            m_hi = means.astype(BF16)
            m_lo = (means - m_hi.astype(F32)).astype(BF16)
            gates.append(_dot_nt(m_hi, q_hi) + _dot_nt(m_lo, q_hi) + _dot_nt(m_hi, q_lo))
            qe_o[0, hd, rows, :MOBA_HD] = (q * (MOBA_HD ** -0.5 * LOG2_E)).astype(BF16)
        gate = jnp.concatenate(gates, axis=1)
        blk = lax.broadcasted_iota(jnp.int32, gate.shape, 0).astype(F32)
        gate = jnp.where(blk < own, gate, -jnp.inf)
        mask = jnp.where(blk == own, 0.0, -1.0)
        for _ in range(MOBA_TOPK):
            best = jnp.max(gate, axis=0, keepdims=True)
            first = jnp.min(jnp.where(gate == best, blk, float(nb)), axis=0, keepdims=True)
            first = jnp.where(best > -jnp.inf, first, -1.0)
            pick = blk == first
            mask = jnp.where(pick, 0.0, mask)
            gate = jnp.where(pick, -jnp.inf, gate)
        mask = jnp.concatenate([mask, jnp.full((LANES - nb, mask.shape[1]), -1.0, F32)], axis=0).astype(BF16)
        for hd in range(MOBA_HEADS):
            qe_o[0, hd, rows, MOBA_HD:] = _dot_nt(eye, mask[:, hd * tq:(hd + 1) * tq]).astype(BF16)


def _moba_proj(x, mix_g, w, rope_cos, rope_sin, n_seq, select):
    m = x.shape[0]
    seq = m // n_seq
    tm = min(PROJ_ROWS, seq if select else m)
    n_rope = rope_cos.shape[0] // tm
    per_seq = seq // tm
    nb = max(seq // MOBA_BLOCK, SUBLANES)
    rows = pl.BlockSpec((tm, MOBA_W), lambda i: (i, 0))
    rope_spec = pl.BlockSpec((tm, MOBA_HD), lambda i: (i % n_rope, 0))
    weights = [w["mq"], w["mk"], w["mv"], w["qg"], w["kg"]]
    w_bytes = sum(int(a.size) * a.dtype.itemsize for a in weights)
    out_bytes = tm * MOBA_W * 3 * 4 + tm * MOBA_HEADS * 2 * MOBA_HD * 2
    need = w_bytes + 2 * out_bytes + 2 * tm * D_MODEL * 4 + 8 * tm * MOBA_W * 4 + (4 << 20)
    return pl.pallas_call(
        functools.partial(_moba_proj_body, select=select, tiles_per_seq=per_seq),
        grid=(m // tm,),
        in_specs=[pl.BlockSpec((tm, D_MODEL), lambda i: (i, 0)), _resident((1, D_MODEL))]
                 + [_resident(a.shape) for a in weights] + [rope_spec, rope_spec],
        out_specs=[rows, rows, rows,
                   pl.BlockSpec((1, MOBA_HEADS, tm, 2 * MOBA_HD), lambda i: (i // per_seq, 0, i % per_seq, 0))],
        out_shape=[jax.ShapeDtypeStruct((m, MOBA_W), F32)] * 3
                  + [jax.ShapeDtypeStruct((n_seq if select else 1, MOBA_HEADS, seq if select else m, 2 * MOBA_HD), BF16)],
        scratch_shapes=[pltpu.VMEM((nb, MOBA_W), F32)],
        compiler_params=pltpu.CompilerParams(dimension_semantics=("arbitrary",),
                                             vmem_limit_bytes=_vmem_limit(need)),
        name="moba_input_projection",
    )(x, mix_g, *weights, rope_cos, rope_sin)


def _gla_tables():
    c = GLA_CHUNK
    t = np.arange(c)
    le = t[None, :] <= t[:, None]
    gt = t[None, :] > t[:, None]
    masks = []
    s = c // 2
    while s >= 1:
        same = (t // (2 * s))[:, None] == (t // (2 * s))[None, :]
        right = (t % (2 * s)) >= s
        masks.append(same & right[:, None] & ~right[None, :])
        s //= 2
    masks.append(np.eye(c, dtype=bool))
    tail = np.concatenate([gt.T, np.ones((c, c), dtype=bool)], axis=1)
    return le.astype(np.float32), np.stack(masks).astype(np.float32), tail.astype(np.float32)


def _level_reference(b, s):
    c, dk = b.shape
    if s >= SUBLANES:
        blocks = b.reshape(c // (2 * s), 2 * s, dk)
        return jnp.broadcast_to(blocks[:, s - 1:s, :], blocks.shape).reshape(c, dk)
    rows8 = b.reshape(c // SUBLANES, SUBLANES, dk)
    sub = lax.broadcasted_iota(jnp.int32, rows8.shape, 1)
    ref = jnp.broadcast_to(rows8[:, SUBLANES - s - 1:SUBLANES - s, :], rows8.shape)
    for first in range(SUBLANES - 4 * s, -1, -2 * s):
        ref = jnp.where(sub < first + 2 * s, jnp.broadcast_to(rows8[:, first + s - 1:first + s, :], rows8.shape), ref)
    return ref.reshape(c, dk)


def _gla_body(q_ref, k_ref, la_ref, v_ref, gg_ref, ng_ref, le_ref, masks_ref, tail_ref, o_ref, s_ref):
    c = GLA_CHUNK
    n_lvl = masks_ref.shape[0] - 1

    @pl.when(pl.program_id(1) == 0)
    def _():
        s_ref[...] = jnp.zeros_like(s_ref)

    le = le_ref[...]
    tail = tail_ref[...]
    for chunk, hd in [(cc, hh) for cc in range(q_ref.shape[0] // c) for hh in range(GLA_HEADS)]:
        rows = slice(chunk * c, (chunk + 1) * c)
        ks = slice(hd * GLA_HDK, (hd + 1) * GLA_HDK)
        vs = slice(hd * GLA_HDV, (hd + 1) * GLA_HDV)
        q = q_ref[rows, ks]
        k = k_ref[rows, ks]
        v = v_ref[rows, vs]
        state = s_ref[0, hd]
        b = sum(_dot(le, p) for p in _split3(la_ref[rows, ks])) * LOG2_E
        et = jnp.exp(sum(_dot(p, tail) for p in _split3(la_ref[rows, ks].T)))
        out = _dot((q * jnp.exp2(b)).astype(BF16), state.astype(BF16))
        attn = masks_ref[n_lvl] * _dot_nt(q.astype(BF16), k.astype(BF16))
        for lv in range(n_lvl):
            e = jnp.exp2(-jnp.abs(b - _level_reference(b, c >> (lv + 1))))
            attn = attn + masks_ref[lv] * _dot_nt((q * e).astype(BF16), (k * e).astype(BF16))
        out = out + _dot(attn.astype(BF16), v)
        k_dec = (k.T * et[:, :c]).astype(BF16)
        decay = et[:, c:]
        s_ref[0, hd] = jnp.concatenate([state[:, :c] * decay, state[:, c:] * decay], axis=1) + _dot(k_dec, v)
        gate = gg_ref[rows, vs].astype(F32)
        o_ref[rows, vs] = (_rms_norm(out, ng_ref[...]) * (gate * jax.nn.sigmoid(gate))).astype(BF16)


def _gla_prompt(gq, gk, la, gv, gg, norm_g, batch, seq):
    c = GLA_CHUNK * GLA_CHUNKS_PER_STEP
    n_chunks = seq // c
    le, masks, tail = (jnp.asarray(a, BF16 if i != 1 else F32) for i, a in enumerate(_gla_tables()))

    def rows(width):
        return pl.BlockSpec((c, width), lambda b, t: (b * n_chunks + t, 0))

    need = 2 * (3 * c * GLA_DK * 4 + 3 * c * GLA_DV * 2) + 4 * GLA_DK * GLA_HDV * 4 + (8 << 20)
    return pl.pallas_call(
        _gla_body,
        grid=(batch, n_chunks),
        in_specs=[rows(GLA_DK), rows(GLA_DK), rows(GLA_DK), rows(GLA_DV), rows(GLA_DV),
                  _resident((1, GLA_HDV)), _resident(le.shape), _resident(masks.shape), _resident(tail.shape)],
        out_specs=[rows(GLA_DV),
                   pl.BlockSpec((1, GLA_HEADS, GLA_HDK, GLA_HDV), lambda b, t: (b, 0, 0, 0))],
        out_shape=[jax.ShapeDtypeStruct((batch * seq, GLA_DV), BF16),
                   jax.ShapeDtypeStruct((batch, GLA_HEADS, GLA_HDK, GLA_HDV), F32)],
        compiler_params=pltpu.CompilerParams(dimension_semantics=("parallel", "arbitrary"),
                                             vmem_limit_bytes=_vmem_limit(need)),
        name="gla_prompt_chunks",
    )(gq, gk, la, gv, gg, norm_g, le, masks, tail)


def _moba_attend_body(pt_ref, qe_ref, k_ref, v_ref, *refs):
    pages, (o_ref, ksum_ref, acc_ref, m_ref) = refs[:-4], refs[-4:]
    j = pl.program_id(2)
    tb = MOBA_BLOCK
    tkv = k_ref.shape[0]
    n_groups = qe_ref.shape[2] // (tb * MOBA_GROUP)

    @pl.when(j == 0)
    def _():
        acc_ref[...] = jnp.zeros_like(acc_ref)
        m_ref[...] = jnp.full_like(m_ref, NEG_INIT)

    k = k_ref[...].astype(BF16)
    lane = lax.broadcasted_iota(jnp.int32, (tkv, MOBA_HD), 1)
    key_block = j * (tkv // tb) + lax.broadcasted_iota(jnp.int32, (tkv, MOBA_HD), 0) // tb
    k_ext = jnp.concatenate([k, jnp.where(lane == key_block, MASK_BIG, 0.0).astype(BF16)], axis=1)
    v_ext = jnp.concatenate([v_ref[...].astype(BF16), jnp.ones((tkv, MOBA_HD), BF16)], axis=1)

    def update(rows, s):
        m_old = m_ref[rows, :]
        m_new = jnp.maximum(m_old, jnp.max(s, axis=1, keepdims=True))
        p = jnp.exp2(s - jnp.concatenate([m_new] * (tkv // MOBA_HD), axis=1)).astype(BF16)
        alpha = jnp.exp2(m_old - m_new)
        acc_new = jnp.concatenate([alpha, alpha], axis=1) * acc_ref[rows, :] + _dot(p, v_ext)
        m_ref[rows, :] = m_new
        acc_ref[rows, :] = acc_new
        return acc_new

    rows_per_group = tb * MOBA_GROUP

    def group_rows(g):
        return pl.ds(pl.multiple_of(g * rows_per_group, rows_per_group), rows_per_group)

    def scores(g):
        return _dot_nt(qe_ref[0, 0, group_rows(g), :], k_ext)

    first = (j * tkv) // rows_per_group
    n_later = n_groups - 1 - first
    n_extra = n_later % MOBA_VISITS

    def page_sum(page_ref):
        return jnp.sum(jnp.sum(page_ref[...].reshape(4, PAGE_SIZE // 4, MOBA_HEADS, MOBA_HD), axis=1), axis=0)

    def first_groups(extra):
        lower = (lax.broadcasted_iota(jnp.int32, (tb, tb), 1) <= lax.broadcasted_iota(jnp.int32, (tb, tb), 0))
        n_parts = 1 if extra else 2
        part = rows_per_group // n_parts
        visits = []
        for hh in range(n_parts):
            rows = pl.ds(pl.multiple_of(first * rows_per_group + hh * part, part), part)
            s_own = _dot_nt(qe_ref[0, 0, rows, :], k_ext)
            tiles = []
            for t2 in range(part // tb):
                t = hh * (part // tb) + t2
                r, c = slice(t2 * tb, (t2 + 1) * tb), slice(t * tb, (t + 1) * tb)
                diag = jnp.where(lower, s_own[r, c], -MASK_BIG)
                parts = (([s_own[r, :t * tb]] if t else []) + [diag]
                         + ([s_own[r, (t + 1) * tb:]] if (t + 1) * tb < tkv else []))
                tiles.append(jnp.concatenate(parts, axis=1))
            visits.append((rows, jnp.concatenate(tiles, axis=0)))
        visits += [(group_rows(first + u), scores(first + u)) for u in range(1, extra + 1)]
        for rows, s in visits:
            update(rows, s)
        acc = acc_ref[pl.ds(pl.multiple_of(j * tkv, tkv), tkv), :]
        o_ref[...] = (acc[:, :MOBA_HD] / acc[:, MOBA_HD:]).astype(BF16)
        per_block = MOBA_BLOCK // PAGE_SIZE
        for blk in range(len(pages) // per_block):
            tot = page_sum(pages[blk * per_block])
            for p in range(1, per_block):
                tot = tot + page_sum(pages[blk * per_block + p])
            ksum_ref[0, blk] = tot

    for extra in range(MOBA_VISITS):
        pl.when(n_extra == extra)(functools.partial(first_groups, extra))

    def body(i, carry):
        g0 = first + 1 + n_extra + MOBA_VISITS * i
        ss = [scores(g0 + u) for u in range(MOBA_VISITS)]
        for u, s in enumerate(ss):
            update(group_rows(g0 + u), s)
        return carry

    lax.fori_loop(0, n_later // MOBA_VISITS, body, 0)


def _moba_attend(qe, k, v, batch, seq, cache, layer, page_table_flat):
    tkv = MOBA_BLOCK * MOBA_KV_BLOCKS
    nb = seq // tkv
    n_steps = batch * MOBA_HEADS * nb
    per_block = MOBA_BLOCK // PAGE_SIZE
    pages_per_step = page_table_flat.shape[0] // n_steps
    assert pages_per_step * n_steps == page_table_flat.shape[0] and pages_per_step % per_block == 0
    kv = pl.BlockSpec((tkv, MOBA_HD), lambda b, h, j, pt: (b * nb + j, h))

    def step(b, h, j):
        return (b * MOBA_HEADS + h) * nb + j

    def page_spec(i):
        return pl.BlockSpec((None, None, PAGE_SIZE, MOBA_HEADS, MOBA_HD),
                            lambda b, h, j, pt: (layer, pt[step(b, h, j) * pages_per_step + i], 0, 0, 0))

    blocks_per_step = pages_per_step // per_block
    need = (2 * seq * 2 * MOBA_HD * 2 + 3 * seq * MOBA_HD * 4 + 2 * pages_per_step * PAGE_SIZE * MOBA_W * 4
            + (16 << 20))
    out, sums = pl.pallas_call(
        _moba_attend_body,
        grid_spec=pltpu.PrefetchScalarGridSpec(
            num_scalar_prefetch=1,
            grid=(batch, MOBA_HEADS, nb),
            in_specs=[pl.BlockSpec((1, 1, seq, 2 * MOBA_HD), lambda b, h, j, pt: (b, h, 0, 0)), kv, kv]
                     + [page_spec(i) for i in range(pages_per_step)],
            out_specs=[kv, pl.BlockSpec((1, blocks_per_step, MOBA_HEADS, MOBA_HD),
                                        lambda b, h, j, pt: (step(b, h, j), 0, 0, 0))],
            scratch_shapes=[pltpu.VMEM((seq, 2 * MOBA_HD), F32), pltpu.VMEM((seq, MOBA_HD), F32)],
        ),
        out_shape=[jax.ShapeDtypeStruct((batch * seq, MOBA_W), BF16),
                   jax.ShapeDtypeStruct((n_steps, blocks_per_step, MOBA_HEADS, MOBA_HD), F32)],
        compiler_params=pltpu.CompilerParams(dimension_semantics=("parallel", "parallel", "arbitrary"),
                                             vmem_limit_bytes=_vmem_limit(need)),
        name="moba_prompt_attention",
    )(page_table_flat, qe, k, v, *([cache] * pages_per_step))
    return out, sums


def _merge_body(x_ref, g_ref, oa_ref, ob_ref, wga, wgb, wpa, wpb, wo, o_ref):
    x = x_ref[...]
    h = _rms_norm(x, g_ref[...]).astype(BF16)
    mix = jax.nn.sigmoid(_dot(h, wga[...])) * _dot(oa_ref[...], wpa[...])
    mix = mix + jax.nn.sigmoid(_dot(h, wgb[...])) * _dot(ob_ref[...], wpb[...])
    o_ref[...] = x + _dot(mix.astype(BF16), wo[...])


def _merge(x, mix_g, o_gla, o_moba, w):
    m = x.shape[0]
    tm = min(FFN_ROWS, m)
    row32 = pl.BlockSpec((tm, D_MODEL), lambda i: (i, 0))
    sq = _resident((D_MODEL, D_MODEL))
    need = 5 * D_MODEL * D_MODEL * 2 + 4 * tm * D_MODEL * 4 + 4 * tm * D_MODEL * 2 + 6 * tm * D_MODEL * 4
    return pl.pallas_call(
        _merge_body,
        grid=(m // tm,),
        in_specs=[row32, _resident((1, D_MODEL)), row32, row32, sq, sq, sq, sq, sq],
        out_specs=row32,
        out_shape=jax.ShapeDtypeStruct((m, D_MODEL), F32),
        compiler_params=pltpu.CompilerParams(dimension_semantics=("parallel",),
                                             vmem_limit_bytes=_vmem_limit(need)),
        name="gated_merge_projection",
    )(x, mix_g, o_gla, o_moba, w["ga"], w["gb"], w["pa"], w["pb"], w["o"])


def _gla_sample_body(q_ref, k_ref, la_ref, v_ref, gg_ref, s_ref, ng_ref, o_ref, so_ref):
    def column(ref, s, hd):
        row = ref[s, :, hd * GLA_HDK:(hd + 1) * GLA_HDK]
        col = jnp.broadcast_to(row, (GLA_HDK, GLA_HDK)).T
        return jnp.concatenate([col] * (GLA_HDV // GLA_HDK), axis=1)

    for s, hd in [(ss, hh) for ss in range(q_ref.shape[0]) for hh in range(GLA_HEADS)]:
        vs = slice(hd * GLA_HDV, (hd + 1) * GLA_HDV)
        v = v_ref[s, :, vs].astype(F32)
        s_new = jnp.exp(column(la_ref, s, hd)) * s_ref[s, hd] + column(k_ref, s, hd) * v
        so_ref[s, hd] = s_new
        out = jnp.sum(column(q_ref, s, hd) * s_new, axis=0, keepdims=True)
        gate = gg_ref[s, :, vs].astype(F32)
        o_ref[s, :, vs] = (_rms_norm(out, ng_ref[...]) * (gate * jax.nn.sigmoid(gate))).astype(BF16)


def _gla_sample(gq, gk, la, gv, gg, state, norm_g):
    n = gq.shape[0]
    assert GLA_HDV % GLA_HDK == 0
    per_step = 4 if n % 4 == 0 else 1
    key_row = pl.BlockSpec((per_step, 1, GLA_DK), lambda i: (i, 0, 0))
    val_row = pl.BlockSpec((per_step, 1, GLA_DV), lambda i: (i, 0, 0))
    st = pl.BlockSpec((per_step, GLA_HEADS, GLA_HDK, GLA_HDV), lambda i: (i, 0, 0, 0))
    as_row = lambda a: a.reshape(n, 1, a.shape[-1])
    o, s_new = pl.pallas_call(
        _gla_sample_body,
        grid=(n // per_step,),
        in_specs=[key_row, key_row, key_row, val_row, val_row, st, _resident((1, GLA_HDV))],
        out_specs=[val_row, st],
        out_shape=[jax.ShapeDtypeStruct((n, 1, GLA_DV), BF16),
                   jax.ShapeDtypeStruct((n, GLA_HEADS, GLA_HDK, GLA_HDV), F32)],
        compiler_params=pltpu.CompilerParams(dimension_semantics=("parallel",),
                                             vmem_limit_bytes=_vmem_limit(16 << 20)),
        name="gla_sample_step",
    )(as_row(gq), as_row(gk), as_row(la), as_row(gv), as_row(gg), state, norm_g)
    return o.reshape(n, GLA_DV), s_new


def _sample_select_body(q_ref, ksum_ref, sel_ref):
    nb = ksum_ref.shape[1]
    lane = lax.broadcasted_iota(jnp.int32, (nb, LANES), 1)
    row = lax.broadcasted_iota(jnp.int32, (nb, LANES), 0).astype(F32)
    out_row = lax.broadcasted_iota(jnp.int32, (SUBLANES, LANES), 0)
    for s in range(q_ref.shape[0]):
        q = q_ref[s]
        gate = jnp.full((nb, LANES), -jnp.inf, F32)
        for hd in range(MOBA_HEADS):
            sl = slice(hd * MOBA_HD, (hd + 1) * MOBA_HD)
            g = jnp.sum(ksum_ref[s, :, sl] * (1.0 / MOBA_BLOCK) * q[:, sl], axis=1, keepdims=True)
            gate = jnp.where(lane == hd, g, gate)
        out = jnp.zeros((SUBLANES, LANES), jnp.int32)
        for r in range(MOBA_TOPK):
            best = jnp.max(gate, axis=0, keepdims=True)
            first = jnp.min(jnp.where(gate == best, row, float(nb)), axis=0, keepdims=True)
            out = jnp.where(out_row == r, first.astype(jnp.int32), out)
            gate = jnp.where(row == first, -jnp.inf, gate)
        sel_ref[s] = out


def _sample_select(q, ksum):
    n, nb = ksum.shape[0], ksum.shape[1]
    per_step = SUBLANES if n % SUBLANES == 0 else 1
    sel = pl.pallas_call(
        _sample_select_body,
        grid=(n // per_step,),
        in_specs=[pl.BlockSpec((per_step, 1, MOBA_W), lambda i: (i, 0, 0)),
                  pl.BlockSpec((per_step, nb, MOBA_W), lambda i: (i, 0, 0))],
        out_specs=pl.BlockSpec((per_step, SUBLANES, LANES), lambda i: (i, 0, 0)),
        out_shape=jax.ShapeDtypeStruct((n, SUBLANES, LANES), jnp.int32),
        compiler_params=pltpu.CompilerParams(dimension_semantics=("parallel",)),
        name="moba_sample_select",
    )(q.reshape(n, 1, MOBA_W), ksum)
    return jnp.transpose(sel[:, :MOBA_TOPK, :MOBA_HEADS], (0, 2, 1)).reshape(-1)


def _sample_attend_body(pt_ref, sel_ref, q_ref, kn_ref, vn_ref, ck_hbm, cv_hbm, o_ref, k_buf, v_buf, sem,
                        *, layer, n_pages):
    per_block = MOBA_BLOCK // PAGE_SIZE
    n_sel = MOBA_TOPK * per_block
    n_slots = MOBA_HEADS * n_sel
    seq = pl.program_id(0)
    scale = MOBA_HD ** -0.5

    def page_copies(s, hd, i):
        blk = sel_ref[(s * MOBA_HEADS + hd) * MOBA_TOPK + i // per_block]
        page = pt_ref[s * n_pages + blk * per_block + i % per_block]
        slot = (s % 2) * n_slots + hd * n_sel + i
        return (pltpu.make_async_copy(ck_hbm.at[layer, page, :, hd, :], k_buf.at[slot], sem.at[0, slot]),
                pltpu.make_async_copy(cv_hbm.at[layer, page, :, hd, :], v_buf.at[slot], sem.at[1, slot]))

    def start_fetch(s):
        for hd in range(MOBA_HEADS):
            for i in range(n_sel):
                for cp in page_copies(s, hd, i):
                    cp.start()

    @pl.when(seq == 0)
    def _():
        start_fetch(seq)

    @pl.when(seq + 1 < pl.num_programs(0))
    def _():
        start_fetch(seq + 1)

    for hd in range(MOBA_HEADS):
        for i in range(n_sel):
            for cp in page_copies(seq, hd, i):
                cp.wait()

    base = (seq % 2) * n_slots
    for hd in range(MOBA_HEADS):
        sl = slice(hd * MOBA_HD, (hd + 1) * MOBA_HD)
        q = q_ref[0, :, sl]
        s_new = jnp.sum(kn_ref[0, :, sl] * q, axis=1, keepdims=True) * scale
        scores = [jnp.sum(k_buf[base + hd * n_sel + i] * q, axis=1, keepdims=True) * scale for i in range(n_sel)]
        top = s_new
        for s in scores:
            top = jnp.maximum(top, jnp.max(s, axis=0, keepdims=True))
        p_new = jnp.exp(s_new - top)
        denom = p_new
        acc = p_new * vn_ref[0, :, sl]
        for i, s in enumerate(scores):
            p = jnp.exp(s - top)
            denom = denom + jnp.sum(p, axis=0, keepdims=True)
            acc = acc + jnp.sum(p * v_buf[base + hd * n_sel + i], axis=0, keepdims=True)
        o_ref[0, :, sl] = (acc / denom).astype(BF16)


def _sample_attend(q, k_new, v_new, cache_k, cache_v, layer, page_table_flat, sel_flat, n_pages):
    n = q.shape[0]
    n_slots = MOBA_HEADS * MOBA_TOPK * (MOBA_BLOCK // PAGE_SIZE)
    tok = pl.BlockSpec((1, 1, MOBA_W), lambda s, pt, sel: (s, 0, 0))
    hbm = pl.BlockSpec(memory_space=pl.ANY)
    as_tok = lambda a: a.reshape(n, 1, MOBA_W)
    out = pl.pallas_call(
        functools.partial(_sample_attend_body, layer=layer, n_pages=n_pages),
        grid_spec=pltpu.PrefetchScalarGridSpec(
            num_scalar_prefetch=2,
            grid=(n,),
            in_specs=[tok, tok, tok, hbm, hbm],
            out_specs=tok,
            scratch_shapes=[pltpu.VMEM((2 * n_slots, PAGE_SIZE, MOBA_HD), F32),
                            pltpu.VMEM((2 * n_slots, PAGE_SIZE, MOBA_HD), F32),
                            pltpu.SemaphoreType.DMA((2, 2 * n_slots))],
        ),
        out_shape=jax.ShapeDtypeStruct((n, 1, MOBA_W), BF16),
        compiler_params=pltpu.CompilerParams(dimension_semantics=("arbitrary",),
                                             vmem_limit_bytes=_vmem_limit(5 * n_slots * PAGE_SIZE * MOBA_HD * 4)),
        name="moba_sample_attention",
    )(page_table_flat, sel_flat, as_tok(q), as_tok(k_new), as_tok(v_new), cache_k, cache_v)
    return out.reshape(n, MOBA_W)


def _rope_tables(pos):
    half = MOBA_HD // 2
    inv = ROPE_THETA ** (-jnp.arange(half, dtype=F32) / half)
    ang = pos.astype(F32)[:, None] * inv[None, :]
    cos, sin = jnp.cos(ang), jnp.sin(ang)
    return jnp.concatenate([cos, cos], axis=1), jnp.concatenate([-sin, sin], axis=1)


def _layer_weights(w_in, w_a2, b_a, q_norm_g, k_norm_g, w_pa, w_pb, w_o):
    offs = np.concatenate([[0], np.cumsum(IN_SPLITS)])
    gq, gk, gv, gg, glr, mq, mk, mv, ga, gb = (w_in[:, offs[i]:offs[i + 1]] for i in range(len(IN_SPLITS)))
    glr = jnp.pad(glr, ((0, 0), (0, LANES - GLA_RANK)))
    a2 = jnp.pad(w_a2, ((0, LANES - GLA_RANK), (0, 0)))
    b16 = lambda a: a.astype(BF16)
    return dict(
        gq=b16(gq), gk=b16(gk), gv=b16(gv), gg=b16(gg), glr=b16(glr), a2=b16(a2), ba=b_a.reshape(1, GLA_DK),
        mq=b16(mq), mk=b16(mk), mv=b16(mv),
        qg=jnp.tile(q_norm_g, MOBA_HEADS).reshape(1, MOBA_W), kg=jnp.tile(k_norm_g, MOBA_HEADS).reshape(1, MOBA_W),
        ga=b16(ga), gb=b16(gb), pa=b16(w_pa), pb=b16(w_pb), o=b16(w_o))


def kernel(x_prompt, x_sample, cache_k, cache_v, state_gla, page_table, ffn1_g, ffn1_wg, ffn1_wu, ffn1_wd, mix_g, w_in, w_a2, b_a, gla_norm_g, q_norm_g, k_norm_g, w_pa, w_pb, w_o, ffn2_g, ffn2_wg, ffn2_wu, ffn2_wd):
    batch, seq, _ = x_prompt.shape
    n_dec, dec_seq, _ = x_sample.shape
    n_pages = page_table.shape[1]
    depth = w_in.shape[0]
    past_len = n_pages * PAGE_SIZE
    assert dec_seq == 1 and seq % MOBA_BLOCK == 0 and past_len % MOBA_BLOCK == 0
    assert past_len // MOBA_BLOCK >= MOBA_TOPK
    assert (seq // MOBA_BLOCK) % MOBA_GROUP == 0 and MOBA_GROUP == MOBA_KV_BLOCKS

    yp = x_prompt.reshape(batch * seq, D_MODEL)
    ys = x_sample.reshape(n_dec, D_MODEL)
    rope_p = _rope_tables(jnp.arange(seq, dtype=jnp.int32))
    rope_s = _rope_tables(jnp.full((n_dec,), past_len, jnp.int32))
    pt_flat = page_table.reshape(-1)
    row = lambda a: a.reshape(1, -1)
    outs = [[] for _ in range(6)]
    for l in range(depth):
        w = _layer_weights(w_in[l], w_a2[l], b_a[l], q_norm_g[l], k_norm_g[l], w_pa[l], w_pb[l], w_o[l])
        ffn1 = (row(ffn1_g[l]), ffn1_wg[l].astype(BF16), ffn1_wu[l].astype(BF16), ffn1_wd[l].astype(BF16))
        ffn2 = (row(ffn2_g[l]), ffn2_wg[l].astype(BF16), ffn2_wu[l].astype(BF16), ffn2_wd[l].astype(BF16))
        norm_g = row(gla_norm_g[l])

        x1 = _ffn(yp, *ffn1)
        gq, gk, gv, gg, la = _gla_proj(x1, row(mix_g[l]), w)
        o_gla, s_prompt = _gla_prompt(gq, gk, la, gv, gg, norm_g, batch, seq)
        _, k, v, qe = _moba_proj(x1, row(mix_g[l]), w, *rope_p, n_seq=batch, select=True)
        o_moba, page_sums = _moba_attend(qe, k, v, batch, seq, cache_k, l, pt_flat)
        yp = _ffn(_merge(x1, row(mix_g[l]), o_gla, o_moba, w), *ffn2)
        outs[0].append(k.reshape(batch, seq, MOBA_HEADS, MOBA_HD))
        outs[1].append(v.reshape(batch, seq, MOBA_HEADS, MOBA_HD))
        outs[2].append(s_prompt)

        x1 = _ffn(ys, *ffn1)
        gq, gk, gv, gg, la = _gla_proj(x1, row(mix_g[l]), w)
        q, k, v, _ = _moba_proj(x1, row(mix_g[l]), w, *rope_s, n_seq=1, select=False)
        o_gla, s_sample = _gla_sample(gq, gk, la, gv, gg, state_gla[l], norm_g)
        sel = _sample_select(q, page_sums.reshape(n_dec, n_pages * PAGE_SIZE // MOBA_BLOCK, MOBA_W))
        o_moba = _sample_attend(q, k, v, cache_k, cache_v, l, pt_flat, sel, n_pages)
        ys = _ffn(_merge(x1, row(mix_g[l]), o_gla, o_moba, w), *ffn2)
        outs[3].append(k.reshape(n_dec, 1, MOBA_HEADS, MOBA_HD))
        outs[4].append(v.reshape(n_dec, 1, MOBA_HEADS, MOBA_HD))
        outs[5].append(s_sample)

    return (yp.reshape(batch, seq, D_MODEL), ys.reshape(n_dec, 1, D_MODEL),
            jnp.stack(outs[0]), jnp.stack(outs[1]), jnp.stack(outs[2]),
            jnp.stack(outs[3]), jnp.stack(outs[4]), jnp.stack(outs[5]))
```

```python
import functools

import numpy as np
import jax
import jax.numpy as jnp
from jax import lax
from jax.experimental import pallas as pl
from jax.experimental.pallas import tpu as pltpu

F32 = jnp.float32
BF16 = jnp.bfloat16

D_MODEL = 1024
D_FF = 2816
GLA_HEADS = 4
GLA_HDK = 128
GLA_HDV = 256
GLA_DK = GLA_HEADS * GLA_HDK
GLA_DV = GLA_HEADS * GLA_HDV
GLA_RANK = 16
GLA_TAU = 16.0
MOBA_HEADS = 8
MOBA_HD = 128
MOBA_W = MOBA_HEADS * MOBA_HD
MOBA_BLOCK = 256
MOBA_TOPK = 3
ROPE_THETA = 10000.0
EPS = 1e-6
PAGE_SIZE = 128
IN_SPLITS = (GLA_DK, GLA_DK, GLA_DV, GLA_DV, GLA_RANK, MOBA_W, MOBA_W, MOBA_W, D_MODEL, D_MODEL)

LANES = 128
SUBLANES = 8
VMEM_BYTES = 64 * 1024 * 1024

FF_CHUNK = 256
FFN_ROWS = 512
PROJ_ROWS = 512
GLA_CHUNK = 128
GLA_CHUNKS_PER_STEP = 4
MOBA_GROUP = 4
MOBA_KV_BLOCKS = 4
MOBA_VISITS = 3
MASK_BIG = 2.0 ** 100
NEG_INIT = -1.0e38
LOG2_E = 1.4426950408889634


def _vmem_limit(nbytes):
    return int(min(VMEM_BYTES - (4 << 20), max(nbytes, 16 << 20)))


def _resident(shape):
    return pl.BlockSpec(shape, lambda *_: (0,) * len(shape), pipeline_mode=pl.Buffered(1))


def _dot(a, b):
    return jnp.dot(a, b, preferred_element_type=F32)


def _dot_nt(a, b):
    return lax.dot_general(a, b, (((1,), (1,)), ((), ())), preferred_element_type=F32)


def _rms_norm(x, g):
    return x * lax.rsqrt(jnp.mean(x * x, axis=-1, keepdims=True) + EPS) * g


def _log_sigmoid(z):
    return jnp.minimum(z, 0.0) - jnp.log1p(jnp.exp(-jnp.abs(z)))


def _split3(x):
    hi = x.astype(BF16)
    r = x - hi.astype(F32)
    mid = r.astype(BF16)
    lo = (r - mid.astype(F32)).astype(BF16)
    return hi, mid, lo


def _ffn_body(x_ref, g_ref, wg_ref, wu_ref, wd_ref, o_ref):
    x = x_ref[...]
    h = _rms_norm(x, g_ref[...]).astype(BF16)
    acc = jnp.zeros_like(x)
    for c in range(D_FF // FF_CHUNK):
        sl = slice(c * FF_CHUNK, (c + 1) * FF_CHUNK)
        a = _dot(h, wg_ref[:, sl])
        u = _dot(h, wu_ref[:, sl])
        act = (a * jax.nn.sigmoid(a) * u).astype(BF16)
        acc = acc + _dot(act, wd_ref[sl, :])
    o_ref[...] = x + 0.5 * acc


def _ffn(x, g, wg, wu, wd):
    m = x.shape[0]
    tm = min(FFN_ROWS, m)
    row = pl.BlockSpec((tm, D_MODEL), lambda i: (i, 0))
    need = 3 * D_MODEL * D_FF * 2 + 4 * tm * D_MODEL * 4 + 6 * tm * D_MODEL * 4
    return pl.pallas_call(
        _ffn_body,
        grid=(m // tm,),
        in_specs=[row, _resident((1, D_MODEL)), _resident((D_MODEL, D_FF)), _resident((D_MODEL, D_FF)),
                  _resident((D_FF, D_MODEL))],
        out_specs=row,
        out_shape=jax.ShapeDtypeStruct((m, D_MODEL), F32),
        compiler_params=pltpu.CompilerParams(dimension_semantics=("parallel",),
                                             vmem_limit_bytes=_vmem_limit(need)),
        name="swiglu_half_step",
    )(x, g, wg, wu, wd)


def _gla_proj_body(x_ref, g_ref, wgq, wgk, wgv, wgg, wglr, wa2, ba, gq_o, gk_o, gv_o, gg_o, la_o):
    h = _rms_norm(x_ref[...], g_ref[...]).astype(BF16)
    gq_o[...] = _dot(h, wgq[...]) * (GLA_HDK ** -0.5)
    gk_o[...] = _dot(h, wgk[...])
    gv_o[...] = _dot(h, wgv[...]).astype(BF16)
    gg_o[...] = _dot(h, wgg[...]).astype(BF16)
    glr = _dot(h, wglr[...]).astype(BF16)
    la_o[...] = _log_sigmoid(_dot(glr, wa2[...]) + ba[...]) * (1.0 / GLA_TAU)


def _gla_proj(x, mix_g, w):
    m = x.shape[0]
    tm = min(PROJ_ROWS, m)

    def rows(width):
        return pl.BlockSpec((tm, width), lambda i: (i, 0))

    weights = [w["gq"], w["gk"], w["gv"], w["gg"], w["glr"], w["a2"], w["ba"]]
    out_shape = [
        jax.ShapeDtypeStruct((m, GLA_DK), F32), jax.ShapeDtypeStruct((m, GLA_DK), F32),
        jax.ShapeDtypeStruct((m, GLA_DV), BF16), jax.ShapeDtypeStruct((m, GLA_DV), BF16),
        jax.ShapeDtypeStruct((m, GLA_DK), F32),
    ]
    w_bytes = sum(int(a.size) * a.dtype.itemsize for a in weights)
    need = w_bytes + 2 * tm * (3 * GLA_DK * 4 + 2 * GLA_DV * 2) + 2 * tm * D_MODEL * 4 + 6 * tm * GLA_DV * 4
    return pl.pallas_call(
        _gla_proj_body,
        grid=(m // tm,),
        in_specs=[rows(D_MODEL), _resident((1, D_MODEL))] + [_resident(a.shape) for a in weights],
        out_specs=[rows(GLA_DK), rows(GLA_DK), rows(GLA_DV), rows(GLA_DV), rows(GLA_DK)],
        out_shape=out_shape,
        compiler_params=pltpu.CompilerParams(dimension_semantics=("parallel",),
                                             vmem_limit_bytes=_vmem_limit(need)),
        name="gla_input_projection",
    )(x, mix_g, *weights)


def _moba_proj_body(x_ref, g_ref, wmq, wmk, wmv, qg, kg, rc, rs, q_o, k_o, v_o, qe_o, ksum_ref, *,
                    select, tiles_per_seq):
    tile = pl.program_id(0) % tiles_per_seq
    if select:
        @pl.when(tile == 0)
        def _():
            ksum_ref[...] = jnp.zeros_like(ksum_ref)

    h = _rms_norm(x_ref[...], g_ref[...]).astype(BF16)
    mq = _dot(h, wmq[...])
    mk = _dot(h, wmk[...])
    v_o[...] = _dot(h, wmv[...])
    cos = rc[...]
    sin = rs[...]
    for hd in range(MOBA_HEADS):
        sl = slice(hd * MOBA_HD, (hd + 1) * MOBA_HD)
        qh = _rms_norm(mq[:, sl], qg[:, sl])
        q_o[:, sl] = qh * cos + pltpu.roll(qh, MOBA_HD // 2, 1) * sin
        kh = _rms_norm(mk[:, sl], kg[:, sl])
        k_o[:, sl] = kh * cos + pltpu.roll(kh, MOBA_HD // 2, 1) * sin
    if not select:
        qe_o[...] = jnp.zeros_like(qe_o)
        return

    tq = MOBA_BLOCK
    nb = ksum_ref.shape[0]
    eye = (lax.broadcasted_iota(jnp.int32, (tq, tq), 0) == lax.broadcasted_iota(jnp.int32, (tq, tq), 1))
    eye = jnp.where(eye, 1.0, 0.0).astype(BF16)
    for c in range(q_o.shape[0] // tq):
        rows = slice(c * tq, (c + 1) * tq)
        own_i = tile * (q_o.shape[0] // tq) + c
        own = own_i.astype(F32)
        ksum_ref[pl.ds(own_i, 1), :] = jnp.sum(k_o[rows, :], axis=0, keepdims=True)
        gates = []
        for hd in range(MOBA_HEADS):
            sl = slice(hd * MOBA_HD, (hd + 1) * MOBA_HD)
            q = q_o[rows, sl]
            means = ksum_ref[:, sl] * (1.0 / MOBA_BLOCK)
            q_hi = q.astype(BF16)
            q_lo = (q - q_hi.astype(F32)).astype(BF16)
            m_hi = means.astype(BF16)
            m_lo = (means - m_hi.astype(F32)).astype(BF16)
            gates.append(_dot_nt(m_hi, q_hi) + _dot_nt(m_lo, q_hi) + _dot_nt(m_hi, q_lo))
            qe_o[0, hd, rows, :MOBA_HD] = (q * (MOBA_HD ** -0.5 * LOG2_E)).astype(BF16)
        gate = jnp.concatenate(gates, axis=1)
        blk = lax.broadcasted_iota(jnp.int32, gate.shape, 0).astype(F32)
        gate = jnp.where(blk < own, gate, -jnp.inf)
        mask = jnp.where(blk == own, 0.0, -1.0)
        for _ in range(MOBA_TOPK):
            best = jnp.max(gate, axis=0, keepdims=True)
            first = jnp.min(jnp.where(gate == best, blk, float(nb)), axis=0, keepdims=True)
            first = jnp.where(best > -jnp.inf, first, -1.0)
            pick = blk == first
            mask = jnp.where(pick, 0.0, mask)
            gate = jnp.where(pick, -jnp.inf, gate)
        mask = jnp.concatenate([mask, jnp.full((LANES - nb, mask.shape[1]), -1.0, F32)], axis=0).astype(BF16)
        for hd in range(MOBA_HEADS):
            qe_o[0, hd, rows, MOBA_HD:] = _dot_nt(eye, mask[:, hd * tq:(hd + 1) * tq]).astype(BF16)


def _moba_proj(x, mix_g, w, rope_cos, rope_sin, n_seq, select):
    m = x.shape[0]
    seq = m // n_seq
    tm = min(PROJ_ROWS, seq if select else m)
    n_rope = rope_cos.shape[0] // tm
    per_seq = seq // tm
    nb = max(seq // MOBA_BLOCK, SUBLANES)
    rows = pl.BlockSpec((tm, MOBA_W), lambda i: (i, 0))
    rope_spec = pl.BlockSpec((tm, MOBA_HD), lambda i: (i % n_rope, 0))
    weights = [w["mq"], w["mk"], w["mv"], w["qg"], w["kg"]]
    w_bytes = sum(int(a.size) * a.dtype.itemsize for a in weights)
    out_bytes = tm * MOBA_W * 3 * 4 + tm * MOBA_HEADS * 2 * MOBA_HD * 2
    need = w_bytes + 2 * out_bytes + 2 * tm * D_MODEL * 4 + 8 * tm * MOBA_W * 4 + (4 << 20)
    return pl.pallas_call(
        functools.partial(_moba_proj_body, select=select, tiles_per_seq=per_seq),
        grid=(m // tm,),
        in_specs=[pl.BlockSpec((tm, D_MODEL), lambda i: (i, 0)), _resident((1, D_MODEL))]
                 + [_resident(a.shape) for a in weights] + [rope_spec, rope_spec],
        out_specs=[rows, rows, rows,
                   pl.BlockSpec((1, MOBA_HEADS, tm, 2 * MOBA_HD), lambda i: (i // per_seq, 0, i % per_seq, 0))],
        out_shape=[jax.ShapeDtypeStruct((m, MOBA_W), F32)] * 3
                  + [jax.ShapeDtypeStruct((n_seq if select else 1, MOBA_HEADS, seq if select else m, 2 * MOBA_HD), BF16)],
        scratch_shapes=[pltpu.VMEM((nb, MOBA_W), F32)],
        compiler_params=pltpu.CompilerParams(dimension_semantics=("arbitrary",),
                                             vmem_limit_bytes=_vmem_limit(need)),
        name="moba_input_projection",
    )(x, mix_g, *weights, rope_cos, rope_sin)


def _gla_tables():
    c = GLA_CHUNK
    t = np.arange(c)
    le = t[None, :] <= t[:, None]
    gt = t[None, :] > t[:, None]
    masks = []
    s = c // 2
    while s >= 1:
        same = (t // (2 * s))[:, None] == (t // (2 * s))[None, :]
        right = (t % (2 * s)) >= s
        masks.append(same & right[:, None] & ~right[None, :])
        s //= 2
    masks.append(np.eye(c, dtype=bool))
    tail = np.concatenate([gt.T, np.ones((c, c), dtype=bool)], axis=1)
    return le.astype(np.float32), np.stack(masks).astype(np.float32), tail.astype(np.float32)


def _level_reference(b, s):
    c, dk = b.shape
    if s >= SUBLANES:
        blocks = b.reshape(c // (2 * s), 2 * s, dk)
        return jnp.broadcast_to(blocks[:, s - 1:s, :], blocks.shape).reshape(c, dk)
    rows8 = b.reshape(c // SUBLANES, SUBLANES, dk)
    sub = lax.broadcasted_iota(jnp.int32, rows8.shape, 1)
    ref = jnp.broadcast_to(rows8[:, SUBLANES - s - 1:SUBLANES - s, :], rows8.shape)
    for first in range(SUBLANES - 4 * s, -1, -2 * s):
        ref = jnp.where(sub < first + 2 * s, jnp.broadcast_to(rows8[:, first + s - 1:first + s, :], rows8.shape), ref)
    return ref.reshape(c, dk)


def _gla_body(q_ref, k_ref, la_ref, v_ref, gg_ref, ng_ref, le_ref, masks_ref, tail_ref, o_ref, s_ref):
    c = GLA_CHUNK
    n_lvl = masks_ref.shape[0] - 1

    @pl.when(pl.program_id(1) == 0)
    def _():
        s_ref[...] = jnp.zeros_like(s_ref)

    le = le_ref[...]
    tail = tail_ref[...]
    for chunk, hd in [(cc, hh) for cc in range(q_ref.shape[0] // c) for hh in range(GLA_HEADS)]:
        rows = slice(chunk * c, (chunk + 1) * c)
        ks = slice(hd * GLA_HDK, (hd + 1) * GLA_HDK)
        vs = slice(hd * GLA_HDV, (hd + 1) * GLA_HDV)
        q = q_ref[rows, ks]
        k = k_ref[rows, ks]
        v = v_ref[rows, vs]
        state = s_ref[0, hd]
        b = sum(_dot(le, p) for p in _split3(la_ref[rows, ks])) * LOG2_E
        et = jnp.exp(sum(_dot(p, tail) for p in _split3(la_ref[rows, ks].T)))
        out = _dot((q * jnp.exp2(b)).astype(BF16), state.astype(BF16))
        attn = masks_ref[n_lvl] * _dot_nt(q.astype(BF16), k.astype(BF16))
        for lv in range(n_lvl):
            e = jnp.exp2(-jnp.abs(b - _level_reference(b, c >> (lv + 1))))
            attn = attn + masks_ref[lv] * _dot_nt((q * e).astype(BF16), (k * e).astype(BF16))
        out = out + _dot(attn.astype(BF16), v)
        k_dec = (k.T * et[:, :c]).astype(BF16)
        decay = et[:, c:]
        s_ref[0, hd] = jnp.concatenate([state[:, :c] * decay, state[:, c:] * decay], axis=1) + _dot(k_dec, v)
        gate = gg_ref[rows, vs].astype(F32)
        o_ref[rows, vs] = (_rms_norm(out, ng_ref[...]) * (gate * jax.nn.sigmoid(gate))).astype(BF16)


def _gla_prompt(gq, gk, la, gv, gg, norm_g, batch, seq):
    c = GLA_CHUNK * GLA_CHUNKS_PER_STEP
    n_chunks = seq // c
    le, masks, tail = (jnp.asarray(a, BF16 if i != 1 else F32) for i, a in enumerate(_gla_tables()))

    def rows(width):
        return pl.BlockSpec((c, width), lambda b, t: (b * n_chunks + t, 0))

    need = 2 * (3 * c * GLA_DK * 4 + 3 * c * GLA_DV * 2) + 4 * GLA_DK * GLA_HDV * 4 + (8 << 20)
    return pl.pallas_call(
        _gla_body,
        grid=(batch, n_chunks),
        in_specs=[rows(GLA_DK), rows(GLA_DK), rows(GLA_DK), rows(GLA_DV), rows(GLA_DV),
                  _resident((1, GLA_HDV)), _resident(le.shape), _resident(masks.shape), _resident(tail.shape)],
        out_specs=[rows(GLA_DV),
                   pl.BlockSpec((1, GLA_HEADS, GLA_HDK, GLA_HDV), lambda b, t: (b, 0, 0, 0))],
        out_shape=[jax.ShapeDtypeStruct((batch * seq, GLA_DV), BF16),
                   jax.ShapeDtypeStruct((batch, GLA_HEADS, GLA_HDK, GLA_HDV), F32)],
        compiler_params=pltpu.CompilerParams(dimension_semantics=("parallel", "arbitrary"),
                                             vmem_limit_bytes=_vmem_limit(need)),
        name="gla_prompt_chunks",
    )(gq, gk, la, gv, gg, norm_g, le, masks, tail)


def _moba_attend_body(pt_ref, qe_ref, k_ref, v_ref, *refs):
    pages, (o_ref, ksum_ref, acc_ref, m_ref) = refs[:-4], refs[-4:]
    j = pl.program_id(2)
    tb = MOBA_BLOCK
    tkv = k_ref.shape[0]
    n_groups = qe_ref.shape[2] // (tb * MOBA_GROUP)

    @pl.when(j == 0)
    def _():
        acc_ref[...] = jnp.zeros_like(acc_ref)
        m_ref[...] = jnp.full_like(m_ref, NEG_INIT)

    k = k_ref[...].astype(BF16)
    lane = lax.broadcasted_iota(jnp.int32, (tkv, MOBA_HD), 1)
    key_block = j * (tkv // tb) + lax.broadcasted_iota(jnp.int32, (tkv, MOBA_HD), 0) // tb
    k_ext = jnp.concatenate([k, jnp.where(lane == key_block, MASK_BIG, 0.0).astype(BF16)], axis=1)
    v_ext = jnp.concatenate([v_ref[...].astype(BF16), jnp.ones((tkv, MOBA_HD), BF16)], axis=1)

    def update(rows, s):
        m_old = m_ref[rows, :]
        m_new = jnp.maximum(m_old, jnp.max(s, axis=1, keepdims=True))
        p = jnp.exp2(s - jnp.concatenate([m_new] * (tkv // MOBA_HD), axis=1)).astype(BF16)
        alpha = jnp.exp2(m_old - m_new)
        acc_new = jnp.concatenate([alpha, alpha], axis=1) * acc_ref[rows, :] + _dot(p, v_ext)
        m_ref[rows, :] = m_new
        acc_ref[rows, :] = acc_new
        return acc_new

    rows_per_group = tb * MOBA_GROUP

    def group_rows(g):
        return pl.ds(pl.multiple_of(g * rows_per_group, rows_per_group), rows_per_group)

    def scores(g):
        return _dot_nt(qe_ref[0, 0, group_rows(g), :], k_ext)

    first = (j * tkv) // rows_per_group
    n_later = n_groups - 1 - first
    n_extra = n_later % MOBA_VISITS

    def page_sum(page_ref):
        return jnp.sum(jnp.sum(page_ref[...].reshape(4, PAGE_SIZE // 4, MOBA_HEADS, MOBA_HD), axis=1), axis=0)

    def first_groups(extra):
        lower = (lax.broadcasted_iota(jnp.int32, (tb, tb), 1) <= lax.broadcasted_iota(jnp.int32, (tb, tb), 0))
        n_parts = 1 if extra else 2
        part = rows_per_group // n_parts
        visits = []
        for hh in range(n_parts):
            rows = pl.ds(pl.multiple_of(first * rows_per_group + hh * part, part), part)
            s_own = _dot_nt(qe_ref[0, 0, rows, :], k_ext)
            tiles = []
            for t2 in range(part // tb):
                t = hh * (part // tb) + t2
                r, c = slice(t2 * tb, (t2 + 1) * tb), slice(t * tb, (t + 1) * tb)
                diag = jnp.where(lower, s_own[r, c], -MASK_BIG)
                parts = (([s_own[r, :t * tb]] if t else []) + [diag]
                         + ([s_own[r, (t + 1) * tb:]] if (t + 1) * tb < tkv else []))
                tiles.append(jnp.concatenate(parts, axis=1))
            visits.append((rows, jnp.concatenate(tiles, axis=0)))
        visits += [(group_rows(first + u), scores(first + u)) for u in range(1, extra + 1)]
        for rows, s in visits:
            update(rows, s)
        acc = acc_ref[pl.ds(pl.multiple_of(j * tkv, tkv), tkv), :]
        o_ref[...] = (acc[:, :MOBA_HD] / acc[:, MOBA_HD:]).astype(BF16)
        per_block = MOBA_BLOCK // PAGE_SIZE
        for blk in range(len(pages) // per_block):
            tot = page_sum(pages[blk * per_block])
            for p in range(1, per_block):
                tot = tot + page_sum(pages[blk * per_block + p])
            ksum_ref[0, blk] = tot

    for extra in range(MOBA_VISITS):
        pl.when(n_extra == extra)(functools.partial(first_groups, extra))

    def body(i, carry):
        g0 = first + 1 + n_extra + MOBA_VISITS * i
        ss = [scores(g0 + u) for u in range(MOBA_VISITS)]
        for u, s in enumerate(ss):
            update(group_rows(g0 + u), s)
        return carry

    lax.fori_loop(0, n_later // MOBA_VISITS, body, 0)


def _moba_attend(qe, k, v, batch, seq, cache, layer, page_table_flat):
    tkv = MOBA_BLOCK * MOBA_KV_BLOCKS
    nb = seq // tkv
    n_steps = batch * MOBA_HEADS * nb
    per_block = MOBA_BLOCK // PAGE_SIZE
    pages_per_step = page_table_flat.shape[0] // n_steps
    assert pages_per_step * n_steps == page_table_flat.shape[0] and pages_per_step % per_block == 0
    kv = pl.BlockSpec((tkv, MOBA_HD), lambda b, h, j, pt: (b * nb + j, h))

    def step(b, h, j):
        return (b * MOBA_HEADS + h) * nb + j

    def page_spec(i):
        return pl.BlockSpec((None, None, PAGE_SIZE, MOBA_HEADS, MOBA_HD),
                            lambda b, h, j, pt: (layer, pt[step(b, h, j) * pages_per_step + i], 0, 0, 0))

    blocks_per_step = pages_per_step // per_block
    need = (2 * seq * 2 * MOBA_HD * 2 + 3 * seq * MOBA_HD * 4 + 2 * pages_per_step * PAGE_SIZE * MOBA_W * 4
            + (16 << 20))
    out, sums = pl.pallas_call(
        _moba_attend_body,
        grid_spec=pltpu.PrefetchScalarGridSpec(
            num_scalar_prefetch=1,
            grid=(batch, MOBA_HEADS, nb),
            in_specs=[pl.BlockSpec((1, 1, seq, 2 * MOBA_HD), lambda b, h, j, pt: (b, h, 0, 0)), kv, kv]
                     + [page_spec(i) for i in range(pages_per_step)],
            out_specs=[kv, pl.BlockSpec((1, blocks_per_step, MOBA_HEADS, MOBA_HD),
                                        lambda b, h, j, pt: (step(b, h, j), 0, 0, 0))],
            scratch_shapes=[pltpu.VMEM((seq, 2 * MOBA_HD), F32), pltpu.VMEM((seq, MOBA_HD), F32)],
        ),
        out_shape=[jax.ShapeDtypeStruct((batch * seq, MOBA_W), BF16),
                   jax.ShapeDtypeStruct((n_steps, blocks_per_step, MOBA_HEADS, MOBA_HD), F32)],
        compiler_params=pltpu.CompilerParams(dimension_semantics=("parallel", "parallel", "arbitrary"),
                                             vmem_limit_bytes=_vmem_limit(need)),
        name="moba_prompt_attention",
    )(page_table_flat, qe, k, v, *([cache] * pages_per_step))
    return out, sums


def _merge_body(x_ref, g_ref, oa_ref, ob_ref, wga, wgb, wpa, wpb, wo, o_ref):
    x = x_ref[...]
    h = _rms_norm(x, g_ref[...]).astype(BF16)
    mix = jax.nn.sigmoid(_dot(h, wga[...])) * _dot(oa_ref[...], wpa[...])
    mix = mix + jax.nn.sigmoid(_dot(h, wgb[...])) * _dot(ob_ref[...], wpb[...])
    o_ref[...] = x + _dot(mix.astype(BF16), wo[...])


def _merge(x, mix_g, o_gla, o_moba, w):
    m = x.shape[0]
    tm = min(FFN_ROWS, m)
    row32 = pl.BlockSpec((tm, D_MODEL), lambda i: (i, 0))
    sq = _resident((D_MODEL, D_MODEL))
    need = 5 * D_MODEL * D_MODEL * 2 + 4 * tm * D_MODEL * 4 + 4 * tm * D_MODEL * 2 + 6 * tm * D_MODEL * 4
    return pl.pallas_call(
        _merge_body,
        grid=(m // tm,),
        in_specs=[row32, _resident((1, D_MODEL)), row32, row32, sq, sq, sq, sq, sq],
        out_specs=row32,
        out_shape=jax.ShapeDtypeStruct((m, D_MODEL), F32),
        compiler_params=pltpu.CompilerParams(dimension_semantics=("parallel",),
                                             vmem_limit_bytes=_vmem_limit(need)),
        name="gated_merge_projection",
    )(x, mix_g, o_gla, o_moba, w["ga"], w["gb"], w["pa"], w["pb"], w["o"])


def _gla_sample_body(q_ref, k_ref, la_ref, v_ref, gg_ref, s_ref, ng_ref, o_ref, so_ref):
    def column(ref, s, hd):
        row = ref[s, :, hd * GLA_HDK:(hd + 1) * GLA_HDK]
        col = jnp.broadcast_to(row, (GLA_HDK, GLA_HDK)).T
        return jnp.concatenate([col] * (GLA_HDV // GLA_HDK), axis=1)

    for s, hd in [(ss, hh) for ss in range(q_ref.shape[0]) for hh in range(GLA_HEADS)]:
        vs = slice(hd * GLA_HDV, (hd + 1) * GLA_HDV)
        v = v_ref[s, :, vs].astype(F32)
        s_new = jnp.exp(column(la_ref, s, hd)) * s_ref[s, hd] + column(k_ref, s, hd) * v
        so_ref[s, hd] = s_new
        out = jnp.sum(column(q_ref, s, hd) * s_new, axis=0, keepdims=True)
        gate = gg_ref[s, :, vs].astype(F32)
        o_ref[s, :, vs] = (_rms_norm(out, ng_ref[...]) * (gate * jax.nn.sigmoid(gate))).astype(BF16)


def _gla_sample(gq, gk, la, gv, gg, state, norm_g):
    n = gq.shape[0]
    assert GLA_HDV % GLA_HDK == 0
    per_step = 4 if n % 4 == 0 else 1
    key_row = pl.BlockSpec((per_step, 1, GLA_DK), lambda i: (i, 0, 0))
    val_row = pl.BlockSpec((per_step, 1, GLA_DV), lambda i: (i, 0, 0))
    st = pl.BlockSpec((per_step, GLA_HEADS, GLA_HDK, GLA_HDV), lambda i: (i, 0, 0, 0))
    as_row = lambda a: a.reshape(n, 1, a.shape[-1])
    o, s_new = pl.pallas_call(
        _gla_sample_body,
        grid=(n // per_step,),
        in_specs=[key_row, key_row, key_row, val_row, val_row, st, _resident((1, GLA_HDV))],
        out_specs=[val_row, st],
        out_shape=[jax.ShapeDtypeStruct((n, 1, GLA_DV), BF16),
                   jax.ShapeDtypeStruct((n, GLA_HEADS, GLA_HDK, GLA_HDV), F32)],
        compiler_params=pltpu.CompilerParams(dimension_semantics=("parallel",),
                                             vmem_limit_bytes=_vmem_limit(16 << 20)),
        name="gla_sample_step",
    )(as_row(gq), as_row(gk), as_row(la), as_row(gv), as_row(gg), state, norm_g)
    return o.reshape(n, GLA_DV), s_new


def _sample_select_body(q_ref, ksum_ref, sel_ref):
    nb = ksum_ref.shape[1]
    lane = lax.broadcasted_iota(jnp.int32, (nb, LANES), 1)
    row = lax.broadcasted_iota(jnp.int32, (nb, LANES), 0).astype(F32)
    out_row = lax.broadcasted_iota(jnp.int32, (SUBLANES, LANES), 0)
    for s in range(q_ref.shape[0]):
        q = q_ref[s]
        gate = jnp.full((nb, LANES), -jnp.inf, F32)
        for hd in range(MOBA_HEADS):
            sl = slice(hd * MOBA_HD, (hd + 1) * MOBA_HD)
            g = jnp.sum(ksum_ref[s, :, sl] * (1.0 / MOBA_BLOCK) * q[:, sl], axis=1, keepdims=True)
            gate = jnp.where(lane == hd, g, gate)
        out = jnp.zeros((SUBLANES, LANES), jnp.int32)
        for r in range(MOBA_TOPK):
            best = jnp.max(gate, axis=0, keepdims=True)
            first = jnp.min(jnp.where(gate == best, row, float(nb)), axis=0, keepdims=True)
            out = jnp.where(out_row == r, first.astype(jnp.int32), out)
            gate = jnp.where(row == first, -jnp.inf, gate)
        sel_ref[s] = out


def _sample_select(q, ksum):
    n, nb = ksum.shape[0], ksum.shape[1]
    per_step = SUBLANES if n % SUBLANES == 0 else 1
    sel = pl.pallas_call(
        _sample_select_body,
        grid=(n // per_step,),
        in_specs=[pl.BlockSpec((per_step, 1, MOBA_W), lambda i: (i, 0, 0)),
                  pl.BlockSpec((per_step, nb, MOBA_W), lambda i: (i, 0, 0))],
        out_specs=pl.BlockSpec((per_step, SUBLANES, LANES), lambda i: (i, 0, 0)),
        out_shape=jax.ShapeDtypeStruct((n, SUBLANES, LANES), jnp.int32),
        compiler_params=pltpu.CompilerParams(dimension_semantics=("parallel",)),
        name="moba_sample_select",
    )(q.reshape(n, 1, MOBA_W), ksum)
    return jnp.transpose(sel[:, :MOBA_TOPK, :MOBA_HEADS], (0, 2, 1)).reshape(-1)


def _sample_attend_body(pt_ref, sel_ref, q_ref, kn_ref, vn_ref, ck_hbm, cv_hbm, o_ref, k_buf, v_buf, sem,
                        *, layer, n_pages):
    per_block = MOBA_BLOCK // PAGE_SIZE
    n_sel = MOBA_TOPK * per_block
    n_slots = MOBA_HEADS * n_sel
    seq = pl.program_id(0)
    scale = MOBA_HD ** -0.5

    def page_copies(s, hd, i):
        blk = sel_ref[(s * MOBA_HEADS + hd) * MOBA_TOPK + i // per_block]
        page = pt_ref[s * n_pages + blk * per_block + i % per_block]
        slot = (s % 2) * n_slots + hd * n_sel + i
        return (pltpu.make_async_copy(ck_hbm.at[layer, page, :, hd, :], k_buf.at[slot], sem.at[0, slot]),
                pltpu.make_async_copy(cv_hbm.at[layer, page, :, hd, :], v_buf.at[slot], sem.at[1, slot]))

    def start_fetch(s):
        for hd in range(MOBA_HEADS):
            for i in range(n_sel):
                for queue, cp in enumerate(page_copies(s, hd, i)):
                    cp.start(priority=queue)

    @pl.when(seq == 0)
    def _():
        start_fetch(seq)

    @pl.when(seq + 1 < pl.num_programs(0))
    def _():
        start_fetch(seq + 1)

    for hd in range(MOBA_HEADS):
        for i in range(n_sel):
            for cp in page_copies(seq, hd, i):
                cp.wait()

    base = (seq % 2) * n_slots
    for hd in range(MOBA_HEADS):
        sl = slice(hd * MOBA_HD, (hd + 1) * MOBA_HD)
        q = q_ref[0, :, sl]
        s_new = jnp.sum(kn_ref[0, :, sl] * q, axis=1, keepdims=True) * scale
        scores = [jnp.sum(k_buf[base + hd * n_sel + i] * q, axis=1, keepdims=True) * scale for i in range(n_sel)]
        top = s_new
        for s in scores:
            top = jnp.maximum(top, jnp.max(s, axis=0, keepdims=True))
        p_new = jnp.exp(s_new - top)
        denom = p_new
        acc = p_new * vn_ref[0, :, sl]
        for i, s in enumerate(scores):
            p = jnp.exp(s - top)
            denom = denom + jnp.sum(p, axis=0, keepdims=True)
            acc = acc + jnp.sum(p * v_buf[base + hd * n_sel + i], axis=0, keepdims=True)
        o_ref[0, :, sl] = (acc / denom).astype(BF16)


def _sample_attend(q, k_new, v_new, cache_k, cache_v, layer, page_table_flat, sel_flat, n_pages):
    n = q.shape[0]
    n_slots = MOBA_HEADS * MOBA_TOPK * (MOBA_BLOCK // PAGE_SIZE)
    tok = pl.BlockSpec((1, 1, MOBA_W), lambda s, pt, sel: (s, 0, 0))
    hbm = pl.BlockSpec(memory_space=pl.ANY)
    as_tok = lambda a: a.reshape(n, 1, MOBA_W)
    out = pl.pallas_call(
        functools.partial(_sample_attend_body, layer=layer, n_pages=n_pages),
        grid_spec=pltpu.PrefetchScalarGridSpec(
            num_scalar_prefetch=2,
            grid=(n,),
            in_specs=[tok, tok, tok, hbm, hbm],
            out_specs=tok,
            scratch_shapes=[pltpu.VMEM((2 * n_slots, PAGE_SIZE, MOBA_HD), F32),
                            pltpu.VMEM((2 * n_slots, PAGE_SIZE, MOBA_HD), F32),
                            pltpu.SemaphoreType.DMA((2, 2 * n_slots))],
        ),
        out_shape=jax.ShapeDtypeStruct((n, 1, MOBA_W), BF16),
        compiler_params=pltpu.CompilerParams(dimension_semantics=("arbitrary",),
                                             vmem_limit_bytes=_vmem_limit(5 * n_slots * PAGE_SIZE * MOBA_HD * 4)),
        name="moba_sample_attention",
    )(page_table_flat, sel_flat, as_tok(q), as_tok(k_new), as_tok(v_new), cache_k, cache_v)
    return out.reshape(n, MOBA_W)


def _rope_tables(pos):
    half = MOBA_HD // 2
    inv = ROPE_THETA ** (-jnp.arange(half, dtype=F32) / half)
    ang = pos.astype(F32)[:, None] * inv[None, :]
    cos, sin = jnp.cos(ang), jnp.sin(ang)
    return jnp.concatenate([cos, cos], axis=1), jnp.concatenate([-sin, sin], axis=1)


def _layer_weights(w_in, w_a2, b_a, q_norm_g, k_norm_g, w_pa, w_pb, w_o):
    offs = np.concatenate([[0], np.cumsum(IN_SPLITS)])
    gq, gk, gv, gg, glr, mq, mk, mv, ga, gb = (w_in[:, offs[i]:offs[i + 1]] for i in range(len(IN_SPLITS)))
    glr = jnp.pad(glr, ((0, 0), (0, LANES - GLA_RANK)))
    a2 = jnp.pad(w_a2, ((0, LANES - GLA_RANK), (0, 0)))
    b16 = lambda a: a.astype(BF16)
    return dict(
        gq=b16(gq), gk=b16(gk), gv=b16(gv), gg=b16(gg), glr=b16(glr), a2=b16(a2), ba=b_a.reshape(1, GLA_DK),
        mq=b16(mq), mk=b16(mk), mv=b16(mv),
        qg=jnp.tile(q_norm_g, MOBA_HEADS).reshape(1, MOBA_W), kg=jnp.tile(k_norm_g, MOBA_HEADS).reshape(1, MOBA_W),
        ga=b16(ga), gb=b16(gb), pa=b16(w_pa), pb=b16(w_pb), o=b16(w_o))


def kernel(x_prompt, x_sample, cache_k, cache_v, state_gla, page_table, ffn1_g, ffn1_wg, ffn1_wu, ffn1_wd, mix_g, w_in, w_a2, b_a, gla_norm_g, q_norm_g, k_norm_g, w_pa, w_pb, w_o, ffn2_g, ffn2_wg, ffn2_wu, ffn2_wd):
    batch, seq, _ = x_prompt.shape
    n_dec, dec_seq, _ = x_sample.shape
    n_pages = page_table.shape[1]
    depth = w_in.shape[0]
    past_len = n_pages * PAGE_SIZE
    assert dec_seq == 1 and seq % MOBA_BLOCK == 0 and past_len % MOBA_BLOCK == 0
    assert past_len // MOBA_BLOCK >= MOBA_TOPK
    assert (seq // MOBA_BLOCK) % MOBA_GROUP == 0 and MOBA_GROUP == MOBA_KV_BLOCKS

    yp = x_prompt.reshape(batch * seq, D_MODEL)
    ys = x_sample.reshape(n_dec, D_MODEL)
    rope_p = _rope_tables(jnp.arange(seq, dtype=jnp.int32))
    rope_s = _rope_tables(jnp.full((n_dec,), past_len, jnp.int32))
    pt_flat = page_table.reshape(-1)
    row = lambda a: a.reshape(1, -1)
    outs = [[] for _ in range(6)]
    for l in range(depth):
        w = _layer_weights(w_in[l], w_a2[l], b_a[l], q_norm_g[l], k_norm_g[l], w_pa[l], w_pb[l], w_o[l])
        ffn1 = (row(ffn1_g[l]), ffn1_wg[l].astype(BF16), ffn1_wu[l].astype(BF16), ffn1_wd[l].astype(BF16))
        ffn2 = (row(ffn2_g[l]), ffn2_wg[l].astype(BF16), ffn2_wu[l].astype(BF16), ffn2_wd[l].astype(BF16))
        norm_g = row(gla_norm_g[l])

        x1 = _ffn(yp, *ffn1)
        gq, gk, gv, gg, la = _gla_proj(x1, row(mix_g[l]), w)
        o_gla, s_prompt = _gla_prompt(gq, gk, la, gv, gg, norm_g, batch, seq)
        _, k, v, qe = _moba_proj(x1, row(mix_g[l]), w, *rope_p, n_seq=batch, select=True)
        o_moba, page_sums = _moba_attend(qe, k, v, batch, seq, cache_k, l, pt_flat)
        yp = _ffn(_merge(x1, row(mix_g[l]), o_gla, o_moba, w), *ffn2)
        outs[0].append(k.reshape(batch, seq, MOBA_HEADS, MOBA_HD))
        outs[1].append(v.reshape(batch, seq, MOBA_HEADS, MOBA_HD))
        outs[2].append(s_prompt)

        x1 = _ffn(ys, *ffn1)
        gq, gk, gv, gg, la = _gla_proj(x1, row(mix_g[l]), w)
        q, k, v, _ = _moba_proj(x1, row(mix_g[l]), w, *rope_s, n_seq=1, select=False)
        o_gla, s_sample = _gla_sample(gq, gk, la, gv, gg, state_gla[l], norm_g)
        sel = _sample_select(q, page_sums.reshape(n_dec, n_pages * PAGE_SIZE // MOBA_BLOCK, MOBA_W))
        o_moba = _sample_attend(q, k, v, cache_k, cache_v, l, pt_flat, sel, n_pages)
        ys = _ffn(_merge(x1, row(mix_g[l]), o_gla, o_moba, w), *ffn2)
        outs[3].append(k.reshape(n_dec, 1, MOBA_HEADS, MOBA_HD))
        outs[4].append(v.reshape(n_dec, 1, MOBA_HEADS, MOBA_HD))
        outs[5].append(s_sample)

    return (yp.reshape(batch, seq, D_MODEL), ys.reshape(n_dec, 1, D_MODEL),
            jnp.stack(outs[0]), jnp.stack(outs[1]), jnp.stack(outs[2]),
            jnp.stack(outs[3]), jnp.stack(outs[4]), jnp.stack(outs[5]))
```
